```python
import jax, jax.numpy as jnp
from jax import lax
import numpy as np

D_MODEL = 2048
BATCH = 4
SEQ = 2048
DEPTH = 1
DEC_BATCH = 2
DEC_SEQ = 4096
PAST_LEN = 128

HEAD_DIM = 128
ATTN_GROUPS = ((128, 1), (512, 4), (2048, 16))
N_GROUPS = 3
HEADS_PER_GROUP = D_MODEL // 256
ATTN_QKV = N_GROUPS * HEADS_PER_GROUP * HEAD_DIM
ATTN_OUT = HEADS_PER_GROUP * HEAD_DIM
ROT_DIM = HEAD_DIM // 4
ROPE_THETA = 500000.0
GLA_HEADS = 4
GLA_KEY = D_MODEL // 2
GLA_VAL = D_MODEL
GLA_DK = GLA_KEY // GLA_HEADS
GLA_DV = GLA_VAL // GLA_HEADS
GLA_RANK = 16
GLA_NORMALIZER = 16.0
GLA_CHUNK = 64
D_FF = 4 * D_MODEL
EPS = 1e-6
IN_SPLITS = (ATTN_QKV, ATTN_QKV, ATTN_QKV, GLA_KEY, GLA_KEY, GLA_VAL, GLA_VAL, GLA_RANK, GLA_RANK, D_MODEL, D_MODEL)
IN_COLS = 3 * ATTN_QKV + 2 * GLA_KEY + 2 * GLA_VAL + 2 * GLA_RANK + 2 * D_MODEL

kernel_name = "hybrid_dilated_attn_gla_encoder"


def rmsnorm(x, gain):
    x32 = x.astype(jnp.float32)
    y = x32 * lax.rsqrt(jnp.mean(x32 * x32, axis=-1, keepdims=True) + EPS)
    return (y * gain.astype(jnp.float32)).astype(x.dtype)


def partial_rope(x, pos):
    half = ROT_DIM // 2
    inv_freq = ROPE_THETA ** (-jnp.arange(0, ROT_DIM, 2, dtype=jnp.float32) / ROT_DIM)
    ang = pos[:, None] * inv_freq[None, :]
    ang = ang.reshape((ang.shape[0],) + (1,) * (x.ndim - 3) + (half,))
    cos, sin = jnp.cos(ang), jnp.sin(ang)
    x32 = x.astype(jnp.float32)
    x1, x2 = x32[..., :half], x32[..., half:ROT_DIM]
    out = jnp.concatenate([x1 * cos - x2 * sin, x2 * cos + x1 * sin, x32[..., ROT_DIM:]], axis=-1)
    return out.astype(x.dtype)


def dilated_window_attention(q, k, v, dilation, half):
    B, S, H, Dh = q.shape
    r = dilation
    L = S // r
    nb = -(-L // half)
    Lp = nb * half

    def to_sub(t):
        return t.astype(jnp.float32).reshape(B, L, r, H, Dh).transpose(0, 2, 1, 3, 4)

    qs, ks, vs = to_sub(q), to_sub(k), to_sub(v)
    qs = jnp.pad(qs, ((0, 0), (0, 0), (0, Lp - L), (0, 0), (0, 0))).reshape(B, r, nb, half, H, Dh)
    kpad = ((0, 0), (0, 0), (half, Lp - L + half), (0, 0), (0, 0))
    kblk = jnp.pad(ks, kpad).reshape(B, r, nb + 2, half, H, Dh)
    vblk = jnp.pad(vs, kpad).reshape(B, r, nb + 2, half, H, Dh)
    kb = jnp.concatenate([kblk[:, :, :-2], kblk[:, :, 1:-1], kblk[:, :, 2:]], axis=3)
    vb = jnp.concatenate([vblk[:, :, :-2], vblk[:, :, 1:-1], vblk[:, :, 2:]], axis=3)

    a = jnp.arange(half)[:, None]
    b = jnp.arange(3 * half)[None, :]
    rel = b - half - a
    kpos = jnp.arange(nb)[:, None, None] * half - half + b[None]
    mask = (jnp.abs(rel) <= half)[None] & (kpos >= 0) & (kpos < L)

    s = jnp.einsum('brnqhd,brnkhd->brnhqk', qs, kb) * (Dh ** -0.5)
    s = jnp.where(mask[None, None, :, None], s, -jnp.inf)
    m = jnp.max(s, axis=-1, keepdims=True)
    p = jnp.exp(s - m)
    den = jnp.sum(p, axis=-1)
    o = jnp.einsum('brnhqk,brnkhd->brnqhd', p, vb) / den.transpose(0, 1, 2, 4, 3)[..., None]
    lse = (m[..., 0] + jnp.log(den)).transpose(0, 1, 2, 4, 3)
    o = o.reshape(B, r, Lp, H, Dh)[:, :, :L].transpose(0, 2, 1, 3, 4).reshape(B, S, H, Dh)
    lse = lse.reshape(B, r, Lp, H)[:, :, :L].transpose(0, 2, 1, 3).reshape(B, S, H)
    return o, lse


def gla_direction(q, k, v, log_a, strict):
    B, S, H, DK = q.shape
    DV = v.shape[-1]
    C = GLA_CHUNK
    N = S // C

    def chunks(t):
        return t.astype(jnp.float32).reshape(B, N, C, H, t.shape[-1]).transpose(1, 0, 3, 2, 4)

    qc, kc, vc, gc = chunks(q), chunks(k), chunks(v), chunks(log_a)
    tri = jnp.tril(jnp.ones((C, C), dtype=bool), -1 if strict else 0)

    def step(state, inp):
        qi, ki, vi, gi = inp
        bcum = jnp.cumsum(gi, axis=2)
        blast = bcum[:, :, -1:, :]
        o_inter = jnp.einsum('bhck,bhkv->bhcv', qi * jnp.exp(bcum), state)
        diff = bcum[:, :, :, None, :] - bcum[:, :, None, :, :]
        decay = jnp.exp(jnp.where(tri[:, :, None], diff, -jnp.inf))
        att = jnp.einsum('bhik,bhjk,bhijk->bhij', qi, ki, decay)
        o_intra = jnp.einsum('bhij,bhjv->bhiv', att, vi)
        new_state = jnp.exp(blast)[:, :, 0, :, None] * state + jnp.einsum(
            'bhck,bhcv->bhkv', ki * jnp.exp(blast - bcum), vi)
        return new_state, o_inter + o_intra

    init = jnp.zeros((B, H, DK, DV), jnp.float32)
    _, out = lax.scan(step, init, (qc, kc, vc, gc))
    return out.transpose(1, 0, 3, 2, 4).reshape(B, S, H, DV)


def encoder_layer(x, norm_mix, w_in, q_norm, k_norm, w_gla_gate, b_gla_gate, gla_norm,
                  w_branch_attn, w_branch_gla, w_out, norm_ffn, w_ff1, w_ff2):
    B, S, _ = x.shape
    xn = rmsnorm(x, norm_mix)
    proj = xn @ w_in
    cuts = [int(c) for c in np.cumsum(IN_SPLITS)[:-1]]
    qa, ka, va, qg, kg, vg, rg, lrf, lrb, ga, gb = jnp.split(proj, cuts, axis=-1)

    def heads(t):
        return t.reshape(B, S, N_GROUPS, HEADS_PER_GROUP, HEAD_DIM)
    pos = jnp.arange(S, dtype=jnp.float32)
    qa = partial_rope(rmsnorm(heads(qa), q_norm[:, None, :]), pos)
    ka = partial_rope(rmsnorm(heads(ka), k_norm[:, None, :]), pos)
    va = heads(va)
    outs, lses = [], []
    for g, (window, dil) in enumerate(ATTN_GROUPS):
        o_g, l_g = dilated_window_attention(qa[:, :, g], ka[:, :, g], va[:, :, g], dil, window // (2 * dil))
        outs.append(o_g)
        lses.append(l_g)
    wts = jax.nn.softmax(jnp.stack(lses), axis=0)
    o_attn = jnp.einsum('gbsh,gbshd->bshd', wts, jnp.stack(outs)).reshape(B, S, ATTN_OUT)

    qg = qg.reshape(B, S, GLA_HEADS, GLA_DK) * (GLA_DK ** -0.5)
    kg = kg.reshape(B, S, GLA_HEADS, GLA_DK)
    vg = vg.reshape(B, S, GLA_HEADS, GLA_DV)
    def log_gate(lr, d):
        z = lr.astype(jnp.float32) @ w_gla_gate[d].astype(jnp.float32) + b_gla_gate[d].astype(jnp.float32)
        return (jax.nn.log_sigmoid(z) / GLA_NORMALIZER).reshape(B, S, GLA_HEADS, GLA_DK)
    o_fwd = gla_direction(qg, kg, vg, log_gate(lrf, 0), False)
    flip = lambda t: jnp.flip(t, axis=1)
    o_bwd = flip(gla_direction(flip(qg), flip(kg), flip(vg), flip(log_gate(lrb, 1)), True))
    o_gla = rmsnorm(o_fwd + o_bwd, gla_norm) * jax.nn.silu(rg.reshape(B, S, GLA_HEADS, GLA_DV).astype(jnp.float32))
    o_gla = o_gla.reshape(B, S, GLA_VAL)

    u_a = o_attn.astype(x.dtype) @ w_branch_attn
    u_b = o_gla.astype(x.dtype) @ w_branch_gla
    merged = jax.nn.sigmoid(ga) * u_a + jax.nn.sigmoid(gb) * u_b
    h = x + merged @ w_out

    hn = rmsnorm(h, norm_ffn)
    return h + jnp.square(jax.nn.relu(hn @ w_ff1)) @ w_ff2


def trunk(x, norm_mix, w_in, q_norm, k_norm, w_gla_gate, b_gla_gate, gla_norm,
          w_branch_attn, w_branch_gla, w_out, norm_ffn, w_ff1, w_ff2):
    for layer in range(DEPTH):
        x = encoder_layer(x, norm_mix[layer], w_in[layer], q_norm[layer], k_norm[layer],
                          w_gla_gate[layer], b_gla_gate[layer], gla_norm[layer],
                          w_branch_attn[layer], w_branch_gla[layer], w_out[layer],
                          norm_ffn[layer], w_ff1[layer], w_ff2[layer])
    return x


def setup_inputs(seed: int = 0) -> dict:
    key = jax.random.key(seed)
    ks = jax.random.split(key, 16)
    f32 = jnp.float32
    def nrm(k, shape, scale):
        return jax.random.normal(k, shape, f32) * scale
    def gain(k, shape):
        return 1.0 + 0.02 * jax.random.normal(k, shape, f32)
    return {
        'x_prompt': nrm(ks[0], (BATCH, SEQ, D_MODEL), 1.0),
        'x_sample': nrm(ks[1], (DEC_BATCH, DEC_SEQ, D_MODEL), 1.0),
        'norm_mix': gain(ks[2], (DEPTH, D_MODEL)),
        'w_in': nrm(ks[3], (DEPTH, D_MODEL, IN_COLS), D_MODEL ** -0.5),
        'q_norm': gain(ks[4], (DEPTH, N_GROUPS, HEAD_DIM)),
        'k_norm': gain(ks[5], (DEPTH, N_GROUPS, HEAD_DIM)),
        'w_gla_gate': nrm(ks[6], (DEPTH, 2, GLA_RANK, GLA_KEY), GLA_RANK ** -0.5),
        'b_gla_gate': nrm(ks[7], (DEPTH, 2, GLA_KEY), 0.1),
        'gla_norm': gain(ks[8], (DEPTH, GLA_DV)),
        'w_branch_attn': nrm(ks[9], (DEPTH, ATTN_OUT, D_MODEL), ATTN_OUT ** -0.5),
        'w_branch_gla': nrm(ks[10], (DEPTH, GLA_VAL, D_MODEL), GLA_VAL ** -0.5),
        'w_out': nrm(ks[11], (DEPTH, D_MODEL, D_MODEL), D_MODEL ** -0.5),
        'norm_ffn': gain(ks[12], (DEPTH, D_MODEL)),
        'w_ff1': nrm(ks[13], (DEPTH, D_MODEL, D_FF), D_MODEL ** -0.5),
        'w_ff2': nrm(ks[14], (DEPTH, D_FF, D_MODEL), D_FF ** -0.5),
    }


def reference(x_prompt, x_sample, norm_mix, w_in, q_norm, k_norm, w_gla_gate, b_gla_gate, gla_norm,
              w_branch_attn, w_branch_gla, w_out, norm_ffn, w_ff1, w_ff2):
    y_prompt = trunk(x_prompt, norm_mix, w_in, q_norm, k_norm, w_gla_gate, b_gla_gate, gla_norm,
                     w_branch_attn, w_branch_gla, w_out, norm_ffn, w_ff1, w_ff2)
    y_sample = trunk(x_sample, norm_mix, w_in, q_norm, k_norm, w_gla_gate, b_gla_gate, gla_norm,
                     w_branch_attn, w_branch_gla, w_out, norm_ffn, w_ff1, w_ff2)
    return (y_prompt, y_sample)
```

```python
import functools

import jax
import jax.numpy as jnp
import numpy as np
from jax import lax
from jax.experimental import pallas as pl
from jax.experimental.pallas import tpu as pltpu

F32 = jnp.float32
BF16 = jnp.bfloat16

D_MODEL = 2048
HEAD_DIM = 128
ATTN_GROUPS = ((128, 1), (512, 4), (2048, 16))
N_GROUPS = 3
HEADS_PER_GROUP = 8
ATTN_QKV = N_GROUPS * HEADS_PER_GROUP * HEAD_DIM
ATTN_OUT = HEADS_PER_GROUP * HEAD_DIM
ROT_DIM = HEAD_DIM // 4
ROPE_THETA = 500000.0
GLA_HEADS = 4
GLA_KEY = 1024
GLA_VAL = 2048
GLA_DK = 256
GLA_DV = 512
GLA_RANK = 16
GLA_NORMALIZER = 16.0
D_FF = 4 * D_MODEL
EPS = 1e-6

COL_VG = 0
COL_RG = 2048
COL_GA = 4096
COL_GB = 6144
COL_QG = 8192
COL_KG = 9216
COL_QA = 10240
COL_KA = 13312
COL_VA = 16384
P_COLS = 19456
LR_COLS = 128

V7X_VMEM_LIMIT = 56 * 1024 * 1024
NEG = -1e30

ATTN_HALF = 64
ATTN_TQ = 128
GLA_CHUNK = 64
GLA_BLOCK = 256


def _dot(a, b):
    return jnp.dot(a, b, preferred_element_type=F32)


def _dot_nt(a, b):
    return lax.dot_general(a, b, (((1,), (1,)), ((), ())), preferred_element_type=F32)


def _dot_tn(a, b):
    return lax.dot_general(a, b, (((0,), (0,)), ((), ())), preferred_element_type=F32)


def _split_bf16(x):
    hi = x.astype(BF16)
    lo = (x - hi.astype(F32)).astype(BF16)
    return hi, lo


def _inproj_kernel(x_ref, gain_ref, w_ref, wlr_ref, p_ref, lr_ref, xn_ref):
    j = pl.program_id(1)

    @pl.when(j == 0)
    def _():
        x = x_ref[...]
        ms = jnp.mean(x * x, axis=-1, keepdims=True)
        xn = (x * lax.rsqrt(ms + EPS) * gain_ref[...]).astype(BF16)
        xn_ref[...] = xn
        lr_ref[...] = _dot(xn, wlr_ref[...])

    p_ref[...] = _dot(xn_ref[...], w_ref[...]).astype(BF16)


def _inproj(x, gain, w_main, w_lr, *, tm=1024, tn=1024):
    t = x.shape[0]
    return pl.pallas_call(
        _inproj_kernel,
        grid=(t // tm, P_COLS // tn),
        in_specs=[
            pl.BlockSpec((tm, D_MODEL), lambda i, j: (i, 0)),
            pl.BlockSpec((1, D_MODEL), lambda i, j: (0, 0)),
            pl.BlockSpec((D_MODEL, tn), lambda i, j: (0, j)),
            pl.BlockSpec((D_MODEL, LR_COLS), lambda i, j: (0, 0)),
        ],
        out_specs=[
            pl.BlockSpec((tm, tn), lambda i, j: (i, j)),
            pl.BlockSpec((tm, LR_COLS), lambda i, j: (i, 0)),
        ],
        out_shape=[
            jax.ShapeDtypeStruct((t, P_COLS), BF16),
            jax.ShapeDtypeStruct((t, LR_COLS), F32),
        ],
        scratch_shapes=[pltpu.VMEM((tm, D_MODEL), BF16)],
        compiler_params=pltpu.CompilerParams(
            dimension_semantics=("parallel", "arbitrary"),
            vmem_limit_bytes=V7X_VMEM_LIMIT),
        name="inproj",
    )(x, gain, w_main, w_lr)


def _attn_kernel(q_ref, k_ref, v_ref, cos_ref, sa_ref, sb_ref, qg_ref, kg_ref,
                 o_ref, lse_ref, qs, ks, vs, bias_s, *, L, hb):
    TQ, HALF = ATTN_TQ, ATTN_HALF
    TK = TQ + 2 * HALF
    hblk = pl.program_id(2)
    TR = min(L, 256)

    @pl.when(hblk == 0)
    def _():
        lse_ref[...] = jnp.zeros_like(lse_ref)

    zpad = jnp.zeros((HALF, HEAD_DIM), BF16)
    ks[0:HALF, :] = zpad
    ks[L + HALF:L + 2 * HALF, :] = zpad
    vs[0:HALF, :] = zpad
    vs[L + HALF:L + 2 * HALF, :] = zpad

    ri = lax.broadcasted_iota(jnp.int32, (TQ, TK), 0)
    ci = lax.broadcasted_iota(jnp.int32, (TQ, TK), 1)
    d = ci - ri
    bias_s[...] = jnp.where(d < 0, NEG, jnp.where(d > 2 * HALF, NEG, 0.0)).astype(F32)

    qgain = qg_ref[...] * (HEAD_DIM ** -0.5)
    kgain = kg_ref[...]
    col = lax.broadcasted_iota(jnp.int32, (1, TK), 1)
    lane = lax.broadcasted_iota(jnp.int32, (TQ, HEAD_DIM), 1)

    def norm_rope(x, gain, cos, sa, sb):
        ms = jnp.mean(x * x, axis=-1, keepdims=True)
        xn = x * lax.rsqrt(ms + EPS) * gain
        return xn * cos + pltpu.roll(xn, HEAD_DIM - ROT_DIM // 2, 1) * sa + pltpu.roll(xn, ROT_DIM // 2, 1) * sb

    for hh in range(hb):
        hs = slice(hh * HEAD_DIM, (hh + 1) * HEAD_DIM)

        def prep(t, carry):
            r0 = pl.multiple_of(t * TR, TR)
            rows = pl.ds(r0, TR)
            prow = pl.ds(pl.multiple_of(r0 + HALF, HALF), TR)
            cos, sa, sb = cos_ref[rows, :], sa_ref[rows, :], sb_ref[rows, :]
            qs[rows, :] = norm_rope(q_ref[0, rows, hs].astype(F32), qgain, cos, sa, sb).astype(BF16)
            ks[prow, :] = norm_rope(k_ref[0, rows, hs].astype(F32), kgain, cos, sa, sb).astype(BF16)
            vs[prow, :] = v_ref[0, rows, hs]
            return carry

        lax.fori_loop(0, L // TR, prep, 0)

        def tile(t, carry):
            q0 = pl.multiple_of(t * TQ, TQ)
            q = qs[pl.ds(q0, TQ), :]
            kt = ks[pl.ds(q0, TK), :]
            vt = vs[pl.ds(q0, TK), :]
            s = _dot_nt(q, kt)
            kpos = col + (q0 - HALF)
            edge = jnp.where(kpos < 0, NEG, jnp.where(kpos >= L, NEG, 0.0)).astype(F32)
            s = s + bias_s[...] + edge
            m = jnp.max(s, axis=-1, keepdims=True)
            p = jnp.exp(s - m)
            den = jnp.sum(p, axis=-1, keepdims=True)
            o = _dot(p.astype(BF16), vt) * (1.0 / den)
            o_ref[0, pl.ds(q0, TQ), hs] = o.astype(BF16)
            lse = m + jnp.log(den)
            cur = lse_ref[0, pl.ds(q0, TQ), :]
            lse_ref[0, pl.ds(q0, TQ), :] = jnp.where(lane == hblk * hb + hh, lse, cur)
            return carry

        lax.fori_loop(0, L // TQ, tile, 0)


def _attention_group(p3, tables, q_gain, k_gain, g, B, S):
    window, r = ATTN_GROUPS[g]
    assert window // (2 * r) == ATTN_HALF
    L = S // r
    assert L % ATTN_TQ == 0
    hb = max(1, min(HEADS_PER_GROUP, 8192 // L))
    nhb = HEADS_PER_GROUP // hb
    bw = hb * HEAD_DIM
    pv = p3.reshape(B, L, r * P_COLS)
    cos, sa, sb = (t[:S].reshape(L, r * HEAD_DIM) for t in tables)

    def col_map(base):
        return lambda b, c, h: (b, 0, (c * P_COLS + base + g * ATTN_OUT) // bw + h)

    tab_spec = pl.BlockSpec((L, HEAD_DIM), lambda b, c, h: (0, c))
    gain_spec = pl.BlockSpec((1, HEAD_DIM), lambda b, c, h: (0, 0))
    o, lse = pl.pallas_call(
        functools.partial(_attn_kernel, L=L, hb=hb),
        grid=(B, r, nhb),
        in_specs=[
            pl.BlockSpec((1, L, bw), col_map(COL_QA)),
            pl.BlockSpec((1, L, bw), col_map(COL_KA)),
            pl.BlockSpec((1, L, bw), col_map(COL_VA)),
            tab_spec, tab_spec, tab_spec, gain_spec, gain_spec,
        ],
        out_specs=[
            pl.BlockSpec((1, L, bw), lambda b, c, h: (b, 0, c * nhb + h)),
            pl.BlockSpec((1, L, HEAD_DIM), lambda b, c, h: (b, 0, c)),
        ],
        out_shape=[
            jax.ShapeDtypeStruct((B, L, r * ATTN_OUT), BF16),
            jax.ShapeDtypeStruct((B, L, r * HEAD_DIM), F32),
        ],
        scratch_shapes=[
            pltpu.VMEM((L, HEAD_DIM), BF16),
            pltpu.VMEM((L + 2 * ATTN_HALF, HEAD_DIM), BF16),
            pltpu.VMEM((L + 2 * ATTN_HALF, HEAD_DIM), BF16),
            pltpu.VMEM((ATTN_TQ, ATTN_TQ + 2 * ATTN_HALF), F32),
        ],
        compiler_params=pltpu.CompilerParams(
            dimension_semantics=("parallel", "parallel", "arbitrary"),
            vmem_limit_bytes=V7X_VMEM_LIMIT),
        name=f"attn_g{g}",
    )(pv, pv, pv, cos, sa, sb, q_gain[g:g + 1], k_gain[g:g + 1])
    return o.reshape(B, S, ATTN_OUT), lse.reshape(B, S, HEAD_DIM)


def _gla_kernel(qf_ref, kf_ref, vf_ref, lrf_ref, qb_ref, kb_ref, vb_ref, lrb_ref, wg_ref, bg_ref,
                of_ref, ob_ref, stf, stb, gf_s, gb_s):
    C = GLA_CHUNK
    nchunk = GLA_BLOCK // C
    n = pl.program_id(2)

    @pl.when(n == 0)
    def _():
        stf[...] = jnp.zeros_like(stf)
        stb[...] = jnp.zeros_like(stb)

    def gates(lr_ref, d):
        lr_hi, lr_lo = _split_bf16(lr_ref[0])
        w_hi, w_lo = _split_bf16(wg_ref[d])
        z = _dot(lr_hi, w_hi) + (_dot(lr_lo, w_hi) + _dot(lr_hi, w_lo)) + bg_ref[d]
        softplus = jnp.maximum(-z, 0.0) + jnp.log(1.0 + jnp.exp(-jnp.abs(z)))
        return softplus * (-1.0 / GLA_NORMALIZER)

    gf_s[...] = gates(lrf_ref, 0)
    gb_s[...] = gates(lrb_ref, 1)

    ri = lax.broadcasted_iota(jnp.int32, (C, C), 0)
    ci = lax.broadcasted_iota(jnp.int32, (C, C), 1)
    lower_incl = ri >= ci
    upper_incl = ci >= ri
    upper_strict = ci > ri
    tri_f = jnp.where(lower_incl, 1.0, 0.0).astype(BF16)
    tri_b = jnp.where(upper_incl, 1.0, 0.0).astype(BF16)

    def chunk(q_ref, k_ref, v_ref, g_s, o_ref, st, c0, tri, mask, mid, last):
        rows = slice(c0, c0 + C)
        g_hi, g_lo = _split_bf16(g_s[rows, :])
        cum = _dot(tri, g_hi) + _dot(tri, g_lo)
        ref = cum[mid:mid + 1, :]
        end = cum[last:last + 1, :]
        q = q_ref[0, rows, :].astype(F32) * (GLA_DK ** -0.5)
        k = k_ref[0, rows, :].astype(F32)
        v = v_ref[0, rows, :]
        qe = (q * jnp.exp(cum - ref)).astype(BF16)
        ke = (k * jnp.exp(ref - cum)).astype(BF16)
        att = jnp.where(mask, _dot_nt(qe, ke), 0.0).astype(BF16)
        qi = (q * jnp.exp(cum)).astype(BF16)
        state = st[...]
        o = _dot(att, v) + _dot_nt(qi, state.astype(BF16))
        o_ref[0, rows, :] = o.astype(BF16)
        k2 = (k * jnp.exp(end - cum)).astype(BF16)
        st[...] = state * jnp.exp(end) + _dot_tn(v, k2)

    for i in range(nchunk):
        chunk(qf_ref, kf_ref, vf_ref, gf_s, of_ref, stf, i * C, tri_f, lower_incl, C // 2 - 1, C - 1)
        chunk(qb_ref, kb_ref, vb_ref, gb_s, ob_ref, stb, (nchunk - 1 - i) * C, tri_b, upper_strict, C // 2, 0)


def _gla(p3, lr3, wg_pad, bg, B, S):
    TS = GLA_BLOCK
    NB = S // TS
    kq, kk, kv = COL_QG // GLA_DK, COL_KG // GLA_DK, COL_VG // GLA_DV

    def fwd(base):
        return lambda b, h, n: (b, n, base + h)

    def bwd(base):
        return lambda b, h, n: (b, NB - 1 - n, base + h)

    def specs(m):
        return [
            pl.BlockSpec((1, TS, GLA_DK), m(kq)),
            pl.BlockSpec((1, TS, GLA_DK), m(kk)),
            pl.BlockSpec((1, TS, GLA_DV), m(kv)),
            pl.BlockSpec((1, TS, LR_COLS), (lambda b, h, n: (b, n, 0)) if m is fwd else (lambda b, h, n: (b, NB - 1 - n, 0))),
        ]

    return pl.pallas_call(
        _gla_kernel,
        grid=(B, GLA_HEADS, NB),
        in_specs=specs(fwd) + specs(bwd) + [
            pl.BlockSpec((2, LR_COLS, GLA_DK), lambda b, h, n: (0, 0, h)),
            pl.BlockSpec((2, 1, GLA_DK), lambda b, h, n: (0, 0, h)),
        ],
        out_specs=[
            pl.BlockSpec((1, TS, GLA_DV), lambda b, h, n: (b, n, h)),
            pl.BlockSpec((1, TS, GLA_DV), lambda b, h, n: (b, NB - 1 - n, h)),
        ],
        out_shape=[jax.ShapeDtypeStruct((B, S, GLA_VAL), BF16)] * 2,
        scratch_shapes=[
            pltpu.VMEM((GLA_DV, GLA_DK), F32),
            pltpu.VMEM((GLA_DV, GLA_DK), F32),
            pltpu.VMEM((TS, GLA_DK), F32),
            pltpu.VMEM((TS, GLA_DK), F32),
        ],
        compiler_params=pltpu.CompilerParams(
            dimension_semantics=("parallel", "parallel", "arbitrary"),
            vmem_limit_bytes=V7X_VMEM_LIMIT),
        name="gla",
    )(p3, p3, p3, lr3, p3, p3, p3, lr3, wg_pad, bg)


def _sigmoid(x):
    return 1.0 / (1.0 + jnp.exp(-x))


def _merge_kernel(o0_ref, o1_ref, o2_ref, l0_ref, l1_ref, l2_ref, of_ref, ob_ref, rg_ref, ga_ref, gb_ref,
                  x_ref, wba_ref, wbg_ref, wout_ref, gnorm_ref, nffn_ref, h_ref, hn_ref, oa_s, og_s):
    l0, l1, l2 = l0_ref[...], l1_ref[...], l2_ref[...]
    m = jnp.maximum(jnp.maximum(l0, l1), l2)
    e0, e1, e2 = jnp.exp(l0 - m), jnp.exp(l1 - m), jnp.exp(l2 - m)
    inv = 1.0 / (e0 + e1 + e2)
    w0, w1, w2 = e0 * inv, e1 * inv, e2 * inv
    for hh in range(HEADS_PER_GROUP):
        hs = slice(hh * HEAD_DIM, (hh + 1) * HEAD_DIM)
        comb = (w0[:, hh:hh + 1] * o0_ref[:, hs].astype(F32)
                + w1[:, hh:hh + 1] * o1_ref[:, hs].astype(F32)
                + w2[:, hh:hh + 1] * o2_ref[:, hs].astype(F32))
        oa_s[:, hs] = comb.astype(BF16)
    u_a = _dot(oa_s[...], wba_ref[...])

    for h in range(GLA_HEADS):
        vs = slice(h * GLA_DV, (h + 1) * GLA_DV)
        og = of_ref[:, vs].astype(F32) + ob_ref[:, vs].astype(F32)
        ms = jnp.mean(og * og, axis=-1, keepdims=True)
        ogn = og * lax.rsqrt(ms + EPS) * gnorm_ref[...]
        rg = rg_ref[:, vs].astype(F32)
        og_s[:, vs] = (ogn * (rg * _sigmoid(rg))).astype(BF16)
    u_b = _dot(og_s[...], wbg_ref[...])

    merged = _sigmoid(ga_ref[...].astype(F32)) * u_a + _sigmoid(gb_ref[...].astype(F32)) * u_b
    h = x_ref[...] + _dot(merged.astype(BF16), wout_ref[...])
    h_ref[...] = h
    ms = jnp.mean(h * h, axis=-1, keepdims=True)
    hn_ref[...] = (h * lax.rsqrt(ms + EPS) * nffn_ref[...]).astype(BF16)


def _merge(o_groups, lse_groups, o_fwd, o_bwd, p2, x, wba, wbg, wout, gnorm, nffn, *, tm=256):
    t = x.shape[0]
    row = lambda i: (i, 0)
    const = lambda i: (0, 0)

    def resident(shape):
        return pl.BlockSpec(shape, const, pipeline_mode=pl.Buffered(1))

    return pl.pallas_call(
        _merge_kernel,
        grid=(t // tm,),
        in_specs=[pl.BlockSpec((tm, ATTN_OUT), row)] * 3 + [pl.BlockSpec((tm, HEAD_DIM), row)] * 3 + [
            pl.BlockSpec((tm, GLA_VAL), row),
            pl.BlockSpec((tm, GLA_VAL), row),
            pl.BlockSpec((tm, GLA_VAL), lambda i: (i, COL_RG // GLA_VAL)),
            pl.BlockSpec((tm, D_MODEL), lambda i: (i, COL_GA // D_MODEL)),
            pl.BlockSpec((tm, D_MODEL), lambda i: (i, COL_GB // D_MODEL)),
            pl.BlockSpec((tm, D_MODEL), row),
            resident((ATTN_OUT, D_MODEL)),
            resident((GLA_VAL, D_MODEL)),
            resident((D_MODEL, D_MODEL)),
            resident((1, GLA_DV)),
            resident((1, D_MODEL)),
        ],
        out_specs=[pl.BlockSpec((tm, D_MODEL), row), pl.BlockSpec((tm, D_MODEL), row)],
        out_shape=[jax.ShapeDtypeStruct((t, D_MODEL), F32), jax.ShapeDtypeStruct((t, D_MODEL), BF16)],
        scratch_shapes=[pltpu.VMEM((tm, ATTN_OUT), BF16), pltpu.VMEM((tm, GLA_VAL), BF16)],
        compiler_params=pltpu.CompilerParams(
            dimension_semantics=("parallel",),
            vmem_limit_bytes=V7X_VMEM_LIMIT),
        name="merge",
    )(*o_groups, *lse_groups, o_fwd, o_bwd, p2, p2, p2, x, wba, wbg, wout, gnorm, nffn)


def _ffn_kernel(hn_ref, w1_ref, w2_ref, h_ref, y_ref):
    j = pl.program_id(1)
    a = _dot(hn_ref[...], w1_ref[...])
    a = jnp.square(jnp.maximum(a, 0.0)).astype(BF16)
    y = _dot(a, w2_ref[...])

    @pl.when(j == 0)
    def _():
        y_ref[...] = h_ref[...] + y

    @pl.when(j > 0)
    def _():
        y_ref[...] += y


def _ffn(hn, h, w1, w2, *, tm=512, tf=1024):
    t = hn.shape[0]
    return pl.pallas_call(
        _ffn_kernel,
        grid=(t // tm, D_FF // tf),
        in_specs=[
            pl.BlockSpec((tm, D_MODEL), lambda i, j: (i, 0)),
            pl.BlockSpec((D_MODEL, tf), lambda i, j: (0, j)),
            pl.BlockSpec((tf, D_MODEL), lambda i, j: (j, 0)),
            pl.BlockSpec((tm, D_MODEL), lambda i, j: (i, 0)),
        ],
        out_specs=pl.BlockSpec((tm, D_MODEL), lambda i, j: (i, 0)),
        out_shape=jax.ShapeDtypeStruct((t, D_MODEL), F32),
        compiler_params=pltpu.CompilerParams(
            dimension_semantics=("parallel", "arbitrary"),
            vmem_limit_bytes=V7X_VMEM_LIMIT),
        name="ffn",
    )(hn, w1, w2, h)


def _rope_tables(s_max):
    half = ROT_DIM // 2
    inv_freq = ROPE_THETA ** (-jnp.arange(0, ROT_DIM, 2, dtype=F32) / ROT_DIM)
    ang = jnp.arange(s_max, dtype=F32)[:, None] * inv_freq[None, :]
    cos, sin = jnp.cos(ang), jnp.sin(ang)
    ones = jnp.ones((s_max, HEAD_DIM - ROT_DIM), F32)
    zeros = jnp.zeros((s_max, HEAD_DIM - half), F32)
    cos_t = jnp.concatenate([cos, cos, ones], axis=1)
    sa_t = jnp.concatenate([-sin, zeros], axis=1)
    sb_t = jnp.concatenate([jnp.zeros((s_max, half), F32), sin, zeros[:, :HEAD_DIM - ROT_DIM]], axis=1)
    return cos_t, sa_t, sb_t


def _prepare_layer(w_in, w_gla_gate, b_gla_gate):
    cuts = np.cumsum((ATTN_QKV, ATTN_QKV, ATTN_QKV, GLA_KEY, GLA_KEY, GLA_VAL, GLA_VAL, GLA_RANK, GLA_RANK,
                      D_MODEL, D_MODEL))[:-1]
    qa, ka, va, qg, kg, vg, rg, lrf, lrb, ga, gb = jnp.split(w_in, [int(c) for c in cuts], axis=1)
    w_main = jnp.concatenate([vg, rg, ga, gb, qg, kg, qa, ka, va], axis=1).astype(BF16)
    w_lr = jnp.pad(jnp.concatenate([lrf, lrb], axis=1), ((0, 0), (0, LR_COLS - 2 * GLA_RANK))).astype(BF16)
    wg_pad = jnp.zeros((2, LR_COLS, GLA_KEY), F32)
    wg_pad = wg_pad.at[0, 0:GLA_RANK].set(w_gla_gate[0].astype(F32))
    wg_pad = wg_pad.at[1, GLA_RANK:2 * GLA_RANK].set(w_gla_gate[1].astype(F32))
    bg = b_gla_gate.astype(F32).reshape(2, 1, GLA_KEY)
    return w_main, w_lr, wg_pad, bg


def _layer(x3, tables, norm_mix, prepared, q_norm, k_norm, gla_norm, wba, wbg, wout, norm_ffn, w1, w2):
    B, S, _ = x3.shape
    T = B * S
    w_main, w_lr, wg_pad, bg = prepared
    x = x3.reshape(T, D_MODEL)
    p, lr = _inproj(x, norm_mix.reshape(1, D_MODEL), w_main, w_lr)
    p3 = p.reshape(B, S, P_COLS)
    o_groups, lse_groups = [], []
    for g in range(N_GROUPS):
        o_g, lse_g = _attention_group(p3, tables, q_norm, k_norm, g, B, S)
        o_groups.append(o_g.reshape(T, ATTN_OUT))
        lse_groups.append(lse_g.reshape(T, HEAD_DIM))
    o_fwd, o_bwd = _gla(p3, lr.reshape(B, S, LR_COLS), wg_pad, bg, B, S)
    h, hn = _merge(o_groups, lse_groups, o_fwd.reshape(T, GLA_VAL), o_bwd.reshape(T, GLA_VAL), p, x,
                   wba, wbg, wout, gla_norm.reshape(1, GLA_DV), norm_ffn.reshape(1, D_MODEL))
    y = _ffn(hn, h, w1, w2)
    return y.reshape(B, S, D_MODEL)


def kernel(x_prompt, x_sample, norm_mix, w_in, q_norm, k_norm, w_gla_gate, b_gla_gate, gla_norm,
           w_branch_attn, w_branch_gla, w_out, norm_ffn, w_ff1, w_ff2):
    depth = w_in.shape[0]
    tables = _rope_tables(max(x_prompt.shape[1], x_sample.shape[1]))
    layers = []
    for l in range(depth):
        layers.append((
            norm_mix[l], _prepare_layer(w_in[l], w_gla_gate[l], b_gla_gate[l]),
            q_norm[l].astype(F32), k_norm[l].astype(F32), gla_norm[l].astype(F32),
            w_branch_attn[l].astype(BF16), w_branch_gla[l].astype(BF16), w_out[l].astype(BF16),
            norm_ffn[l].astype(F32), w_ff1[l].astype(BF16), w_ff2[l].astype(BF16)))
    outs = []
    for x in (x_prompt, x_sample):
        for layer in layers:
            x = _layer(x, tables, *layer)
        outs.append(x)
    return tuple(outs)
```

```python
import functools

import jax
import jax.numpy as jnp
import numpy as np
from jax import lax
from jax.experimental import pallas as pl
from jax.experimental.pallas import tpu as pltpu

F32 = jnp.float32
BF16 = jnp.bfloat16

D_MODEL = 2048
HEAD_DIM = 128
ATTN_GROUPS = ((128, 1), (512, 4), (2048, 16))
N_GROUPS = 3
HEADS_PER_GROUP = 8
ATTN_QKV = N_GROUPS * HEADS_PER_GROUP * HEAD_DIM
ATTN_OUT = HEADS_PER_GROUP * HEAD_DIM
ROT_DIM = HEAD_DIM // 4
ROPE_THETA = 500000.0
GLA_HEADS = 4
GLA_KEY = 1024
GLA_VAL = 2048
GLA_DK = 256
GLA_DV = 512
GLA_RANK = 16
GLA_NORMALIZER = 16.0
D_FF = 4 * D_MODEL
EPS = 1e-6

COL_VG = 0
COL_RG = 2048
COL_GA = 4096
COL_GB = 6144
COL_QG = 8192
COL_KG = 9216
P_COLS = 10240
LR_COLS = 128

V7X_VMEM_LIMIT = 56 * 1024 * 1024
NEG = -1e30

ATTN_HALF = 64
ATTN_TQ = 128
ATTN_CHAINS = 8
GLA_CHUNK = 64
GLA_BLOCK = 256


def _dot(a, b):
    return jnp.dot(a, b, preferred_element_type=F32)


def _dot_nt(a, b):
    return lax.dot_general(a, b, (((1,), (1,)), ((), ())), preferred_element_type=F32)


def _dot_tn(a, b):
    return lax.dot_general(a, b, (((0,), (0,)), ((), ())), preferred_element_type=F32)


def _split_bf16(x):
    hi = x.astype(BF16)
    lo = (x - hi.astype(F32)).astype(BF16)
    return hi, lo


def _inproj_kernel(x_ref, gain_ref, w_ref, wlr_ref, p_ref, lr_ref, xn_ref):
    j = pl.program_id(1)

    @pl.when(j == 0)
    def _():
        x = x_ref[...]
        ms = jnp.mean(x * x, axis=-1, keepdims=True)
        xn = (x * lax.rsqrt(ms + EPS) * gain_ref[...]).astype(BF16)
        xn_ref[...] = xn
        lr_ref[...] = _dot(xn, wlr_ref[...])

    p_ref[...] = _dot(xn_ref[...], w_ref[...]).astype(BF16)


def _inproj(x, gain, w_main, w_lr, *, tm=1024, tn=1024):
    t = x.shape[0]
    return pl.pallas_call(
        _inproj_kernel,
        grid=(t // tm, P_COLS // tn),
        in_specs=[
            pl.BlockSpec((tm, D_MODEL), lambda i, j: (i, 0)),
            pl.BlockSpec((1, D_MODEL), lambda i, j: (0, 0)),
            pl.BlockSpec((D_MODEL, tn), lambda i, j: (0, j)),
            pl.BlockSpec((D_MODEL, LR_COLS), lambda i, j: (0, 0)),
        ],
        out_specs=[
            pl.BlockSpec((tm, tn), lambda i, j: (i, j)),
            pl.BlockSpec((tm, LR_COLS), lambda i, j: (i, 0)),
            pl.BlockSpec((tm, D_MODEL), lambda i, j: (i, 0)),
        ],
        out_shape=[
            jax.ShapeDtypeStruct((t, P_COLS), BF16),
            jax.ShapeDtypeStruct((t, LR_COLS), F32),
            jax.ShapeDtypeStruct((t, D_MODEL), BF16),
        ],
        compiler_params=pltpu.CompilerParams(
            dimension_semantics=("parallel", "arbitrary"),
            vmem_limit_bytes=V7X_VMEM_LIMIT),
        name="inproj",
    )(x, gain, w_main, w_lr)


def _inproj_attn_kernel(xn_ref, w_ref, o_ref, *scratch, r):
    acc = _dot(xn_ref[...], w_ref[...])
    if r == 1:
        o_ref[0, 0] = acc.astype(BF16)
    else:
        acc_s, = scratch
        n = acc_s.shape[1] // r
        for kb in range(acc_s.shape[0]):
            ls = slice(kb * HEAD_DIM, (kb + 1) * HEAD_DIM)
            acc_s[kb] = acc[:, ls]
            for c in range(r):
                o_ref[0, c, :, ls] = acc_s[kb, pl.ds(c, n, stride=r), :].astype(BF16)


def _inproj_attn(xn, w_g, r, B, S, *, tm=1024, tn=1024):
    t = xn.shape[0]
    per_b = S // tm
    ncol = w_g.shape[1]
    return pl.pallas_call(
        functools.partial(_inproj_attn_kernel, r=r),
        grid=(t // tm, ncol // tn),
        in_specs=[
            pl.BlockSpec((tm, D_MODEL), lambda i, j: (i, 0)),
            pl.BlockSpec((D_MODEL, tn), lambda i, j: (0, j)),
        ],
        out_specs=pl.BlockSpec((1, r, tm // r, tn), lambda i, j: (i // per_b, 0, i % per_b, j)),
        out_shape=jax.ShapeDtypeStruct((B, r, S // r, ncol), BF16),
        scratch_shapes=[] if r == 1 else [pltpu.VMEM((tn // HEAD_DIM, tm, HEAD_DIM), F32)],
        compiler_params=pltpu.CompilerParams(
            dimension_semantics=("parallel", "arbitrary"),
            vmem_limit_bytes=V7X_VMEM_LIMIT),
        name=f"inproj_attn_r{r}",
    )(xn, w_g)


def _attn_kernel(q_ref, k_ref, v_ref, cos_ref, sin_ref, qg_ref, kg_ref,
                 o_ref, lse_ref, qs, ks, vs, bias_s, *, L, hb, U):
    TQ, HALF = ATTN_TQ, ATTN_HALF
    TK = TQ + 2 * HALF
    NT = L // TQ
    TR = min(L, 256)
    hblk = pl.program_id(2)

    @pl.when(hblk == 0)
    def _():
        lse_ref[...] = jnp.zeros_like(lse_ref)

    ri = lax.broadcasted_iota(jnp.int32, (TQ, TK), 0)
    ci = lax.broadcasted_iota(jnp.int32, (TQ, TK), 1)
    d = ci - ri
    band = jnp.where(d < 0, NEG, jnp.where(d > 2 * HALF, NEG, 0.0)).astype(F32)
    first = jnp.where(ci < HALF, NEG, band)
    bias_s[0] = band
    bias_s[1] = first
    bias_s[2] = jnp.where(ci >= TQ + HALF, NEG, band)
    bias_s[3] = jnp.where(ci >= TQ + HALF, NEG, first)

    a = lax.broadcasted_iota(jnp.int32, (HEAD_DIM, HEAD_DIM), 0)
    b = lax.broadcasted_iota(jnp.int32, (HEAD_DIM, HEAD_DIM), 1)
    half = ROT_DIM // 2
    ones_m = jnp.ones((HEAD_DIM, HEAD_DIM), BF16)
    rot_m = jnp.where((b < half) & (a == b + half), -1.0,
                      jnp.where((b >= half) & (b < ROT_DIM) & (a == b - half), 1.0, 0.0)).astype(BF16)

    zpad = jnp.zeros((HALF, HEAD_DIM), BF16)
    for hh in range(hb):
        ks[hh, 0:HALF, :] = zpad
        ks[hh, L + HALF:L + 2 * HALF, :] = zpad
        vs[hh, 0:HALF, 0:HEAD_DIM] = zpad
        vs[hh, L + HALF:L + 2 * HALF, 0:HEAD_DIM] = zpad
        vs[hh, :, HEAD_DIM:2 * HEAD_DIM] = jnp.ones((L + 2 * HALF, HEAD_DIM), BF16)

    qgain = qg_ref[...] * (HEAD_DIM ** -0.5)
    kgain = kg_ref[...]

    def norm_rope(x, gain, cos, sin):
        ssq = _dot((x * x).astype(BF16), ones_m)
        xn = x * lax.rsqrt(ssq * (1.0 / HEAD_DIM) + EPS) * gain
        return xn * cos + _dot(xn.astype(BF16), rot_m) * sin

    def prep(t, carry):
        r0 = pl.multiple_of(t * TR, TR)
        rows = pl.ds(r0, TR)
        prow = pl.ds(pl.multiple_of(r0 + HALF, HALF), TR)
        cos, sin = cos_ref[rows, :], sin_ref[rows, :]
        for hh in range(hb):
            hs = slice(hh * HEAD_DIM, (hh + 1) * HEAD_DIM)
            qs[hh, rows, :] = norm_rope(q_ref[0, 0, rows, hs].astype(F32), qgain, cos, sin).astype(BF16)
            ks[hh, prow, :] = norm_rope(k_ref[0, 0, rows, hs].astype(F32), kgain, cos, sin).astype(BF16)
            vs[hh, prow, 0:HEAD_DIM] = v_ref[0, 0, rows, hs]
        return carry

    lax.fori_loop(0, L // TR, prep, 0)

    lane = lax.broadcasted_iota(jnp.int32, (TQ, HEAD_DIM), 1)

    def tiles(tt, carry):
        for u in range(U):
            t = tt * U + u
            q0 = pl.multiple_of(t * TQ, TQ)
            qrows = pl.ds(q0, TQ)
            krows = pl.ds(q0, TK)
            bias = bias_s[jnp.where(t == 0, 1, 0) + jnp.where(t == NT - 1, 2, 0)]
            lse_tile = lse_ref[0, 0, qrows, :]
            for hh in range(hb):
                hs = slice(hh * HEAD_DIM, (hh + 1) * HEAD_DIM)
                s = _dot_nt(qs[hh, qrows, :], ks[hh, krows, :]) + bias
                m = jnp.max(s, axis=-1, keepdims=True)
                p = jnp.exp(s - m).astype(BF16)
                acc = _dot(p, vs[hh, krows, :])
                den = acc[:, HEAD_DIM:]
                o_ref[0, 0, qrows, hs] = (acc[:, :HEAD_DIM] * (1.0 / den)).astype(BF16)
                lse_tile = jnp.where(lane == hblk * hb + hh, m + jnp.log(den), lse_tile)
            lse_ref[0, 0, qrows, :] = lse_tile
        return carry

    lax.fori_loop(0, NT // U, tiles, 0)


def _attention_group(a_g, tables, q_gain, k_gain, g, B, S):
    window, r = ATTN_GROUPS[g]
    assert window // (2 * r) == ATTN_HALF
    L = S // r
    assert L % ATTN_TQ == 0
    hb = max(1, min(HEADS_PER_GROUP, 8192 // L))
    nhb = HEADS_PER_GROUP // hb
    bw = hb * HEAD_DIM
    U = max(1, min(ATTN_CHAINS // hb, L // ATTN_TQ))
    assert (L // ATTN_TQ) % U == 0
    cos, sin = (t[:S].reshape(L, r * HEAD_DIM) for t in tables)

    def col_map(part):
        return lambda b, c, h: (b, c, 0, part * nhb + h)

    tab_spec = pl.BlockSpec((L, HEAD_DIM), lambda b, c, h: (0, c))
    gain_spec = pl.BlockSpec((1, HEAD_DIM), lambda b, c, h: (0, 0))
    return pl.pallas_call(
        functools.partial(_attn_kernel, L=L, hb=hb, U=U),
        grid=(B, r, nhb),
        in_specs=[
            pl.BlockSpec((1, 1, L, bw), col_map(0)),
            pl.BlockSpec((1, 1, L, bw), col_map(1)),
            pl.BlockSpec((1, 1, L, bw), col_map(2)),
            tab_spec, tab_spec, gain_spec, gain_spec,
        ],
        out_specs=[
            pl.BlockSpec((1, 1, L, bw), lambda b, c, h: (b, c, 0, h)),
            pl.BlockSpec((1, 1, L, HEAD_DIM), lambda b, c, h: (b, c, 0, 0)),
        ],
        out_shape=[
            jax.ShapeDtypeStruct((B, r, L, ATTN_OUT), BF16),
            jax.ShapeDtypeStruct((B, r, L, HEAD_DIM), F32),
        ],
        scratch_shapes=[
            pltpu.VMEM((hb, L, HEAD_DIM), BF16),
            pltpu.VMEM((hb, L + 2 * ATTN_HALF, HEAD_DIM), BF16),
            pltpu.VMEM((hb, L + 2 * ATTN_HALF, 2 * HEAD_DIM), BF16),
            pltpu.VMEM((4, ATTN_TQ, ATTN_TQ + 2 * ATTN_HALF), F32),
        ],
        compiler_params=pltpu.CompilerParams(
            dimension_semantics=("parallel", "parallel", "arbitrary"),
            vmem_limit_bytes=V7X_VMEM_LIMIT),
        name=f"attn_g{g}",
    )(a_g, a_g, a_g, cos, sin, q_gain[g:g + 1], k_gain[g:g + 1])


def _gla_kernel(qf_ref, kf_ref, vf_ref, lrf_ref, qb_ref, kb_ref, vb_ref, lrb_ref, wg_ref, bg_ref,
                of_ref, ob_ref, stf, stb, gf_s, gb_s):
    C = GLA_CHUNK
    nchunk = GLA_BLOCK // C
    n = pl.program_id(2)

    @pl.when(n == 0)
    def _():
        stf[...] = jnp.zeros_like(stf)
        stb[...] = jnp.zeros_like(stb)

    def gates(lr_ref, d):
        lr_hi, lr_lo = _split_bf16(lr_ref[0])
        w_hi, w_lo = _split_bf16(wg_ref[d])
        z = _dot(lr_hi, w_hi) + (_dot(lr_lo, w_hi) + _dot(lr_hi, w_lo)) + bg_ref[d]
        softplus = jnp.maximum(-z, 0.0) + jnp.log(1.0 + jnp.exp(-jnp.abs(z)))
        return softplus * (-1.0 / GLA_NORMALIZER)

    gf_s[...] = gates(lrf_ref, 0)
    gb_s[...] = gates(lrb_ref, 1)

    ri = lax.broadcasted_iota(jnp.int32, (C, C), 0)
    ci = lax.broadcasted_iota(jnp.int32, (C, C), 1)
    lower_incl = ri >= ci
    upper_incl = ci >= ri
    upper_strict = ci > ri
    tri_f = jnp.where(lower_incl, 1.0, 0.0).astype(BF16)
    tri_b = jnp.where(upper_incl, 1.0, 0.0).astype(BF16)

    def chunk(q_ref, k_ref, v_ref, g_s, o_ref, st, c0, tri, mask, mid, last):
        rows = slice(c0, c0 + C)
        g_hi, g_lo = _split_bf16(g_s[rows, :])
        cum = _dot(tri, g_hi) + _dot(tri, g_lo)
        ref = cum[mid:mid + 1, :]
        end = cum[last:last + 1, :]
        q = q_ref[0, rows, :].astype(F32) * (GLA_DK ** -0.5)
        k = k_ref[0, rows, :].astype(F32)
        v = v_ref[0, rows, :]
        qe = (q * jnp.exp(cum - ref)).astype(BF16)
        ke = (k * jnp.exp(ref - cum)).astype(BF16)
        att = jnp.where(mask, _dot_nt(qe, ke), 0.0).astype(BF16)
        qi = (q * jnp.exp(cum)).astype(BF16)
        state = st[...]
        o = _dot(att, v) + _dot_nt(qi, state.astype(BF16))
        o_ref[0, rows, :] = o.astype(BF16)
        k2 = (k * jnp.exp(end - cum)).astype(BF16)
        st[...] = state * jnp.exp(end) + _dot_tn(v, k2)

    for i in range(nchunk):
        chunk(qf_ref, kf_ref, vf_ref, gf_s, of_ref, stf, i * C, tri_f, lower_incl, C // 2 - 1, C - 1)
        chunk(qb_ref, kb_ref, vb_ref, gb_s, ob_ref, stb, (nchunk - 1 - i) * C, tri_b, upper_strict, C // 2, 0)


def _gla(p3, lr3, wg_pad, bg, B, S):
    TS = GLA_BLOCK
    NB = S // TS
    kq, kk, kv = COL_QG // GLA_DK, COL_KG // GLA_DK, COL_VG // GLA_DV

    def fwd(base):
        return lambda b, h, n: (b, n, base + h)

    def bwd(base):
        return lambda b, h, n: (b, NB - 1 - n, base + h)

    def specs(m, lr_map):
        return [
            pl.BlockSpec((1, TS, GLA_DK), m(kq)),
            pl.BlockSpec((1, TS, GLA_DK), m(kk)),
            pl.BlockSpec((1, TS, GLA_DV), m(kv)),
            pl.BlockSpec((1, TS, LR_COLS), lr_map),
        ]

    return pl.pallas_call(
        _gla_kernel,
        grid=(B, GLA_HEADS, NB),
        in_specs=specs(fwd, lambda b, h, n: (b, n, 0)) + specs(bwd, lambda b, h, n: (b, NB - 1 - n, 0)) + [
            pl.BlockSpec((2, LR_COLS, GLA_DK), lambda b, h, n: (0, 0, h)),
            pl.BlockSpec((2, 1, GLA_DK), lambda b, h, n: (0, 0, h)),
        ],
        out_specs=[
            pl.BlockSpec((1, TS, GLA_DV), lambda b, h, n: (b, n, h)),
            pl.BlockSpec((1, TS, GLA_DV), lambda b, h, n: (b, NB - 1 - n, h)),
        ],
        out_shape=[jax.ShapeDtypeStruct((B, S, GLA_VAL), BF16)] * 2,
        scratch_shapes=[
            pltpu.VMEM((GLA_DV, GLA_DK), F32),
            pltpu.VMEM((GLA_DV, GLA_DK), F32),
            pltpu.VMEM((TS, GLA_DK), F32),
            pltpu.VMEM((TS, GLA_DK), F32),
        ],
        compiler_params=pltpu.CompilerParams(
            dimension_semantics=("parallel", "parallel", "arbitrary"),
            vmem_limit_bytes=V7X_VMEM_LIMIT),
        name="gla",
    )(p3, p3, p3, lr3, p3, p3, p3, lr3, wg_pad, bg)


def _sigmoid(x):
    return 1.0 / (1.0 + jnp.exp(-x))


def _merge_kernel(o0_ref, o1_ref, o2_ref, l0_ref, l1_ref, l2_ref, of_ref, ob_ref, rg_ref, ga_ref, gb_ref,
                  x_ref, wba_ref, wbg_ref, wout_ref, gnorm_ref, nffn_ref, h_ref, hn_ref,
                  oa_s, og_s, oil_s, lil_s):
    tm = x_ref.shape[0]
    for gi, (o_ref, l_ref) in enumerate(((o1_ref, l1_ref), (o2_ref, l2_ref))):
        r = o_ref.shape[1]
        for c in range(r):
            dst = pl.ds(c, tm // r, stride=r)
            lil_s[gi, dst, :] = l_ref[0, c]
            for hh in range(HEADS_PER_GROUP):
                oil_s[gi, hh, dst, :] = o_ref[0, c, :, hh * HEAD_DIM:(hh + 1) * HEAD_DIM].astype(F32)

    l0, l1, l2 = l0_ref[0, 0], lil_s[0], lil_s[1]
    m = jnp.maximum(jnp.maximum(l0, l1), l2)
    e0, e1, e2 = jnp.exp(l0 - m), jnp.exp(l1 - m), jnp.exp(l2 - m)
    inv = 1.0 / (e0 + e1 + e2)
    w0, w1, w2 = e0 * inv, e1 * inv, e2 * inv
    for hh in range(HEADS_PER_GROUP):
        hs = slice(hh * HEAD_DIM, (hh + 1) * HEAD_DIM)
        comb = (w0[:, hh:hh + 1] * o0_ref[0, 0, :, hs].astype(F32)
                + w1[:, hh:hh + 1] * oil_s[0, hh]
                + w2[:, hh:hh + 1] * oil_s[1, hh])
        oa_s[:, hs] = comb.astype(BF16)
    u_a = _dot(oa_s[...], wba_ref[...])

    for h in range(GLA_HEADS):
        vs = slice(h * GLA_DV, (h + 1) * GLA_DV)
        og = of_ref[:, vs].astype(F32) + ob_ref[:, vs].astype(F32)
        ms = jnp.mean(og * og, axis=-1, keepdims=True)
        ogn = og * lax.rsqrt(ms + EPS) * gnorm_ref[...]
        rg = rg_ref[:, vs].astype(F32)
        og_s[:, vs] = (ogn * (rg * _sigmoid(rg))).astype(BF16)
    u_b = _dot(og_s[...], wbg_ref[...])

    merged = _sigmoid(ga_ref[...].astype(F32)) * u_a + _sigmoid(gb_ref[...].astype(F32)) * u_b
    h = x_ref[...] + _dot(merged.astype(BF16), wout_ref[...])
    h_ref[...] = h
    ms = jnp.mean(h * h, axis=-1, keepdims=True)
    hn_ref[...] = (h * lax.rsqrt(ms + EPS) * nffn_ref[...]).astype(BF16)


def _merge(o_groups, lse_groups, o_fwd, o_bwd, p2, x, wba, wbg, wout, gnorm, nffn, B, S, *, tm=256):
    t = x.shape[0]
    per_b = S // tm
    row = lambda b, i: (b * per_b + i, 0)
    const = lambda b, i: (0, 0)

    def resident(shape):
        return pl.BlockSpec(shape, const, pipeline_mode=pl.Buffered(1))

    def split_spec(r, width):
        return pl.BlockSpec((1, r, tm // r, width), lambda b, i: (b, 0, i, 0))

    rs = [r for _, r in ATTN_GROUPS]
    return pl.pallas_call(
        _merge_kernel,
        grid=(B, per_b),
        in_specs=[split_spec(r, ATTN_OUT) for r in rs] + [split_spec(r, HEAD_DIM) for r in rs] + [
            pl.BlockSpec((tm, GLA_VAL), row),
            pl.BlockSpec((tm, GLA_VAL), row),
            pl.BlockSpec((tm, GLA_VAL), lambda b, i: (b * per_b + i, COL_RG // GLA_VAL)),
            pl.BlockSpec((tm, D_MODEL), lambda b, i: (b * per_b + i, COL_GA // D_MODEL)),
            pl.BlockSpec((tm, D_MODEL), lambda b, i: (b * per_b + i, COL_GB // D_MODEL)),
            pl.BlockSpec((tm, D_MODEL), row),
            resident((ATTN_OUT, D_MODEL)),
            resident((GLA_VAL, D_MODEL)),
            resident((D_MODEL, D_MODEL)),
            resident((1, GLA_DV)),
            resident((1, D_MODEL)),
        ],
        out_specs=[pl.BlockSpec((tm, D_MODEL), row), pl.BlockSpec((tm, D_MODEL), row)],
        out_shape=[jax.ShapeDtypeStruct((t, D_MODEL), F32), jax.ShapeDtypeStruct((t, D_MODEL), BF16)],
        scratch_shapes=[
            pltpu.VMEM((tm, ATTN_OUT), BF16),
            pltpu.VMEM((tm, GLA_VAL), BF16),
            pltpu.VMEM((2, HEADS_PER_GROUP, tm, HEAD_DIM), F32),
            pltpu.VMEM((2, tm, HEAD_DIM), F32),
        ],
        compiler_params=pltpu.CompilerParams(
            dimension_semantics=("parallel", "parallel"),
            vmem_limit_bytes=V7X_VMEM_LIMIT),
        name="merge",
    )(*o_groups, *lse_groups, o_fwd, o_bwd, p2, p2, p2, x, wba, wbg, wout, gnorm, nffn)


def _ffn_kernel(hn_ref, w1_ref, w2_ref, h_ref, y_ref):
    j = pl.program_id(1)
    a = _dot(hn_ref[...], w1_ref[...])
    a = jnp.square(jnp.maximum(a, 0.0)).astype(BF16)
    y = _dot(a, w2_ref[...])

    @pl.when(j == 0)
    def _():
        y_ref[...] = h_ref[...] + y

    @pl.when(j > 0)
    def _():
        y_ref[...] += y


def _ffn(hn, h, w1, w2, *, tm=512, tf=1024):
    t = hn.shape[0]
    return pl.pallas_call(
        _ffn_kernel,
        grid=(t // tm, D_FF // tf),
        in_specs=[
            pl.BlockSpec((tm, D_MODEL), lambda i, j: (i, 0)),
            pl.BlockSpec((D_MODEL, tf), lambda i, j: (0, j)),
            pl.BlockSpec((tf, D_MODEL), lambda i, j: (j, 0)),
            pl.BlockSpec((tm, D_MODEL), lambda i, j: (i, 0)),
        ],
        out_specs=pl.BlockSpec((tm, D_MODEL), lambda i, j: (i, 0)),
        out_shape=jax.ShapeDtypeStruct((t, D_MODEL), F32),
        compiler_params=pltpu.CompilerParams(
            dimension_semantics=("parallel", "arbitrary"),
            vmem_limit_bytes=V7X_VMEM_LIMIT),
        name="ffn",
    )(hn, w1, w2, h)


def _rope_tables(s_max):
    inv_freq = ROPE_THETA ** (-jnp.arange(0, ROT_DIM, 2, dtype=F32) / ROT_DIM)
    ang = jnp.arange(s_max, dtype=F32)[:, None] * inv_freq[None, :]
    cos, sin = jnp.cos(ang), jnp.sin(ang)
    rest = HEAD_DIM - ROT_DIM
    cos_t = jnp.concatenate([cos, cos, jnp.ones((s_max, rest), F32)], axis=1)
    sin_t = jnp.concatenate([sin, sin, jnp.zeros((s_max, rest), F32)], axis=1)
    return cos_t, sin_t


def _prepare_layer(w_in, w_gla_gate, b_gla_gate):
    cuts = np.cumsum((ATTN_QKV, ATTN_QKV, ATTN_QKV, GLA_KEY, GLA_KEY, GLA_VAL, GLA_VAL, GLA_RANK, GLA_RANK,
                      D_MODEL, D_MODEL))[:-1]
    qa, ka, va, qg, kg, vg, rg, lrf, lrb, ga, gb = jnp.split(w_in, [int(c) for c in cuts], axis=1)
    w_main = jnp.concatenate([vg, rg, ga, gb, qg, kg], axis=1).astype(BF16)
    w_attn = []
    for g in range(N_GROUPS):
        gs = slice(g * ATTN_OUT, (g + 1) * ATTN_OUT)
        w_attn.append(jnp.concatenate([qa[:, gs], ka[:, gs], va[:, gs]], axis=1).astype(BF16))
    w_lr = jnp.pad(jnp.concatenate([lrf, lrb], axis=1), ((0, 0), (0, LR_COLS - 2 * GLA_RANK))).astype(BF16)
    wg_pad = jnp.zeros((2, LR_COLS, GLA_KEY), F32)
    wg_pad = wg_pad.at[0, 0:GLA_RANK].set(w_gla_gate[0].astype(F32))
    wg_pad = wg_pad.at[1, GLA_RANK:2 * GLA_RANK].set(w_gla_gate[1].astype(F32))
    bg = b_gla_gate.astype(F32).reshape(2, 1, GLA_KEY)
    return w_main, w_attn, w_lr, wg_pad, bg


def _layer(x3, tables, norm_mix, prepared, q_norm, k_norm, gla_norm, wba, wbg, wout, norm_ffn, w1, w2):
    B, S, _ = x3.shape
    T = B * S
    w_main, w_attn, w_lr, wg_pad, bg = prepared
    x = x3.reshape(T, D_MODEL)
    p, lr, xn = _inproj(x, norm_mix.reshape(1, D_MODEL), w_main, w_lr)
    o_groups, lse_groups = [], []
    for g in range(N_GROUPS):
        a_g = _inproj_attn(xn, w_attn[g], ATTN_GROUPS[g][1], B, S)
        o_g, lse_g = _attention_group(a_g, tables, q_norm, k_norm, g, B, S)
        o_groups.append(o_g)
        lse_groups.append(lse_g)
    o_fwd, o_bwd = _gla(p.reshape(B, S, P_COLS), lr.reshape(B, S, LR_COLS), wg_pad, bg, B, S)
    h, hn = _merge(o_groups, lse_groups, o_fwd.reshape(T, GLA_VAL), o_bwd.reshape(T, GLA_VAL), p, x,
                   wba, wbg, wout, gla_norm.reshape(1, GLA_DV), norm_ffn.reshape(1, D_MODEL), B, S)
    y = _ffn(hn, h, w1, w2)
    return y.reshape(B, S, D_MODEL)


def kernel(x_prompt, x_sample, norm_mix, w_in, q_norm, k_norm, w_gla_gate, b_gla_gate, gla_norm,
           w_branch_attn, w_branch_gla, w_out, norm_ffn, w_ff1, w_ff2):
    depth = w_in.shape[0]
    tables = _rope_tables(max(x_prompt.shape[1], x_sample.shape[1]))
    layers = []
    for l in range(depth):
        layers.append((
            norm_mix[l], _prepare_layer(w_in[l], w_gla_gate[l], b_gla_gate[l]),
            q_norm[l].astype(F32), k_norm[l].astype(F32), gla_norm[l].astype(F32),
            w_branch_attn[l].astype(BF16), w_branch_gla[l].astype(BF16), w_out[l].astype(BF16),
            norm_ffn[l].astype(F32), w_ff1[l].astype(BF16), w_ff2[l].astype(BF16)))
    outs = []
    for x in (x_prompt, x_sample):
        for layer in layers:
            x = _layer(x, tables, *layer)
        outs.append(x)
    return tuple(outs)
```

```python
import functools

import jax
import jax.numpy as jnp
import numpy as np
from jax import lax
from jax.experimental import pallas as pl
from jax.experimental.pallas import tpu as pltpu

F32 = jnp.float32
BF16 = jnp.bfloat16

D_MODEL = 2048
HEAD_DIM = 128
ATTN_GROUPS = ((128, 1), (512, 4), (2048, 16))
N_GROUPS = 3
HEADS_PER_GROUP = 8
ATTN_QKV = N_GROUPS * HEADS_PER_GROUP * HEAD_DIM
ATTN_OUT = HEADS_PER_GROUP * HEAD_DIM
ROT_DIM = HEAD_DIM // 4
ROPE_THETA = 500000.0
GLA_HEADS = 4
GLA_KEY = 1024
GLA_VAL = 2048
GLA_DK = 256
GLA_DV = 512
GLA_RANK = 16
GLA_NORMALIZER = 16.0
D_FF = 4 * D_MODEL
EPS = 1e-6

COL_VG = 0
COL_RG = 2048
COL_GA = 4096
COL_GB = 6144
COL_QG = 8192
COL_KG = 9216
P_COLS = 10240
LR_COLS = 128

V7X_VMEM_LIMIT = 56 * 1024 * 1024
NEG = -1e30

ATTN_HALF = 64
ATTN_TQ = 128
ATTN_CHAINS = 8
GLA_BLOCK = 256
GLA_SUB = 64
GLA_HB = 4


def _dot(a, b):
    return jnp.dot(a, b, preferred_element_type=F32)


def _dot_nt(a, b):
    return lax.dot_general(a, b, (((1,), (1,)), ((), ())), preferred_element_type=F32)


def _dot_tn(a, b):
    return lax.dot_general(a, b, (((0,), (0,)), ((), ())), preferred_element_type=F32)


def _split_bf16(x):
    hi = x.astype(BF16)
    lo = (x - hi.astype(F32)).astype(BF16)
    return hi, lo


def _inproj_kernel(x_ref, gain_ref, w_ref, wlr_ref, p_ref, lr_ref, xn_ref):
    j = pl.program_id(1)

    @pl.when(j == 0)
    def _():
        x = x_ref[...]
        ms = jnp.mean(x * x, axis=-1, keepdims=True)
        xn = (x * lax.rsqrt(ms + EPS) * gain_ref[...]).astype(BF16)
        xn_ref[...] = xn
        lr_ref[...] = _dot(xn, wlr_ref[...])

    p_ref[...] = _dot(xn_ref[...], w_ref[...]).astype(BF16)


def _inproj(x, gain, w_main, w_lr, *, tm=1024, tn=1024):
    t = x.shape[0]
    return pl.pallas_call(
        _inproj_kernel,
        grid=(t // tm, P_COLS // tn),
        in_specs=[
            pl.BlockSpec((tm, D_MODEL), lambda i, j: (i, 0)),
            pl.BlockSpec((1, D_MODEL), lambda i, j: (0, 0)),
            pl.BlockSpec((D_MODEL, tn), lambda i, j: (0, j)),
            pl.BlockSpec((D_MODEL, LR_COLS), lambda i, j: (0, 0)),
        ],
        out_specs=[
            pl.BlockSpec((tm, tn), lambda i, j: (i, j)),
            pl.BlockSpec((tm, LR_COLS), lambda i, j: (i, 0)),
            pl.BlockSpec((tm, D_MODEL), lambda i, j: (i, 0)),
        ],
        out_shape=[
            jax.ShapeDtypeStruct((t, P_COLS), BF16),
            jax.ShapeDtypeStruct((t, LR_COLS), F32),
            jax.ShapeDtypeStruct((t, D_MODEL), BF16),
        ],
        compiler_params=pltpu.CompilerParams(
            dimension_semantics=("parallel", "arbitrary"),
            vmem_limit_bytes=V7X_VMEM_LIMIT),
        name="inproj",
    )(x, gain, w_main, w_lr)


def _inproj_attn_kernel(xn_ref, w_ref, o_ref, *scratch, r):
    acc = _dot(xn_ref[...], w_ref[...])
    if r == 1:
        o_ref[0, 0] = acc.astype(BF16)
    else:
        acc_s, = scratch
        n = acc_s.shape[1] // r
        for kb in range(acc_s.shape[0]):
            ls = slice(kb * HEAD_DIM, (kb + 1) * HEAD_DIM)
            acc_s[kb] = acc[:, ls]
            for c in range(r):
                o_ref[0, c, :, ls] = acc_s[kb, pl.ds(c, n, stride=r), :].astype(BF16)


def _inproj_attn(xn, w_g, r, B, S, *, tm=1024, tn=1024):
    t = xn.shape[0]
    per_b = S // tm
    ncol = w_g.shape[1]
    return pl.pallas_call(
        functools.partial(_inproj_attn_kernel, r=r),
        grid=(t // tm, ncol // tn),
        in_specs=[
            pl.BlockSpec((tm, D_MODEL), lambda i, j: (i, 0)),
            pl.BlockSpec((D_MODEL, tn), lambda i, j: (0, j)),
        ],
        out_specs=pl.BlockSpec((1, r, tm // r, tn), lambda i, j: (i // per_b, 0, i % per_b, j)),
        out_shape=jax.ShapeDtypeStruct((B, r, S // r, ncol), BF16),
        scratch_shapes=[] if r == 1 else [pltpu.VMEM((tn // HEAD_DIM, tm, HEAD_DIM), F32)],
        compiler_params=pltpu.CompilerParams(
            dimension_semantics=("parallel", "arbitrary"),
            vmem_limit_bytes=V7X_VMEM_LIMIT),
        name=f"inproj_attn_r{r}",
    )(xn, w_g)


def _attn_kernel(q_ref, k_ref, v_ref, cos_ref, sin_ref, qg_ref, kg_ref,
                 o_ref, lse_ref, qs, ks, vs, bias_s, *, L, hb, U):
    TQ, HALF = ATTN_TQ, ATTN_HALF
    TK = TQ + 2 * HALF
    NT = L // TQ
    TR = min(L, 256)
    hblk = pl.program_id(2)

    @pl.when(hblk == 0)
    def _():
        lse_ref[...] = jnp.zeros_like(lse_ref)

    ri = lax.broadcasted_iota(jnp.int32, (TQ, TK), 0)
    ci = lax.broadcasted_iota(jnp.int32, (TQ, TK), 1)
    d = ci - ri
    band = jnp.where(d < 0, NEG, jnp.where(d > 2 * HALF, NEG, 0.0)).astype(F32)
    first = jnp.where(ci < HALF, NEG, band)
    bias_s[0] = band
    bias_s[1] = first
    bias_s[2] = jnp.where(ci >= TQ + HALF, NEG, band)
    bias_s[3] = jnp.where(ci >= TQ + HALF, NEG, first)

    a = lax.broadcasted_iota(jnp.int32, (HEAD_DIM, HEAD_DIM), 0)
    b = lax.broadcasted_iota(jnp.int32, (HEAD_DIM, HEAD_DIM), 1)
    half = ROT_DIM // 2
    ones_m = jnp.ones((HEAD_DIM, HEAD_DIM), BF16)
    rot_m = jnp.where((b < half) & (a == b + half), -1.0,
                      jnp.where((b >= half) & (b < ROT_DIM) & (a == b - half), 1.0, 0.0)).astype(BF16)

    zpad = jnp.zeros((HALF, HEAD_DIM), BF16)
    for hh in range(hb):
        ks[hh, 0:HALF, :] = zpad
        ks[hh, L + HALF:L + 2 * HALF, :] = zpad
        vs[hh, 0:HALF, 0:HEAD_DIM] = zpad
        vs[hh, L + HALF:L + 2 * HALF, 0:HEAD_DIM] = zpad
        vs[hh, :, HEAD_DIM:2 * HEAD_DIM] = jnp.ones((L + 2 * HALF, HEAD_DIM), BF16)

    qgain = qg_ref[...] * (HEAD_DIM ** -0.5)
    kgain = kg_ref[...]

    def norm_rope(x, gain, cos, sin):
        ssq = _dot((x * x).astype(BF16), ones_m)
        xn = x * lax.rsqrt(ssq * (1.0 / HEAD_DIM) + EPS) * gain
        return xn * cos + _dot(xn.astype(BF16), rot_m) * sin

    def prep(t, carry):
        r0 = pl.multiple_of(t * TR, TR)
        rows = pl.ds(r0, TR)
        prow = pl.ds(pl.multiple_of(r0 + HALF, HALF), TR)
        cos, sin = cos_ref[rows, :], sin_ref[rows, :]
        for hh in range(hb):
            hs = slice(hh * HEAD_DIM, (hh + 1) * HEAD_DIM)
            qs[hh, rows, :] = norm_rope(q_ref[0, 0, rows, hs].astype(F32), qgain, cos, sin).astype(BF16)
            ks[hh, prow, :] = norm_rope(k_ref[0, 0, rows, hs].astype(F32), kgain, cos, sin).astype(BF16)
            vs[hh, prow, 0:HEAD_DIM] = v_ref[0, 0, rows, hs]
        return carry

    lax.fori_loop(0, L // TR, prep, 0)

    lane = lax.broadcasted_iota(jnp.int32, (TQ, HEAD_DIM), 1)

    def tiles(tt, carry):
        for u in range(U):
            t = tt * U + u
            q0 = pl.multiple_of(t * TQ, TQ)
            qrows = pl.ds(q0, TQ)
            krows = pl.ds(q0, TK)
            bias = bias_s[jnp.where(t == 0, 1, 0) + jnp.where(t == NT - 1, 2, 0)]
            lse_tile = lse_ref[0, 0, qrows, :]
            for hh in range(hb):
                hs = slice(hh * HEAD_DIM, (hh + 1) * HEAD_DIM)
                s = _dot_nt(qs[hh, qrows, :], ks[hh, krows, :]) + bias
                m = jnp.max(s, axis=-1, keepdims=True)
                p = jnp.exp(s - m).astype(BF16)
                acc = _dot(p, vs[hh, krows, :])
                den = acc[:, HEAD_DIM:]
                o_ref[0, 0, qrows, hs] = (acc[:, :HEAD_DIM] * (1.0 / den)).astype(BF16)
                lse_tile = jnp.where(lane == hblk * hb + hh, m + jnp.log(den), lse_tile)
            lse_ref[0, 0, qrows, :] = lse_tile
        return carry

    lax.fori_loop(0, NT // U, tiles, 0)


def _attention_group(a_g, tables, q_gain, k_gain, g, B, S):
    window, r = ATTN_GROUPS[g]
    assert window // (2 * r) == ATTN_HALF
    L = S // r
    assert L % ATTN_TQ == 0
    hb = max(1, min(HEADS_PER_GROUP, 8192 // L))
    nhb = HEADS_PER_GROUP // hb
    bw = hb * HEAD_DIM
    U = max(1, min(ATTN_CHAINS // hb, L // ATTN_TQ))
    assert (L // ATTN_TQ) % U == 0
    cos, sin = (t[:S].reshape(L, r * HEAD_DIM) for t in tables)

    def col_map(part):
        return lambda b, c, h: (b, c, 0, part * nhb + h)

    tab_spec = pl.BlockSpec((L, HEAD_DIM), lambda b, c, h: (0, c))
    gain_spec = pl.BlockSpec((1, HEAD_DIM), lambda b, c, h: (0, 0))
    return pl.pallas_call(
        functools.partial(_attn_kernel, L=L, hb=hb, U=U),
        grid=(B, r, nhb),
        in_specs=[
            pl.BlockSpec((1, 1, L, bw), col_map(0)),
            pl.BlockSpec((1, 1, L, bw), col_map(1)),
            pl.BlockSpec((1, 1, L, bw), col_map(2)),
            tab_spec, tab_spec, gain_spec, gain_spec,
        ],
        out_specs=[
            pl.BlockSpec((1, 1, L, bw), lambda b, c, h: (b, c, 0, h)),
            pl.BlockSpec((1, 1, L, HEAD_DIM), lambda b, c, h: (b, c, 0, 0)),
        ],
        out_shape=[
            jax.ShapeDtypeStruct((B, r, L, ATTN_OUT), BF16),
            jax.ShapeDtypeStruct((B, r, L, HEAD_DIM), F32),
        ],
        scratch_shapes=[
            pltpu.VMEM((hb, L, HEAD_DIM), BF16),
            pltpu.VMEM((hb, L + 2 * ATTN_HALF, HEAD_DIM), BF16),
            pltpu.VMEM((hb, L + 2 * ATTN_HALF, 2 * HEAD_DIM), BF16),
            pltpu.VMEM((4, ATTN_TQ, ATTN_TQ + 2 * ATTN_HALF), F32),
        ],
        compiler_params=pltpu.CompilerParams(
            dimension_semantics=("parallel", "parallel", "arbitrary"),
            vmem_limit_bytes=V7X_VMEM_LIMIT),
        name=f"attn_g{g}",
    )(a_g, a_g, a_g, cos, sin, q_gain[g:g + 1], k_gain[g:g + 1])


def _gla_kernel(qf_ref, kf_ref, vf_ref, lrf_ref, qb_ref, kb_ref, vb_ref, lrb_ref, wg_ref, bg_ref,
                of_ref, ob_ref, stf, stb):
    C, SC = GLA_BLOCK, GLA_SUB
    NS = C // SC
    n = pl.program_id(2)

    @pl.when(n == 0)
    def _():
        stf[...] = jnp.zeros_like(stf)
        stb[...] = jnp.zeros_like(stb)

    ri = lax.broadcasted_iota(jnp.int32, (C, C), 0)
    ci = lax.broadcasted_iota(jnp.int32, (C, C), 1)

    def direction(q_ref, k_ref, v_ref, lr_ref, o_ref, st, d, backward, h):
        ks = slice(h * GLA_DK, (h + 1) * GLA_DK)
        vs = slice(h * GLA_DV, (h + 1) * GLA_DV)
        z = _dot(lr_ref[0].astype(BF16), wg_ref[d, :, ks]) + bg_ref[d, :, ks]
        yield
        softplus = jnp.maximum(-z, 0.0) + jnp.log(1.0 + jnp.exp(-jnp.abs(z)))
        g_hi, g_lo = _split_bf16(softplus * (-1.0 / GLA_NORMALIZER))
        tri = jnp.where((ci >= ri) if backward else (ri >= ci), 1.0, 0.0).astype(BF16)
        cum = _dot(tri, g_hi) + _dot(tri, g_lo)
        yield
        mid_row = SC // 2 if backward else SC // 2 - 1
        end_row = 0 if backward else C - 1
        mids = [cum[I * SC + mid_row:I * SC + mid_row + 1, :] for I in range(NS)]
        end = cum[end_row:end_row + 1, :]
        sub = [slice(I * SC, (I + 1) * SC) for I in range(NS)]
        dl = jnp.concatenate([cum[sub[I], :] - mids[I] for I in range(NS)], axis=0)
        qd = q_ref[0, :, ks].astype(F32) * (GLA_DK ** -0.5) * jnp.exp(dl)
        kd = k_ref[0, :, ks].astype(F32) * jnp.exp(-dl)
        qd_b = qd.astype(BF16)
        kd_b = kd.astype(BF16)
        att_rows = []
        for I in range(NS):
            blocks = []
            for J in range(NS):
                if (J > I) if backward else (J < I):
                    blocks.append((kd[sub[J], :] * jnp.exp(mids[I] - mids[J])).astype(BF16))
                else:
                    blocks.append(kd_b[sub[J], :])
            att_rows.append(_dot_nt(qd_b[sub[I], :], jnp.concatenate(blocks, axis=0)))
        yield
        mask = (ci > ri) if backward else (ri >= ci)
        att = jnp.where(mask, jnp.concatenate(att_rows, axis=0), 0.0).astype(BF16)
        qi = jnp.concatenate([qd[sub[I], :] * jnp.exp(mids[I]) for I in range(NS)], axis=0).astype(BF16)
        k2 = jnp.concatenate([kd[sub[I], :] * jnp.exp(end - mids[I]) for I in range(NS)], axis=0).astype(BF16)
        v = v_ref[0, :, vs]
        state = st[h]
        o = _dot(att, v) + _dot_nt(qi, state.astype(BF16))
        upd = _dot_tn(v, k2)
        yield
        o_ref[0, :, vs] = o.astype(BF16)
        st[h] = state * jnp.exp(end) + upd

    chains = []
    for h in range(GLA_HB):
        chains.append(direction(qf_ref, kf_ref, vf_ref, lrf_ref, of_ref, stf, 0, False, h))
        chains.append(direction(qb_ref, kb_ref, vb_ref, lrb_ref, ob_ref, stb, 1, True, h))
    while chains:
        alive = []
        for chain in chains:
            if next(chain, chain) is not chain:
                alive.append(chain)
        chains = alive


def _gla(p3, lr3, wg_pad, bg, B, S):
    TS = GLA_BLOCK
    NB = S // TS
    HB = GLA_HB
    kq, kk, kv = COL_QG // (HB * GLA_DK), COL_KG // (HB * GLA_DK), COL_VG // (HB * GLA_DV)

    def fwd(base):
        return lambda b, h, n: (b, n, base + h)

    def bwd(base):
        return lambda b, h, n: (b, NB - 1 - n, base + h)

    def specs(m, lr_map):
        return [
            pl.BlockSpec((1, TS, HB * GLA_DK), m(kq)),
            pl.BlockSpec((1, TS, HB * GLA_DK), m(kk)),
            pl.BlockSpec((1, TS, HB * GLA_DV), m(kv)),
            pl.BlockSpec((1, TS, LR_COLS), lr_map),
        ]

    return pl.pallas_call(
        _gla_kernel,
        grid=(B, GLA_HEADS // HB, NB),
        in_specs=specs(fwd, lambda b, h, n: (b, n, 0)) + specs(bwd, lambda b, h, n: (b, NB - 1 - n, 0)) + [
            pl.BlockSpec((2, LR_COLS, HB * GLA_DK), lambda b, h, n: (0, 0, h)),
            pl.BlockSpec((2, 1, HB * GLA_DK), lambda b, h, n: (0, 0, h)),
        ],
        out_specs=[
            pl.BlockSpec((1, TS, HB * GLA_DV), lambda b, h, n: (b, n, h)),
            pl.BlockSpec((1, TS, HB * GLA_DV), lambda b, h, n: (b, NB - 1 - n, h)),
        ],
        out_shape=[jax.ShapeDtypeStruct((B, S, GLA_VAL), BF16)] * 2,
        scratch_shapes=[
            pltpu.VMEM((HB, GLA_DV, GLA_DK), F32),
            pltpu.VMEM((HB, GLA_DV, GLA_DK), F32),
        ],
        compiler_params=pltpu.CompilerParams(
            dimension_semantics=("parallel", "parallel", "arbitrary"),
            vmem_limit_bytes=V7X_VMEM_LIMIT),
        name="gla",
    )(p3, p3, p3, lr3, p3, p3, p3, lr3, wg_pad, bg)


def _sigmoid(x):
    return 1.0 / (1.0 + jnp.exp(-x))


def _merge_kernel(o0_ref, o1_ref, o2_ref, l0_ref, l1_ref, l2_ref, of_ref, ob_ref, rg_ref, ga_ref, gb_ref,
                  x_ref, wba_ref, wbg_ref, wout_ref, gnorm_ref, nffn_ref, h_ref, hn_ref,
                  oa_s, og_s, oil_s, lil_s):
    tm = x_ref.shape[0]
    for gi, (o_ref, l_ref) in enumerate(((o1_ref, l1_ref), (o2_ref, l2_ref))):
        r = o_ref.shape[1]
        for c in range(r):
            dst = pl.ds(c, tm // r, stride=r)
            lil_s[gi, dst, :] = l_ref[0, c]
            for hh in range(HEADS_PER_GROUP):
                oil_s[gi, hh, dst, :] = o_ref[0, c, :, hh * HEAD_DIM:(hh + 1) * HEAD_DIM].astype(F32)

    l0, l1, l2 = l0_ref[0, 0], lil_s[0], lil_s[1]
    m = jnp.maximum(jnp.maximum(l0, l1), l2)
    e0, e1, e2 = jnp.exp(l0 - m), jnp.exp(l1 - m), jnp.exp(l2 - m)
    inv = 1.0 / (e0 + e1 + e2)
    w0, w1, w2 = e0 * inv, e1 * inv, e2 * inv
    for hh in range(HEADS_PER_GROUP):
        hs = slice(hh * HEAD_DIM, (hh + 1) * HEAD_DIM)
        comb = (w0[:, hh:hh + 1] * o0_ref[0, 0, :, hs].astype(F32)
                + w1[:, hh:hh + 1] * oil_s[0, hh]
                + w2[:, hh:hh + 1] * oil_s[1, hh])
        oa_s[:, hs] = comb.astype(BF16)
    u_a = _dot(oa_s[...], wba_ref[...])

    for h in range(GLA_HEADS):
        vs = slice(h * GLA_DV, (h + 1) * GLA_DV)
        og = of_ref[:, vs].astype(F32) + ob_ref[:, vs].astype(F32)
        ms = jnp.mean(og * og, axis=-1, keepdims=True)
        ogn = og * lax.rsqrt(ms + EPS) * gnorm_ref[...]
        rg = rg_ref[:, vs].astype(F32)
        og_s[:, vs] = (ogn * (rg * _sigmoid(rg))).astype(BF16)
    u_b = _dot(og_s[...], wbg_ref[...])

    merged = _sigmoid(ga_ref[...].astype(F32)) * u_a + _sigmoid(gb_ref[...].astype(F32)) * u_b
    h = x_ref[...] + _dot(merged.astype(BF16), wout_ref[...])
    h_ref[...] = h
    ms = jnp.mean(h * h, axis=-1, keepdims=True)
    hn_ref[...] = (h * lax.rsqrt(ms + EPS) * nffn_ref[...]).astype(BF16)


def _merge(o_groups, lse_groups, o_fwd, o_bwd, p2, x, wba, wbg, wout, gnorm, nffn, B, S, *, tm=256):
    t = x.shape[0]
    per_b = S // tm
    row = lambda b, i: (b * per_b + i, 0)
    const = lambda b, i: (0, 0)

    def resident(shape):
        return pl.BlockSpec(shape, const, pipeline_mode=pl.Buffered(1))

    def split_spec(r, width):
        return pl.BlockSpec((1, r, tm // r, width), lambda b, i: (b, 0, i, 0))

    rs = [r for _, r in ATTN_GROUPS]
    return pl.pallas_call(
        _merge_kernel,
        grid=(B, per_b),
        in_specs=[split_spec(r, ATTN_OUT) for r in rs] + [split_spec(r, HEAD_DIM) for r in rs] + [
            pl.BlockSpec((tm, GLA_VAL), row),
            pl.BlockSpec((tm, GLA_VAL), row),
            pl.BlockSpec((tm, GLA_VAL), lambda b, i: (b * per_b + i, COL_RG // GLA_VAL)),
            pl.BlockSpec((tm, D_MODEL), lambda b, i: (b * per_b + i, COL_GA // D_MODEL)),
            pl.BlockSpec((tm, D_MODEL), lambda b, i: (b * per_b + i, COL_GB // D_MODEL)),
            pl.BlockSpec((tm, D_MODEL), row),
            resident((ATTN_OUT, D_MODEL)),
            resident((GLA_VAL, D_MODEL)),
            resident((D_MODEL, D_MODEL)),
            resident((1, GLA_DV)),
            resident((1, D_MODEL)),
        ],
        out_specs=[pl.BlockSpec((tm, D_MODEL), row), pl.BlockSpec((tm, D_MODEL), row)],
        out_shape=[jax.ShapeDtypeStruct((t, D_MODEL), F32), jax.ShapeDtypeStruct((t, D_MODEL), BF16)],
        scratch_shapes=[
            pltpu.VMEM((tm, ATTN_OUT), BF16),
            pltpu.VMEM((tm, GLA_VAL), BF16),
            pltpu.VMEM((2, HEADS_PER_GROUP, tm, HEAD_DIM), F32),
            pltpu.VMEM((2, tm, HEAD_DIM), F32),
        ],
        compiler_params=pltpu.CompilerParams(
            dimension_semantics=("parallel", "parallel"),
            vmem_limit_bytes=V7X_VMEM_LIMIT),
        name="merge",
    )(*o_groups, *lse_groups, o_fwd, o_bwd, p2, p2, p2, x, wba, wbg, wout, gnorm, nffn)


def _ffn_kernel(hn_ref, w1_ref, w2_ref, h_ref, y_ref):
    j = pl.program_id(1)
    a = _dot(hn_ref[...], w1_ref[...])
    a = jnp.square(jnp.maximum(a, 0.0)).astype(BF16)
    y = _dot(a, w2_ref[...])

    @pl.when(j == 0)
    def _():
        y_ref[...] = h_ref[...] + y

    @pl.when(j > 0)
    def _():
        y_ref[...] += y


def _ffn(hn, h, w1, w2, *, tm=512, tf=1024):
    t = hn.shape[0]
    return pl.pallas_call(
        _ffn_kernel,
        grid=(t // tm, D_FF // tf),
        in_specs=[
            pl.BlockSpec((tm, D_MODEL), lambda i, j: (i, 0)),
            pl.BlockSpec((D_MODEL, tf), lambda i, j: (0, j)),
            pl.BlockSpec((tf, D_MODEL), lambda i, j: (j, 0)),
            pl.BlockSpec((tm, D_MODEL), lambda i, j: (i, 0)),
        ],
        out_specs=pl.BlockSpec((tm, D_MODEL), lambda i, j: (i, 0)),
        out_shape=jax.ShapeDtypeStruct((t, D_MODEL), F32),
        compiler_params=pltpu.CompilerParams(
            dimension_semantics=("parallel", "arbitrary"),
            vmem_limit_bytes=V7X_VMEM_LIMIT),
        name="ffn",
    )(hn, w1, w2, h)


def _rope_tables(s_max):
    inv_freq = ROPE_THETA ** (-jnp.arange(0, ROT_DIM, 2, dtype=F32) / ROT_DIM)
    ang = jnp.arange(s_max, dtype=F32)[:, None] * inv_freq[None, :]
    cos, sin = jnp.cos(ang), jnp.sin(ang)
    rest = HEAD_DIM - ROT_DIM
    cos_t = jnp.concatenate([cos, cos, jnp.ones((s_max, rest), F32)], axis=1)
    sin_t = jnp.concatenate([sin, sin, jnp.zeros((s_max, rest), F32)], axis=1)
    return cos_t, sin_t


def _prepare_layer(w_in, w_gla_gate, b_gla_gate):
    cuts = np.cumsum((ATTN_QKV, ATTN_QKV, ATTN_QKV, GLA_KEY, GLA_KEY, GLA_VAL, GLA_VAL, GLA_RANK, GLA_RANK,
                      D_MODEL, D_MODEL))[:-1]
    qa, ka, va, qg, kg, vg, rg, lrf, lrb, ga, gb = jnp.split(w_in, [int(c) for c in cuts], axis=1)
    w_main = jnp.concatenate([vg, rg, ga, gb, qg, kg], axis=1).astype(BF16)
    w_attn = []
    for g in range(N_GROUPS):
        gs = slice(g * ATTN_OUT, (g + 1) * ATTN_OUT)
        w_attn.append(jnp.concatenate([qa[:, gs], ka[:, gs], va[:, gs]], axis=1).astype(BF16))
    w_lr = jnp.pad(jnp.concatenate([lrf, lrb], axis=1), ((0, 0), (0, LR_COLS - 2 * GLA_RANK))).astype(BF16)
    wg_pad = jnp.zeros((2, LR_COLS, GLA_KEY), F32)
    wg_pad = wg_pad.at[0, 0:GLA_RANK].set(w_gla_gate[0].astype(F32))
    wg_pad = wg_pad.at[1, GLA_RANK:2 * GLA_RANK].set(w_gla_gate[1].astype(F32)).astype(BF16)
    bg = b_gla_gate.astype(F32).reshape(2, 1, GLA_KEY)
    return w_main, w_attn, w_lr, wg_pad, bg


def _layer(x3, tables, norm_mix, prepared, q_norm, k_norm, gla_norm, wba, wbg, wout, norm_ffn, w1, w2):
    B, S, _ = x3.shape
    T = B * S
    w_main, w_attn, w_lr, wg_pad, bg = prepared
    x = x3.reshape(T, D_MODEL)
    p, lr, xn = _inproj(x, norm_mix.reshape(1, D_MODEL), w_main, w_lr)
    o_groups, lse_groups = [], []
    for g in range(N_GROUPS):
        a_g = _inproj_attn(xn, w_attn[g], ATTN_GROUPS[g][1], B, S)
        o_g, lse_g = _attention_group(a_g, tables, q_norm, k_norm, g, B, S)
        o_groups.append(o_g)
        lse_groups.append(lse_g)
    o_fwd, o_bwd = _gla(p.reshape(B, S, P_COLS), lr.reshape(B, S, LR_COLS), wg_pad, bg, B, S)
    h, hn = _merge(o_groups, lse_groups, o_fwd.reshape(T, GLA_VAL), o_bwd.reshape(T, GLA_VAL), p, x,
                   wba, wbg, wout, gla_norm.reshape(1, GLA_DV), norm_ffn.reshape(1, D_MODEL), B, S)
    y = _ffn(hn, h, w1, w2)
    return y.reshape(B, S, D_MODEL)


def kernel(x_prompt, x_sample, norm_mix, w_in, q_norm, k_norm, w_gla_gate, b_gla_gate, gla_norm,
           w_branch_attn, w_branch_gla, w_out, norm_ffn, w_ff1, w_ff2):
    depth = w_in.shape[0]
    tables = _rope_tables(max(x_prompt.shape[1], x_sample.shape[1]))
    layers = []
    for l in range(depth):
        layers.append((
            norm_mix[l], _prepare_layer(w_in[l], w_gla_gate[l], b_gla_gate[l]),
            q_norm[l].astype(F32), k_norm[l].astype(F32), gla_norm[l].astype(F32),
            w_branch_attn[l].astype(BF16), w_branch_gla[l].astype(BF16), w_out[l].astype(BF16),
            norm_ffn[l].astype(F32), w_ff1[l].astype(BF16), w_ff2[l].astype(BF16)))
    outs = []
    for x in (x_prompt, x_sample):
        for layer in layers:
            x = _layer(x, tables, *layer)
        outs.append(x)
    return tuple(outs)
```

```python
import functools

import jax
import jax.numpy as jnp
from jax import lax
from jax.experimental import pallas as pl
from jax.experimental.pallas import tpu as pltpu

F32 = jnp.float32
BF16 = jnp.bfloat16

D_MODEL = 2048
HEAD_DIM = 128
ATTN_GROUPS = ((128, 1), (512, 4), (2048, 16))
N_GROUPS = 3
HEADS_PER_GROUP = 8
ATTN_QKV = N_GROUPS * HEADS_PER_GROUP * HEAD_DIM
ATTN_OUT = HEADS_PER_GROUP * HEAD_DIM
ROT_DIM = HEAD_DIM // 4
ROPE_THETA = 500000.0
GLA_HEADS = 4
GLA_KEY = 1024
GLA_VAL = 2048
GLA_DK = 256
GLA_DV = 512
GLA_RANK = 16
GLA_NORMALIZER = 16.0
D_FF = 4 * D_MODEL
EPS = 1e-6

W_COL_QG = 3 * ATTN_QKV
W_COL_LR = W_COL_QG + 2 * GLA_KEY + 2 * GLA_VAL
W_COL_GATES = W_COL_LR + 2 * GLA_RANK
COL_QG = 0
COL_KG = 1024
COL_VG = 2048
COL_RG = 4096
P1_COLS = 6144
COL_GA = 0
COL_GB = 2048
LR_COLS = 128
PROJ_CHUNK = 256

V7X_VMEM_LIMIT = 56 * 1024 * 1024
NEG = -1e30

ATTN_HALF = 64
ATTN_TQ = 128
ATTN_CHAINS = 8
GLA_BLOCK = 256
GLA_SUB = 64
GLA_HB = 4


def _dot(a, b):
    return jnp.dot(a, b, preferred_element_type=F32)


def _dot_nt(a, b):
    return lax.dot_general(a, b, (((1,), (1,)), ((), ())), preferred_element_type=F32)


def _dot_tn(a, b):
    return lax.dot_general(a, b, (((0,), (0,)), ((), ())), preferred_element_type=F32)


def _split_bf16(x):
    hi = x.astype(BF16)
    lo = (x - hi.astype(F32)).astype(BF16)
    return hi, lo


def _xnorm_kernel(x_ref, gain_ref, wlr_ref, xn_ref, lr_ref):
    x = x_ref[...]
    ms = jnp.mean(x * x, axis=-1, keepdims=True)
    xn = (x * lax.rsqrt(ms + EPS) * gain_ref[...]).astype(BF16)
    xn_ref[...] = xn
    lr_ref[...] = _dot(xn, wlr_ref[...])


def _xnorm(x, gain, w_lr, *, tm=512):
    t = x.shape[0]
    return pl.pallas_call(
        _xnorm_kernel,
        grid=(t // tm,),
        in_specs=[
            pl.BlockSpec((tm, D_MODEL), lambda i: (i, 0)),
            pl.BlockSpec((1, D_MODEL), lambda i: (0, 0)),
            pl.BlockSpec((D_MODEL, LR_COLS), lambda i: (0, 0)),
        ],
        out_specs=[
            pl.BlockSpec((tm, D_MODEL), lambda i: (i, 0)),
            pl.BlockSpec((tm, LR_COLS), lambda i: (i, 0)),
        ],
        out_shape=[
            jax.ShapeDtypeStruct((t, D_MODEL), BF16),
            jax.ShapeDtypeStruct((t, LR_COLS), F32),
        ],
        compiler_params=pltpu.CompilerParams(
            dimension_semantics=("parallel",),
            vmem_limit_bytes=V7X_VMEM_LIMIT),
        name="xnorm",
    )(x, gain, w_lr)


def _proj_kernel(xn_ref, w_ref, o_ref, wb_s, *scratch, r):
    @pl.when(pl.program_id(1) == 0)
    def _():
        wb_s[...] = w_ref[...].astype(BF16)

    xn = xn_ref[...]
    tm, tn = xn.shape[0], wb_s.shape[1]

    def store(cb, acc):
        if r == 1:
            o_ref[:, cb * PROJ_CHUNK:(cb + 1) * PROJ_CHUNK] = acc.astype(BF16)
            return
        acc_s, = scratch
        for kb in range(PROJ_CHUNK // HEAD_DIM):
            lb = cb * (PROJ_CHUNK // HEAD_DIM) + kb
            acc_s[lb] = acc[:, kb * HEAD_DIM:(kb + 1) * HEAD_DIM]
            for c in range(r):
                o_ref[0, c, :, lb * HEAD_DIM:(lb + 1) * HEAD_DIM] = (
                    acc_s[lb, pl.ds(c, tm // r, stride=r), :].astype(BF16))

    pending = None
    for cb in range(tn // PROJ_CHUNK):
        acc = _dot(xn, wb_s[:, cb * PROJ_CHUNK:(cb + 1) * PROJ_CHUNK])
        if pending is not None:
            store(*pending)
        pending = (cb, acc)
    store(*pending)


def _proj(xn, w, col_block, ncol_blocks, r, B, S, name, *, tm=1024, tn=1024):
    t = xn.shape[0]
    per_b = S // tm
    if r == 1:
        out_spec = pl.BlockSpec((tm, tn), lambda j, i: (i, j))
        out_shape = jax.ShapeDtypeStruct((t, ncol_blocks * tn), BF16)
        scratch = []
    else:
        out_spec = pl.BlockSpec((1, r, tm // r, tn), lambda j, i: (i // per_b, 0, i % per_b, j))
        out_shape = jax.ShapeDtypeStruct((B, r, S // r, ncol_blocks * tn), BF16)
        scratch = [pltpu.VMEM((tn // HEAD_DIM, tm, HEAD_DIM), F32)]
    return pl.pallas_call(
        functools.partial(_proj_kernel, r=r),
        grid=(ncol_blocks, t // tm),
        in_specs=[
            pl.BlockSpec((tm, D_MODEL), lambda j, i: (i, 0)),
            pl.BlockSpec((D_MODEL, tn), lambda j, i: (0, col_block(j))),
        ],
        out_specs=out_spec,
        out_shape=out_shape,
        scratch_shapes=[pltpu.VMEM((D_MODEL, tn), BF16)] + scratch,
        compiler_params=pltpu.CompilerParams(
            dimension_semantics=("parallel", "arbitrary"),
            vmem_limit_bytes=V7X_VMEM_LIMIT),
        name=name,
    )(xn, w)


def _attn_kernel(q_ref, k_ref, v_ref, cos_ref, sin_ref, qg_ref, kg_ref,
                 o_ref, lse_ref, qs, ks, vs, bias_s, *, L, hb, U):
    TQ, HALF = ATTN_TQ, ATTN_HALF
    TK = TQ + 2 * HALF
    NT = L // TQ
    TR = min(L, 256)
    hblk = pl.program_id(2)

    @pl.when(hblk == 0)
    def _():
        lse_ref[...] = jnp.zeros_like(lse_ref)

    ri = lax.broadcasted_iota(jnp.int32, (TQ, TK), 0)
    ci = lax.broadcasted_iota(jnp.int32, (TQ, TK), 1)
    d = ci - ri
    band = jnp.where(d < 0, NEG, jnp.where(d > 2 * HALF, NEG, 0.0)).astype(F32)
    first = jnp.where(ci < HALF, NEG, band)
    bias_s[0] = band
    bias_s[1] = first
    bias_s[2] = jnp.where(ci >= TQ + HALF, NEG, band)
    bias_s[3] = jnp.where(ci >= TQ + HALF, NEG, first)

    a = lax.broadcasted_iota(jnp.int32, (HEAD_DIM, HEAD_DIM), 0)
    b = lax.broadcasted_iota(jnp.int32, (HEAD_DIM, HEAD_DIM), 1)
    half = ROT_DIM // 2
    ones_m = jnp.ones((HEAD_DIM, HEAD_DIM), BF16)
    rot_m = jnp.where((b < half) & (a == b + half), -1.0,
                      jnp.where((b >= half) & (b < ROT_DIM) & (a == b - half), 1.0, 0.0)).astype(BF16)

    zpad = jnp.zeros((HALF, HEAD_DIM), BF16)
    for hh in range(hb):
        ks[hh, 0:HALF, :] = zpad
        ks[hh, L + HALF:L + 2 * HALF, :] = zpad
        vs[hh, 0:HALF, 0:HEAD_DIM] = zpad
        vs[hh, L + HALF:L + 2 * HALF, 0:HEAD_DIM] = zpad
        vs[hh, :, HEAD_DIM:2 * HEAD_DIM] = jnp.ones((L + 2 * HALF, HEAD_DIM), BF16)

    qgain = qg_ref[...] * (HEAD_DIM ** -0.5)
    kgain = kg_ref[...]

    def norm_rope(x, gain, cos, sin):
        ssq = _dot((x * x).astype(BF16), ones_m)
        xn = x * lax.rsqrt(ssq * (1.0 / HEAD_DIM) + EPS) * gain
        return xn * cos + _dot(xn.astype(BF16), rot_m) * sin

    def prep(t, carry):
        r0 = pl.multiple_of(t * TR, TR)
        rows = pl.ds(r0, TR)
        prow = pl.ds(pl.multiple_of(r0 + HALF, HALF), TR)
        cos, sin = cos_ref[rows, :], sin_ref[rows, :]
        for hh in range(hb):
            hs = slice(hh * HEAD_DIM, (hh + 1) * HEAD_DIM)
            qs[hh, rows, :] = norm_rope(q_ref[0, 0, rows, hs].astype(F32), qgain, cos, sin).astype(BF16)
            ks[hh, prow, :] = norm_rope(k_ref[0, 0, rows, hs].astype(F32), kgain, cos, sin).astype(BF16)
            vs[hh, prow, 0:HEAD_DIM] = v_ref[0, 0, rows, hs]
        return carry

    lax.fori_loop(0, L // TR, prep, 0)

    lane = lax.broadcasted_iota(jnp.int32, (TQ, HEAD_DIM), 1)

    def tiles(tt, carry):
        for u in range(U):
            t = tt * U + u
            q0 = pl.multiple_of(t * TQ, TQ)
            qrows = pl.ds(q0, TQ)
            krows = pl.ds(q0, TK)
            bias = bias_s[jnp.where(t == 0, 1, 0) + jnp.where(t == NT - 1, 2, 0)]
            lse_tile = lse_ref[0, 0, qrows, :]
            for hh in range(hb):
                hs = slice(hh * HEAD_DIM, (hh + 1) * HEAD_DIM)
                s = _dot_nt(qs[hh, qrows, :], ks[hh, krows, :]) + bias
                m = jnp.max(s, axis=-1, keepdims=True)
                p = jnp.exp(s - m).astype(BF16)
                acc = _dot(p, vs[hh, krows, :])
                den = acc[:, HEAD_DIM:]
                o_ref[0, 0, qrows, hs] = (acc[:, :HEAD_DIM] * (1.0 / den)).astype(BF16)
                lse_tile = jnp.where(lane == hblk * hb + hh, m + jnp.log(den), lse_tile)
            lse_ref[0, 0, qrows, :] = lse_tile
        return carry

    lax.fori_loop(0, NT // U, tiles, 0)


def _attention_group(a_g, tables, q_gain, k_gain, g, B, S):
    window, r = ATTN_GROUPS[g]
    assert window // (2 * r) == ATTN_HALF
    L = S // r
    assert L % ATTN_TQ == 0
    hb = max(1, min(HEADS_PER_GROUP, 8192 // L))
    nhb = HEADS_PER_GROUP // hb
    bw = hb * HEAD_DIM
    U = max(1, min(ATTN_CHAINS // hb, L // ATTN_TQ))
    assert (L // ATTN_TQ) % U == 0
    cos, sin = (t[:S].reshape(L, r * HEAD_DIM) for t in tables)

    def col_map(part):
        return lambda b, c, h: (b, c, 0, part * nhb + h)

    tab_spec = pl.BlockSpec((L, HEAD_DIM), lambda b, c, h: (0, c))
    gain_spec = pl.BlockSpec((1, HEAD_DIM), lambda b, c, h: (0, 0))
    return pl.pallas_call(
        functools.partial(_attn_kernel, L=L, hb=hb, U=U),
        grid=(B, r, nhb),
        in_specs=[
            pl.BlockSpec((1, 1, L, bw), col_map(0)),
            pl.BlockSpec((1, 1, L, bw), col_map(1)),
            pl.BlockSpec((1, 1, L, bw), col_map(2)),
            tab_spec, tab_spec, gain_spec, gain_spec,
        ],
        out_specs=[
            pl.BlockSpec((1, 1, L, bw), lambda b, c, h: (b, c, 0, h)),
            pl.BlockSpec((1, 1, L, HEAD_DIM), lambda b, c, h: (b, c, 0, 0)),
        ],
        out_shape=[
            jax.ShapeDtypeStruct((B, r, L, ATTN_OUT), BF16),
            jax.ShapeDtypeStruct((B, r, L, HEAD_DIM), F32),
        ],
        scratch_shapes=[
            pltpu.VMEM((hb, L, HEAD_DIM), BF16),
            pltpu.VMEM((hb, L + 2 * ATTN_HALF, HEAD_DIM), BF16),
            pltpu.VMEM((hb, L + 2 * ATTN_HALF, 2 * HEAD_DIM), BF16),
            pltpu.VMEM((4, ATTN_TQ, ATTN_TQ + 2 * ATTN_HALF), F32),
        ],
        compiler_params=pltpu.CompilerParams(
            dimension_semantics=("parallel", "parallel", "arbitrary"),
            vmem_limit_bytes=V7X_VMEM_LIMIT),
        name=f"attn_g{g}",
    )(a_g, a_g, a_g, cos, sin, q_gain[g:g + 1], k_gain[g:g + 1])


def _gla_kernel(qf_ref, kf_ref, vf_ref, lrf_ref, qb_ref, kb_ref, vb_ref, lrb_ref, wg_ref, bg_ref,
                of_ref, ob_ref, stf, stb):
    C, SC = GLA_BLOCK, GLA_SUB
    NS = C // SC
    n = pl.program_id(2)

    @pl.when(n == 0)
    def _():
        stf[...] = jnp.zeros_like(stf)
        stb[...] = jnp.zeros_like(stb)

    ri = lax.broadcasted_iota(jnp.int32, (C, C), 0)
    ci = lax.broadcasted_iota(jnp.int32, (C, C), 1)

    def direction(q_ref, k_ref, v_ref, lr_ref, o_ref, st, d, backward, h):
        ks = slice(h * GLA_DK, (h + 1) * GLA_DK)
        vs = slice(h * GLA_DV, (h + 1) * GLA_DV)
        z = _dot(lr_ref[0].astype(BF16), wg_ref[d, :, ks]) + bg_ref[d, :, ks]
        yield
        softplus = jnp.maximum(-z, 0.0) + jnp.log(1.0 + jnp.exp(-jnp.abs(z)))
        g_hi, g_lo = _split_bf16(softplus * (-1.0 / GLA_NORMALIZER))
        tri = jnp.where((ci >= ri) if backward else (ri >= ci), 1.0, 0.0).astype(BF16)
        cum = _dot(tri, g_hi) + _dot(tri, g_lo)
        yield
        mid_row = SC // 2 if backward else SC // 2 - 1
        end_row = 0 if backward else C - 1
        mids = [cum[I * SC + mid_row:I * SC + mid_row + 1, :] for I in range(NS)]
        end = cum[end_row:end_row + 1, :]
        sub = [slice(I * SC, (I + 1) * SC) for I in range(NS)]
        dl = jnp.concatenate([cum[sub[I], :] - mids[I] for I in range(NS)], axis=0)
        qd = q_ref[0, :, ks].astype(F32) * (GLA_DK ** -0.5) * jnp.exp(dl)
        kd = k_ref[0, :, ks].astype(F32) * jnp.exp(-dl)
        qd_b = qd.astype(BF16)
        kd_b = kd.astype(BF16)
        att_rows = []
        for I in range(NS):
            blocks = []
            for J in range(NS):
                if (J > I) if backward else (J < I):
                    blocks.append((kd[sub[J], :] * jnp.exp(mids[I] - mids[J])).astype(BF16))
                else:
                    blocks.append(kd_b[sub[J], :])
            att_rows.append(_dot_nt(qd_b[sub[I], :], jnp.concatenate(blocks, axis=0)))
        yield
        mask = (ci > ri) if backward else (ri >= ci)
        att = jnp.where(mask, jnp.concatenate(att_rows, axis=0), 0.0).astype(BF16)
        qi = jnp.concatenate([qd[sub[I], :] * jnp.exp(mids[I]) for I in range(NS)], axis=0).astype(BF16)
        k2 = jnp.concatenate([kd[sub[I], :] * jnp.exp(end - mids[I]) for I in range(NS)], axis=0).astype(BF16)
        v = v_ref[0, :, vs]
        state = st[h]
        o = _dot(att, v) + _dot_nt(qi, state.astype(BF16))
        upd = _dot_tn(v, k2)
        yield
        o_ref[0, :, vs] = o.astype(BF16)
        st[h] = state * jnp.exp(end) + upd

    chains = []
    for h in range(GLA_HB):
        chains.append(direction(qf_ref, kf_ref, vf_ref, lrf_ref, of_ref, stf, 0, False, h))
        chains.append(direction(qb_ref, kb_ref, vb_ref, lrb_ref, ob_ref, stb, 1, True, h))
    while chains:
        alive = []
        for chain in chains:
            if next(chain, chain) is not chain:
                alive.append(chain)
        chains = alive


def _gla(p3, lr3, wg_pad, bg, B, S):
    TS = GLA_BLOCK
    NB = S // TS
    HB = GLA_HB
    kq, kk, kv = COL_QG // (HB * GLA_DK), COL_KG // (HB * GLA_DK), COL_VG // (HB * GLA_DV)

    def fwd(base):
        return lambda b, h, n: (b, n, base + h)

    def bwd(base):
        return lambda b, h, n: (b, NB - 1 - n, base + h)

    def specs(m, lr_map):
        return [
            pl.BlockSpec((1, TS, HB * GLA_DK), m(kq)),
            pl.BlockSpec((1, TS, HB * GLA_DK), m(kk)),
            pl.BlockSpec((1, TS, HB * GLA_DV), m(kv)),
            pl.BlockSpec((1, TS, LR_COLS), lr_map),
        ]

    return pl.pallas_call(
        _gla_kernel,
        grid=(B, GLA_HEADS // HB, NB),
        in_specs=specs(fwd, lambda b, h, n: (b, n, 0)) + specs(bwd, lambda b, h, n: (b, NB - 1 - n, 0)) + [
            pl.BlockSpec((2, LR_COLS, HB * GLA_DK), lambda b, h, n: (0, 0, h)),
            pl.BlockSpec((2, 1, HB * GLA_DK), lambda b, h, n: (0, 0, h)),
        ],
        out_specs=[
            pl.BlockSpec((1, TS, HB * GLA_DV), lambda b, h, n: (b, n, h)),
            pl.BlockSpec((1, TS, HB * GLA_DV), lambda b, h, n: (b, NB - 1 - n, h)),
        ],
        out_shape=[jax.ShapeDtypeStruct((B, S, GLA_VAL), BF16)] * 2,
        scratch_shapes=[
            pltpu.VMEM((HB, GLA_DV, GLA_DK), F32),
            pltpu.VMEM((HB, GLA_DV, GLA_DK), F32),
        ],
        compiler_params=pltpu.CompilerParams(
            dimension_semantics=("parallel", "parallel", "arbitrary"),
            vmem_limit_bytes=V7X_VMEM_LIMIT),
        name="gla",
    )(p3, p3, p3, lr3, p3, p3, p3, lr3, wg_pad, bg)


def _sigmoid(x):
    return 1.0 / (1.0 + jnp.exp(-x))


def _merge_kernel(o0_ref, o1_ref, o2_ref, l0_ref, l1_ref, l2_ref, of_ref, ob_ref, rg_ref, ga_ref, gb_ref,
                  x_ref, wba_ref, wbg_ref, wout_ref, gnorm_ref, nffn_ref, h_ref, hn_ref,
                  oa_s, og_s, oil_s, lil_s):
    tm = x_ref.shape[0]
    for gi, (o_ref, l_ref) in enumerate(((o1_ref, l1_ref), (o2_ref, l2_ref))):
        r = o_ref.shape[1]
        for c in range(r):
            dst = pl.ds(c, tm // r, stride=r)
            lil_s[gi, dst, :] = l_ref[0, c]
            for hh in range(HEADS_PER_GROUP):
                oil_s[gi, hh, dst, :] = o_ref[0, c, :, hh * HEAD_DIM:(hh + 1) * HEAD_DIM].astype(F32)

    l0, l1, l2 = l0_ref[0, 0], lil_s[0], lil_s[1]
    m = jnp.maximum(jnp.maximum(l0, l1), l2)
    e0, e1, e2 = jnp.exp(l0 - m), jnp.exp(l1 - m), jnp.exp(l2 - m)
    inv = 1.0 / (e0 + e1 + e2)
    w0, w1, w2 = e0 * inv, e1 * inv, e2 * inv
    for hh in range(HEADS_PER_GROUP):
        hs = slice(hh * HEAD_DIM, (hh + 1) * HEAD_DIM)
        comb = (w0[:, hh:hh + 1] * o0_ref[0, 0, :, hs].astype(F32)
                + w1[:, hh:hh + 1] * oil_s[0, hh]
                + w2[:, hh:hh + 1] * oil_s[1, hh])
        oa_s[:, hs] = comb.astype(BF16)
    u_a = _dot(oa_s[...], wba_ref[...])

    for h in range(GLA_HEADS):
        vs = slice(h * GLA_DV, (h + 1) * GLA_DV)
        og = of_ref[:, vs].astype(F32) + ob_ref[:, vs].astype(F32)
        ms = jnp.mean(og * og, axis=-1, keepdims=True)
        ogn = og * lax.rsqrt(ms + EPS) * gnorm_ref[...]
        rg = rg_ref[:, vs].astype(F32)
        og_s[:, vs] = (ogn * (rg * _sigmoid(rg))).astype(BF16)
    u_b = _dot(og_s[...], wbg_ref[...])

    merged = _sigmoid(ga_ref[...].astype(F32)) * u_a + _sigmoid(gb_ref[...].astype(F32)) * u_b
    h = x_ref[...] + _dot(merged.astype(BF16), wout_ref[...])
    h_ref[...] = h
    ms = jnp.mean(h * h, axis=-1, keepdims=True)
    hn_ref[...] = (h * lax.rsqrt(ms + EPS) * nffn_ref[...]).astype(BF16)


def _merge(o_groups, lse_groups, o_fwd, o_bwd, p1, p2, x, wba, wbg, wout, gnorm, nffn, B, S, *, tm=256):
    t = x.shape[0]
    per_b = S // tm
    row = lambda b, i: (b * per_b + i, 0)
    const = lambda b, i: (0, 0)

    def resident(shape):
        return pl.BlockSpec(shape, const, pipeline_mode=pl.Buffered(1))

    def split_spec(r, width):
        return pl.BlockSpec((1, r, tm // r, width), lambda b, i: (b, 0, i, 0))

    rs = [r for _, r in ATTN_GROUPS]
    return pl.pallas_call(
        _merge_kernel,
        grid=(B, per_b),
        in_specs=[split_spec(r, ATTN_OUT) for r in rs] + [split_spec(r, HEAD_DIM) for r in rs] + [
            pl.BlockSpec((tm, GLA_VAL), row),
            pl.BlockSpec((tm, GLA_VAL), row),
            pl.BlockSpec((tm, GLA_VAL), lambda b, i: (b * per_b + i, COL_RG // GLA_VAL)),
            pl.BlockSpec((tm, D_MODEL), lambda b, i: (b * per_b + i, COL_GA // D_MODEL)),
            pl.BlockSpec((tm, D_MODEL), lambda b, i: (b * per_b + i, COL_GB // D_MODEL)),
            pl.BlockSpec((tm, D_MODEL), row),
            resident((ATTN_OUT, D_MODEL)),
            resident((GLA_VAL, D_MODEL)),
            resident((D_MODEL, D_MODEL)),
            resident((1, GLA_DV)),
            resident((1, D_MODEL)),
        ],
        out_specs=[pl.BlockSpec((tm, D_MODEL), row), pl.BlockSpec((tm, D_MODEL), row)],
        out_shape=[jax.ShapeDtypeStruct((t, D_MODEL), F32), jax.ShapeDtypeStruct((t, D_MODEL), BF16)],
        scratch_shapes=[
            pltpu.VMEM((tm, ATTN_OUT), BF16),
            pltpu.VMEM((tm, GLA_VAL), BF16),
            pltpu.VMEM((2, HEADS_PER_GROUP, tm, HEAD_DIM), F32),
            pltpu.VMEM((2, tm, HEAD_DIM), F32),
        ],
        compiler_params=pltpu.CompilerParams(
            dimension_semantics=("parallel", "parallel"),
            vmem_limit_bytes=V7X_VMEM_LIMIT),
        name="merge",
    )(*o_groups, *lse_groups, o_fwd, o_bwd, p1, p2, p2, x, wba, wbg, wout, gnorm, nffn)


def _ffn_kernel(hn_ref, w1_ref, w2_ref, h_ref, y_ref):
    j = pl.program_id(1)
    a = _dot(hn_ref[...], w1_ref[...])
    a = jnp.square(jnp.maximum(a, 0.0)).astype(BF16)
    y = _dot(a, w2_ref[...])

    @pl.when(j == 0)
    def _():
        y_ref[...] = h_ref[...] + y

    @pl.when(j > 0)
    def _():
        y_ref[...] += y


def _ffn(hn, h, w1, w2, *, tm=512, tf=1024):
    t = hn.shape[0]
    return pl.pallas_call(
        _ffn_kernel,
        grid=(t // tm, D_FF // tf),
        in_specs=[
            pl.BlockSpec((tm, D_MODEL), lambda i, j: (i, 0)),
            pl.BlockSpec((D_MODEL, tf), lambda i, j: (0, j)),
            pl.BlockSpec((tf, D_MODEL), lambda i, j: (j, 0)),
            pl.BlockSpec((tm, D_MODEL), lambda i, j: (i, 0)),
        ],
        out_specs=pl.BlockSpec((tm, D_MODEL), lambda i, j: (i, 0)),
        out_shape=jax.ShapeDtypeStruct((t, D_MODEL), F32),
        compiler_params=pltpu.CompilerParams(
            dimension_semantics=("parallel", "arbitrary"),
            vmem_limit_bytes=V7X_VMEM_LIMIT),
        name="ffn",
    )(hn, w1, w2, h)


def _rope_tables(s_max):
    inv_freq = ROPE_THETA ** (-jnp.arange(0, ROT_DIM, 2, dtype=F32) / ROT_DIM)
    ang = jnp.arange(s_max, dtype=F32)[:, None] * inv_freq[None, :]
    cos, sin = jnp.cos(ang), jnp.sin(ang)
    rest = HEAD_DIM - ROT_DIM
    cos_t = jnp.concatenate([cos, cos, jnp.ones((s_max, rest), F32)], axis=1)
    sin_t = jnp.concatenate([sin, sin, jnp.zeros((s_max, rest), F32)], axis=1)
    return cos_t, sin_t


def _prepare_layer(w_in, w_gla_gate, b_gla_gate):
    w_gates = w_in[:, W_COL_GATES:].astype(BF16)
    w_lr = jnp.pad(w_in[:, W_COL_LR:W_COL_GATES], ((0, 0), (0, LR_COLS - 2 * GLA_RANK))).astype(BF16)
    wg_pad = jnp.zeros((2, LR_COLS, GLA_KEY), F32)
    wg_pad = wg_pad.at[0, 0:GLA_RANK].set(w_gla_gate[0].astype(F32))
    wg_pad = wg_pad.at[1, GLA_RANK:2 * GLA_RANK].set(w_gla_gate[1].astype(F32)).astype(BF16)
    bg = b_gla_gate.astype(F32).reshape(2, 1, GLA_KEY)
    return w_in, w_gates, w_lr, wg_pad, bg


def _layer(x3, tables, norm_mix, prepared, q_norm, k_norm, gla_norm, wba, wbg, wout, norm_ffn, w1, w2):
    B, S, _ = x3.shape
    T = B * S
    w_in, w_gates, w_lr, wg_pad, bg = prepared
    tn = 1024
    x = x3.reshape(T, D_MODEL)
    xn, lr = _xnorm(x, norm_mix.reshape(1, D_MODEL), w_lr)
    o_groups, lse_groups = [], []
    for g in range(N_GROUPS):
        r = ATTN_GROUPS[g][1]
        a_g = _proj(xn, w_in, lambda j, g=g: N_GROUPS * j + g, 3, r, B, S, f"proj_attn_r{r}", tn=tn)
        o_g, lse_g = _attention_group(a_g.reshape(B, r, S // r, 3 * ATTN_OUT), tables, q_norm, k_norm, g, B, S)
        o_groups.append(o_g)
        lse_groups.append(lse_g)
    p1 = _proj(xn, w_in, lambda j: W_COL_QG // tn + j, P1_COLS // tn, 1, B, S, "proj_gla", tn=tn)
    p2 = _proj(xn, w_gates, lambda j: j, 2 * D_MODEL // tn, 1, B, S, "proj_gates", tn=tn)
    o_fwd, o_bwd = _gla(p1.reshape(B, S, P1_COLS), lr.reshape(B, S, LR_COLS), wg_pad, bg, B, S)
    h, hn = _merge(o_groups, lse_groups, o_fwd.reshape(T, GLA_VAL), o_bwd.reshape(T, GLA_VAL), p1, p2, x,
                   wba, wbg, wout, gla_norm.reshape(1, GLA_DV), norm_ffn.reshape(1, D_MODEL), B, S)
    y = _ffn(hn, h, w1, w2)
    return y.reshape(B, S, D_MODEL)


def kernel(x_prompt, x_sample, norm_mix, w_in, q_norm, k_norm, w_gla_gate, b_gla_gate, gla_norm,
           w_branch_attn, w_branch_gla, w_out, norm_ffn, w_ff1, w_ff2):
    depth = w_in.shape[0]
    tables = _rope_tables(max(x_prompt.shape[1], x_sample.shape[1]))
    layers = []
    for l in range(depth):
        layers.append((
            norm_mix[l], _prepare_layer(w_in[l], w_gla_gate[l], b_gla_gate[l]),
            q_norm[l].astype(F32), k_norm[l].astype(F32), gla_norm[l].astype(F32),
            w_branch_attn[l].astype(BF16), w_branch_gla[l].astype(BF16), w_out[l].astype(BF16),
            norm_ffn[l].astype(F32), w_ff1[l].astype(BF16), w_ff2[l].astype(BF16)))
    outs = []
    for x in (x_prompt, x_sample):
        for layer in layers:
            x = _layer(x, tables, *layer)
        outs.append(x)
    return tuple(outs)
```

```python
import functools

import jax
import jax.numpy as jnp
from jax import lax
from jax.experimental import pallas as pl
from jax.experimental.pallas import tpu as pltpu

F32 = jnp.float32
BF16 = jnp.bfloat16

D_MODEL = 2048
HEAD_DIM = 128
ATTN_GROUPS = ((128, 1), (512, 4), (2048, 16))
N_GROUPS = 3
HEADS_PER_GROUP = 8
ATTN_QKV = N_GROUPS * HEADS_PER_GROUP * HEAD_DIM
ATTN_OUT = HEADS_PER_GROUP * HEAD_DIM
ROT_DIM = HEAD_DIM // 4
ROPE_THETA = 500000.0
GLA_HEADS = 4
GLA_KEY = 1024
GLA_VAL = 2048
GLA_DK = 256
GLA_DV = 512
GLA_RANK = 16
GLA_NORMALIZER = 16.0
D_FF = 4 * D_MODEL
EPS = 1e-6

W_COL_QG = 3 * ATTN_QKV
W_COL_LR = W_COL_QG + 2 * GLA_KEY + 2 * GLA_VAL
W_COL_GATES = W_COL_LR + 2 * GLA_RANK
COL_QG = 0
COL_KG = 1024
COL_VG = 2048
COL_RG = 4096
P1_COLS = 6144
COL_GA = 0
COL_GB = 2048
LR_COLS = 128
PROJ_CHUNK = 256

V7X_VMEM_LIMIT = 56 * 1024 * 1024
NEG = -1e30

ATTN_HALF = 64
ATTN_TQ = 128
ATTN_CHAINS = 8
GLA_BLOCK = 256
GLA_SUB = 64
GLA_HB = 4


def _dot(a, b):
    return jnp.dot(a, b, preferred_element_type=F32)


def _dot_nt(a, b):
    return lax.dot_general(a, b, (((1,), (1,)), ((), ())), preferred_element_type=F32)


def _dot_tn(a, b):
    return lax.dot_general(a, b, (((0,), (0,)), ((), ())), preferred_element_type=F32)


def _split_bf16(x):
    hi = x.astype(BF16)
    lo = (x - hi.astype(F32)).astype(BF16)
    return hi, lo


def _xnorm_kernel(x_ref, gain_ref, wlr_ref, xn_ref, lr_ref):
    x = x_ref[...]
    ms = jnp.mean(x * x, axis=-1, keepdims=True)
    xn = (x * lax.rsqrt(ms + EPS) * gain_ref[...]).astype(BF16)
    xn_ref[...] = xn
    lr_ref[...] = _dot_nt(xn, wlr_ref[...])


def _xnorm(x, gain, w_lr, *, tm=512):
    t = x.shape[0]
    return pl.pallas_call(
        _xnorm_kernel,
        grid=(t // tm,),
        in_specs=[
            pl.BlockSpec((tm, D_MODEL), lambda i: (i, 0)),
            pl.BlockSpec((1, D_MODEL), lambda i: (0, 0)),
            pl.BlockSpec((LR_COLS, D_MODEL), lambda i: (0, 0)),
        ],
        out_specs=[
            pl.BlockSpec((tm, D_MODEL), lambda i: (i, 0)),
            pl.BlockSpec((tm, LR_COLS), lambda i: (i, 0)),
        ],
        out_shape=[
            jax.ShapeDtypeStruct((t, D_MODEL), BF16),
            jax.ShapeDtypeStruct((t, LR_COLS), F32),
        ],
        compiler_params=pltpu.CompilerParams(
            dimension_semantics=("parallel",),
            vmem_limit_bytes=V7X_VMEM_LIMIT),
        name="xnorm",
    )(x, gain, w_lr)


def _proj_kernel(xn_ref, w_ref, o_ref, wb_s, *scratch, r):
    @pl.when(pl.program_id(1) == 0)
    def _():
        wb_s[...] = w_ref[...].astype(BF16)

    xn = xn_ref[...]
    tm, tn = xn.shape[0], wb_s.shape[0]

    def store(cb, acc):
        if r == 1:
            o_ref[:, cb * PROJ_CHUNK:(cb + 1) * PROJ_CHUNK] = acc.astype(BF16)
            return
        acc_s, = scratch
        for kb in range(PROJ_CHUNK // HEAD_DIM):
            lb = cb * (PROJ_CHUNK // HEAD_DIM) + kb
            acc_s[lb] = acc[:, kb * HEAD_DIM:(kb + 1) * HEAD_DIM]
            for c in range(r):
                o_ref[0, c, :, lb * HEAD_DIM:(lb + 1) * HEAD_DIM] = (
                    acc_s[lb, pl.ds(c, tm // r, stride=r), :].astype(BF16))

    pending = None
    for cb in range(tn // PROJ_CHUNK):
        acc = _dot_nt(xn, wb_s[cb * PROJ_CHUNK:(cb + 1) * PROJ_CHUNK, :])
        if pending is not None:
            store(*pending)
        pending = (cb, acc)
    store(*pending)


def _proj(xn, w_t, col_block, ncol_blocks, r, B, S, name, *, tm=1024, tn=1024):
    t = xn.shape[0]
    per_b = S // tm
    if r == 1:
        out_spec = pl.BlockSpec((tm, tn), lambda j, i: (i, j))
        out_shape = jax.ShapeDtypeStruct((t, ncol_blocks * tn), BF16)
        scratch = []
    else:
        out_spec = pl.BlockSpec((1, r, tm // r, tn), lambda j, i: (i // per_b, 0, i % per_b, j))
        out_shape = jax.ShapeDtypeStruct((B, r, S // r, ncol_blocks * tn), BF16)
        scratch = [pltpu.VMEM((tn // HEAD_DIM, tm, HEAD_DIM), F32)]
    return pl.pallas_call(
        functools.partial(_proj_kernel, r=r),
        grid=(ncol_blocks, t // tm),
        in_specs=[
            pl.BlockSpec((tm, D_MODEL), lambda j, i: (i, 0)),
            pl.BlockSpec((tn, D_MODEL), lambda j, i: (col_block(j), 0)),
        ],
        out_specs=out_spec,
        out_shape=out_shape,
        scratch_shapes=[pltpu.VMEM((tn, D_MODEL), BF16)] + scratch,
        compiler_params=pltpu.CompilerParams(
            dimension_semantics=("parallel", "arbitrary"),
            vmem_limit_bytes=V7X_VMEM_LIMIT),
        name=name,
    )(xn, w_t)


def _attn_kernel(q_ref, k_ref, v_ref, cos_ref, sin_ref, qg_ref, kg_ref,
                 o_ref, lse_ref, qs, ks, vs, bias_s, *, L, hb, U):
    TQ, HALF = ATTN_TQ, ATTN_HALF
    TK = TQ + 2 * HALF
    NT = L // TQ
    TR = min(L, 256)
    hblk = pl.program_id(2)

    @pl.when(hblk == 0)
    def _():
        lse_ref[...] = jnp.zeros_like(lse_ref)

    ri = lax.broadcasted_iota(jnp.int32, (TQ, TK), 0)
    ci = lax.broadcasted_iota(jnp.int32, (TQ, TK), 1)
    d = ci - ri
    band = jnp.where(d < 0, NEG, jnp.where(d > 2 * HALF, NEG, 0.0)).astype(F32)
    first = jnp.where(ci < HALF, NEG, band)
    bias_s[0] = band
    bias_s[1] = first
    bias_s[2] = jnp.where(ci >= TQ + HALF, NEG, band)
    bias_s[3] = jnp.where(ci >= TQ + HALF, NEG, first)

    a = lax.broadcasted_iota(jnp.int32, (HEAD_DIM, HEAD_DIM), 0)
    b = lax.broadcasted_iota(jnp.int32, (HEAD_DIM, HEAD_DIM), 1)
    half = ROT_DIM // 2
    ones_m = jnp.ones((HEAD_DIM, HEAD_DIM), BF16)
    rot_m = jnp.where((b < half) & (a == b + half), -1.0,
                      jnp.where((b >= half) & (b < ROT_DIM) & (a == b - half), 1.0, 0.0)).astype(BF16)

    zpad = jnp.zeros((HALF, HEAD_DIM), BF16)
    for hh in range(hb):
        ks[hh, 0:HALF, :] = zpad
        ks[hh, L + HALF:L + 2 * HALF, :] = zpad
        vs[hh, 0:HALF, 0:HEAD_DIM] = zpad
        vs[hh, L + HALF:L + 2 * HALF, 0:HEAD_DIM] = zpad
        vs[hh, :, HEAD_DIM:2 * HEAD_DIM] = jnp.ones((L + 2 * HALF, HEAD_DIM), BF16)

    qgain = qg_ref[...] * (HEAD_DIM ** -0.5)
    kgain = kg_ref[...]

    def norm_rope(x, gain, cos, sin):
        ssq = _dot((x * x).astype(BF16), ones_m)
        xn = x * lax.rsqrt(ssq * (1.0 / HEAD_DIM) + EPS) * gain
        return xn * cos + _dot(xn.astype(BF16), rot_m) * sin

    def prep(t, carry):
        r0 = pl.multiple_of(t * TR, TR)
        rows = pl.ds(r0, TR)
        prow = pl.ds(pl.multiple_of(r0 + HALF, HALF), TR)
        cos, sin = cos_ref[rows, :], sin_ref[rows, :]
        for hh in range(hb):
            hs = slice(hh * HEAD_DIM, (hh + 1) * HEAD_DIM)
            qs[hh, rows, :] = norm_rope(q_ref[0, 0, rows, hs].astype(F32), qgain, cos, sin).astype(BF16)
            ks[hh, prow, :] = norm_rope(k_ref[0, 0, rows, hs].astype(F32), kgain, cos, sin).astype(BF16)
            vs[hh, prow, 0:HEAD_DIM] = v_ref[0, 0, rows, hs]
        return carry

    lax.fori_loop(0, L // TR, prep, 0)

    lane = lax.broadcasted_iota(jnp.int32, (TQ, HEAD_DIM), 1)

    def tiles(tt, carry):
        for u in range(U):
            t = tt * U + u
            q0 = pl.multiple_of(t * TQ, TQ)
            qrows = pl.ds(q0, TQ)
            krows = pl.ds(q0, TK)
            bias = bias_s[jnp.where(t == 0, 1, 0) + jnp.where(t == NT - 1, 2, 0)]
            lse_tile = lse_ref[0, 0, qrows, :]
            for hh in range(hb):
                hs = slice(hh * HEAD_DIM, (hh + 1) * HEAD_DIM)
                s = _dot_nt(qs[hh, qrows, :], ks[hh, krows, :]) + bias
                m = jnp.max(s, axis=-1, keepdims=True)
                p = jnp.exp(s - m).astype(BF16)
                acc = _dot(p, vs[hh, krows, :])
                den = acc[:, HEAD_DIM:]
                o_ref[0, 0, qrows, hs] = (acc[:, :HEAD_DIM] * (1.0 / den)).astype(BF16)
                lse_tile = jnp.where(lane == hblk * hb + hh, m + jnp.log(den), lse_tile)
            lse_ref[0, 0, qrows, :] = lse_tile
        return carry

    lax.fori_loop(0, NT // U, tiles, 0)


def _attention_group(a_g, tables, q_gain, k_gain, g, B, S):
    window, r = ATTN_GROUPS[g]
    assert window // (2 * r) == ATTN_HALF
    L = S // r
    assert L % ATTN_TQ == 0
    hb = max(1, min(HEADS_PER_GROUP, 8192 // L))
    nhb = HEADS_PER_GROUP // hb
    bw = hb * HEAD_DIM
    U = max(1, min(ATTN_CHAINS // hb, L // ATTN_TQ))
    assert (L // ATTN_TQ) % U == 0
    cos, sin = (t[:S].reshape(L, r * HEAD_DIM) for t in tables)

    def col_map(part):
        return lambda b, c, h: (b, c, 0, part * nhb + h)

    tab_spec = pl.BlockSpec((L, HEAD_DIM), lambda b, c, h: (0, c))
    gain_spec = pl.BlockSpec((1, HEAD_DIM), lambda b, c, h: (0, 0))
    return pl.pallas_call(
        functools.partial(_attn_kernel, L=L, hb=hb, U=U),
        grid=(B, r, nhb),
        in_specs=[
            pl.BlockSpec((1, 1, L, bw), col_map(0)),
            pl.BlockSpec((1, 1, L, bw), col_map(1)),
            pl.BlockSpec((1, 1, L, bw), col_map(2)),
            tab_spec, tab_spec, gain_spec, gain_spec,
        ],
        out_specs=[
            pl.BlockSpec((1, 1, L, bw), lambda b, c, h: (b, c, 0, h)),
            pl.BlockSpec((1, 1, L, HEAD_DIM), lambda b, c, h: (b, c, 0, 0)),
        ],
        out_shape=[
            jax.ShapeDtypeStruct((B, r, L, ATTN_OUT), BF16),
            jax.ShapeDtypeStruct((B, r, L, HEAD_DIM), F32),
        ],
        scratch_shapes=[
            pltpu.VMEM((hb, L, HEAD_DIM), BF16),
            pltpu.VMEM((hb, L + 2 * ATTN_HALF, HEAD_DIM), BF16),
            pltpu.VMEM((hb, L + 2 * ATTN_HALF, 2 * HEAD_DIM), BF16),
            pltpu.VMEM((4, ATTN_TQ, ATTN_TQ + 2 * ATTN_HALF), F32),
        ],
        compiler_params=pltpu.CompilerParams(
            dimension_semantics=("parallel", "parallel", "arbitrary"),
            vmem_limit_bytes=V7X_VMEM_LIMIT),
        name=f"attn_g{g}",
    )(a_g, a_g, a_g, cos, sin, q_gain[g:g + 1], k_gain[g:g + 1])


def _gla_kernel(qf_ref, kf_ref, vf_ref, lrf_ref, qb_ref, kb_ref, vb_ref, lrb_ref, wg_ref, bg_ref,
                of_ref, ob_ref, stf, stb):
    C, SC = GLA_BLOCK, GLA_SUB
    NS = C // SC
    n = pl.program_id(2)

    @pl.when(n == 0)
    def _():
        stf[...] = jnp.zeros_like(stf)
        stb[...] = jnp.zeros_like(stb)

    ri = lax.broadcasted_iota(jnp.int32, (C, C), 0)
    ci = lax.broadcasted_iota(jnp.int32, (C, C), 1)

    def direction(q_ref, k_ref, v_ref, lr_ref, o_ref, st, d, backward, h):
        ks = slice(h * GLA_DK, (h + 1) * GLA_DK)
        vs = slice(h * GLA_DV, (h + 1) * GLA_DV)
        z = _dot(lr_ref[0].astype(BF16), wg_ref[d, :, ks]) + bg_ref[d, :, ks]
        yield
        softplus = jnp.maximum(-z, 0.0) + jnp.log(1.0 + jnp.exp(-jnp.abs(z)))
        g_hi, g_lo = _split_bf16(softplus * (-1.0 / GLA_NORMALIZER))
        tri = jnp.where((ci >= ri) if backward else (ri >= ci), 1.0, 0.0).astype(BF16)
        cum = _dot(tri, g_hi) + _dot(tri, g_lo)
        yield
        mid_row = SC // 2 if backward else SC // 2 - 1
        end_row = 0 if backward else C - 1
        mids = [cum[I * SC + mid_row:I * SC + mid_row + 1, :] for I in range(NS)]
        end = cum[end_row:end_row + 1, :]
        sub = [slice(I * SC, (I + 1) * SC) for I in range(NS)]
        dl = jnp.concatenate([cum[sub[I], :] - mids[I] for I in range(NS)], axis=0)
        qd = q_ref[0, :, ks].astype(F32) * (GLA_DK ** -0.5) * jnp.exp(dl)
        kd = k_ref[0, :, ks].astype(F32) * jnp.exp(-dl)
        qd_b = qd.astype(BF16)
        kd_b = kd.astype(BF16)
        att_rows = []
        for I in range(NS):
            blocks = []
            for J in range(NS):
                if (J > I) if backward else (J < I):
                    blocks.append((kd[sub[J], :] * jnp.exp(mids[I] - mids[J])).astype(BF16))
                else:
                    blocks.append(kd_b[sub[J], :])
            att_rows.append(_dot_nt(qd_b[sub[I], :], jnp.concatenate(blocks, axis=0)))
        yield
        mask = (ci > ri) if backward else (ri >= ci)
        att = jnp.where(mask, jnp.concatenate(att_rows, axis=0), 0.0).astype(BF16)
        qi = jnp.concatenate([qd[sub[I], :] * jnp.exp(mids[I]) for I in range(NS)], axis=0).astype(BF16)
        k2 = jnp.concatenate([kd[sub[I], :] * jnp.exp(end - mids[I]) for I in range(NS)], axis=0).astype(BF16)
        v = v_ref[0, :, vs]
        state = st[h]
        o = _dot(att, v) + _dot_nt(qi, state.astype(BF16))
        upd = _dot_tn(v, k2)
        yield
        o_ref[0, :, vs] = o.astype(BF16)
        st[h] = state * jnp.exp(end) + upd

    chains = []
    for h in range(GLA_HB):
        chains.append(direction(qf_ref, kf_ref, vf_ref, lrf_ref, of_ref, stf, 0, False, h))
        chains.append(direction(qb_ref, kb_ref, vb_ref, lrb_ref, ob_ref, stb, 1, True, h))
    while chains:
        alive = []
        for chain in chains:
            if next(chain, chain) is not chain:
                alive.append(chain)
        chains = alive


def _gla(p3, lr3, wg_pad, bg, B, S):
    TS = GLA_BLOCK
    NB = S // TS
    HB = GLA_HB
    kq, kk, kv = COL_QG // (HB * GLA_DK), COL_KG // (HB * GLA_DK), COL_VG // (HB * GLA_DV)

    def fwd(base):
        return lambda b, h, n: (b, n, base + h)

    def bwd(base):
        return lambda b, h, n: (b, NB - 1 - n, base + h)

    def specs(m, lr_map):
        return [
            pl.BlockSpec((1, TS, HB * GLA_DK), m(kq)),
            pl.BlockSpec((1, TS, HB * GLA_DK), m(kk)),
            pl.BlockSpec((1, TS, HB * GLA_DV), m(kv)),
            pl.BlockSpec((1, TS, LR_COLS), lr_map),
        ]

    return pl.pallas_call(
        _gla_kernel,
        grid=(B, GLA_HEADS // HB, NB),
        in_specs=specs(fwd, lambda b, h, n: (b, n, 0)) + specs(bwd, lambda b, h, n: (b, NB - 1 - n, 0)) + [
            pl.BlockSpec((2, LR_COLS, HB * GLA_DK), lambda b, h, n: (0, 0, h)),
            pl.BlockSpec((2, 1, HB * GLA_DK), lambda b, h, n: (0, 0, h)),
        ],
        out_specs=[
            pl.BlockSpec((1, TS, HB * GLA_DV), lambda b, h, n: (b, n, h)),
            pl.BlockSpec((1, TS, HB * GLA_DV), lambda b, h, n: (b, NB - 1 - n, h)),
        ],
        out_shape=[jax.ShapeDtypeStruct((B, S, GLA_VAL), BF16)] * 2,
        scratch_shapes=[
            pltpu.VMEM((HB, GLA_DV, GLA_DK), F32),
            pltpu.VMEM((HB, GLA_DV, GLA_DK), F32),
        ],
        compiler_params=pltpu.CompilerParams(
            dimension_semantics=("parallel", "parallel", "arbitrary"),
            vmem_limit_bytes=V7X_VMEM_LIMIT),
        name="gla",
    )(p3, p3, p3, lr3, p3, p3, p3, lr3, wg_pad, bg)


def _sigmoid(x):
    return 1.0 / (1.0 + jnp.exp(-x))


def _merge_kernel(o0_ref, o1_ref, o2_ref, l0_ref, l1_ref, l2_ref, of_ref, ob_ref, rg_ref, ga_ref, gb_ref,
                  x_ref, wba_ref, wbg_ref, wout_ref, gnorm_ref, nffn_ref, h_ref, hn_ref,
                  oa_s, og_s, oil_s, lil_s):
    tm = x_ref.shape[0]
    for gi, (o_ref, l_ref) in enumerate(((o1_ref, l1_ref), (o2_ref, l2_ref))):
        r = o_ref.shape[1]
        for c in range(r):
            dst = pl.ds(c, tm // r, stride=r)
            lil_s[gi, dst, :] = l_ref[0, c]
            for hh in range(HEADS_PER_GROUP):
                oil_s[gi, hh, dst, :] = o_ref[0, c, :, hh * HEAD_DIM:(hh + 1) * HEAD_DIM].astype(F32)

    l0, l1, l2 = l0_ref[0, 0], lil_s[0], lil_s[1]
    m = jnp.maximum(jnp.maximum(l0, l1), l2)
    e0, e1, e2 = jnp.exp(l0 - m), jnp.exp(l1 - m), jnp.exp(l2 - m)
    inv = 1.0 / (e0 + e1 + e2)
    w0, w1, w2 = e0 * inv, e1 * inv, e2 * inv
    for hh in range(HEADS_PER_GROUP):
        hs = slice(hh * HEAD_DIM, (hh + 1) * HEAD_DIM)
        comb = (w0[:, hh:hh + 1] * o0_ref[0, 0, :, hs].astype(F32)
                + w1[:, hh:hh + 1] * oil_s[0, hh]
                + w2[:, hh:hh + 1] * oil_s[1, hh])
        oa_s[:, hs] = comb.astype(BF16)
    u_a = _dot(oa_s[...], wba_ref[...])

    for h in range(GLA_HEADS):
        vs = slice(h * GLA_DV, (h + 1) * GLA_DV)
        og = of_ref[:, vs].astype(F32) + ob_ref[:, vs].astype(F32)
        ms = jnp.mean(og * og, axis=-1, keepdims=True)
        ogn = og * lax.rsqrt(ms + EPS) * gnorm_ref[...]
        rg = rg_ref[:, vs].astype(F32)
        og_s[:, vs] = (ogn * (rg * _sigmoid(rg))).astype(BF16)
    u_b = _dot(og_s[...], wbg_ref[...])

    merged = _sigmoid(ga_ref[...].astype(F32)) * u_a + _sigmoid(gb_ref[...].astype(F32)) * u_b
    h = x_ref[...] + _dot(merged.astype(BF16), wout_ref[...])
    h_ref[...] = h
    ms = jnp.mean(h * h, axis=-1, keepdims=True)
    hn_ref[...] = (h * lax.rsqrt(ms + EPS) * nffn_ref[...]).astype(BF16)


def _merge(o_groups, lse_groups, o_fwd, o_bwd, p1, p2, x, wba, wbg, wout, gnorm, nffn, B, S, *, tm=256):
    t = x.shape[0]
    per_b = S // tm
    row = lambda b, i: (b * per_b + i, 0)
    const = lambda b, i: (0, 0)

    def resident(shape):
        return pl.BlockSpec(shape, const, pipeline_mode=pl.Buffered(1))

    def split_spec(r, width):
        return pl.BlockSpec((1, r, tm // r, width), lambda b, i: (b, 0, i, 0))

    rs = [r for _, r in ATTN_GROUPS]
    return pl.pallas_call(
        _merge_kernel,
        grid=(B, per_b),
        in_specs=[split_spec(r, ATTN_OUT) for r in rs] + [split_spec(r, HEAD_DIM) for r in rs] + [
            pl.BlockSpec((tm, GLA_VAL), row),
            pl.BlockSpec((tm, GLA_VAL), row),
            pl.BlockSpec((tm, GLA_VAL), lambda b, i: (b * per_b + i, COL_RG // GLA_VAL)),
            pl.BlockSpec((tm, D_MODEL), lambda b, i: (b * per_b + i, COL_GA // D_MODEL)),
            pl.BlockSpec((tm, D_MODEL), lambda b, i: (b * per_b + i, COL_GB // D_MODEL)),
            pl.BlockSpec((tm, D_MODEL), row),
            resident((ATTN_OUT, D_MODEL)),
            resident((GLA_VAL, D_MODEL)),
            resident((D_MODEL, D_MODEL)),
            resident((1, GLA_DV)),
            resident((1, D_MODEL)),
        ],
        out_specs=[pl.BlockSpec((tm, D_MODEL), row), pl.BlockSpec((tm, D_MODEL), row)],
        out_shape=[jax.ShapeDtypeStruct((t, D_MODEL), F32), jax.ShapeDtypeStruct((t, D_MODEL), BF16)],
        scratch_shapes=[
            pltpu.VMEM((tm, ATTN_OUT), BF16),
            pltpu.VMEM((tm, GLA_VAL), BF16),
            pltpu.VMEM((2, HEADS_PER_GROUP, tm, HEAD_DIM), F32),
            pltpu.VMEM((2, tm, HEAD_DIM), F32),
        ],
        compiler_params=pltpu.CompilerParams(
            dimension_semantics=("parallel", "parallel"),
            vmem_limit_bytes=V7X_VMEM_LIMIT),
        name="merge",
    )(*o_groups, *lse_groups, o_fwd, o_bwd, p1, p2, p2, x, wba, wbg, wout, gnorm, nffn)


def _ffn_kernel(hn_ref, w1_ref, w2_ref, h_ref, y_ref):
    j = pl.program_id(1)
    a = _dot(hn_ref[...], w1_ref[...])
    a = jnp.square(jnp.maximum(a, 0.0)).astype(BF16)
    y = _dot(a, w2_ref[...])

    @pl.when(j == 0)
    def _():
        y_ref[...] = h_ref[...] + y

    @pl.when(j > 0)
    def _():
        y_ref[...] += y


def _ffn(hn, h, w1, w2, *, tm=512, tf=1024):
    t = hn.shape[0]
    return pl.pallas_call(
        _ffn_kernel,
        grid=(t // tm, D_FF // tf),
        in_specs=[
            pl.BlockSpec((tm, D_MODEL), lambda i, j: (i, 0)),
            pl.BlockSpec((D_MODEL, tf), lambda i, j: (0, j)),
            pl.BlockSpec((tf, D_MODEL), lambda i, j: (j, 0)),
            pl.BlockSpec((tm, D_MODEL), lambda i, j: (i, 0)),
        ],
        out_specs=pl.BlockSpec((tm, D_MODEL), lambda i, j: (i, 0)),
        out_shape=jax.ShapeDtypeStruct((t, D_MODEL), F32),
        compiler_params=pltpu.CompilerParams(
            dimension_semantics=("parallel", "arbitrary"),
            vmem_limit_bytes=V7X_VMEM_LIMIT),
        name="ffn",
    )(hn, w1, w2, h)


def _rope_tables(s_max):
    inv_freq = ROPE_THETA ** (-jnp.arange(0, ROT_DIM, 2, dtype=F32) / ROT_DIM)
    ang = jnp.arange(s_max, dtype=F32)[:, None] * inv_freq[None, :]
    cos, sin = jnp.cos(ang), jnp.sin(ang)
    rest = HEAD_DIM - ROT_DIM
    cos_t = jnp.concatenate([cos, cos, jnp.ones((s_max, rest), F32)], axis=1)
    sin_t = jnp.concatenate([sin, sin, jnp.zeros((s_max, rest), F32)], axis=1)
    return cos_t, sin_t


def _prepare_layer(w_in, w_gla_gate, b_gla_gate):
    w_t = jnp.swapaxes(w_in, 0, 1)
    tail = w_t[W_COL_LR:].astype(BF16)
    w_gates = tail[2 * GLA_RANK:]
    w_lr = jnp.pad(tail[:2 * GLA_RANK], ((0, LR_COLS - 2 * GLA_RANK), (0, 0)))
    wg_pad = jnp.zeros((2, LR_COLS, GLA_KEY), F32)
    wg_pad = wg_pad.at[0, 0:GLA_RANK].set(w_gla_gate[0].astype(F32))
    wg_pad = wg_pad.at[1, GLA_RANK:2 * GLA_RANK].set(w_gla_gate[1].astype(F32)).astype(BF16)
    bg = b_gla_gate.astype(F32).reshape(2, 1, GLA_KEY)
    return w_t, w_gates, w_lr, wg_pad, bg


def _layer(x3, tables, norm_mix, prepared, q_norm, k_norm, gla_norm, wba, wbg, wout, norm_ffn, w1, w2):
    B, S, _ = x3.shape
    T = B * S
    w_t, w_gates, w_lr, wg_pad, bg = prepared
    tn = 1024
    x = x3.reshape(T, D_MODEL)
    xn, lr = _xnorm(x, norm_mix.reshape(1, D_MODEL), w_lr)
    o_groups, lse_groups = [], []
    for g in range(N_GROUPS):
        r = ATTN_GROUPS[g][1]
        a_g = _proj(xn, w_t, lambda j, g=g: N_GROUPS * j + g, 3, r, B, S, f"proj_attn_r{r}", tn=tn)
        o_g, lse_g = _attention_group(a_g.reshape(B, r, S // r, 3 * ATTN_OUT), tables, q_norm, k_norm, g, B, S)
        o_groups.append(o_g)
        lse_groups.append(lse_g)
    p1 = _proj(xn, w_t, lambda j: W_COL_QG // tn + j, P1_COLS // tn, 1, B, S, "proj_gla", tn=tn)
    p2 = _proj(xn, w_gates, lambda j: j, 2 * D_MODEL // tn, 1, B, S, "proj_gates", tn=tn)
    o_fwd, o_bwd = _gla(p1.reshape(B, S, P1_COLS), lr.reshape(B, S, LR_COLS), wg_pad, bg, B, S)
    h, hn = _merge(o_groups, lse_groups, o_fwd.reshape(T, GLA_VAL), o_bwd.reshape(T, GLA_VAL), p1, p2, x,
                   wba, wbg, wout, gla_norm.reshape(1, GLA_DV), norm_ffn.reshape(1, D_MODEL), B, S)
    y = _ffn(hn, h, w1, w2)
    return y.reshape(B, S, D_MODEL)


def kernel(x_prompt, x_sample, norm_mix, w_in, q_norm, k_norm, w_gla_gate, b_gla_gate, gla_norm,
           w_branch_attn, w_branch_gla, w_out, norm_ffn, w_ff1, w_ff2):
    depth = w_in.shape[0]
    tables = _rope_tables(max(x_prompt.shape[1], x_sample.shape[1]))
    layers = []
    for l in range(depth):
        layers.append((
            norm_mix[l], _prepare_layer(w_in[l], w_gla_gate[l], b_gla_gate[l]),
            q_norm[l].astype(F32), k_norm[l].astype(F32), gla_norm[l].astype(F32),
            w_branch_attn[l].astype(BF16), w_branch_gla[l].astype(BF16), w_out[l].astype(BF16),
            norm_ffn[l].astype(F32), w_ff1[l].astype(BF16), w_ff2[l].astype(BF16)))
    outs = []
    for x in (x_prompt, x_sample):
        for layer in layers:
            x = _layer(x, tables, *layer)
        outs.append(x)
    return tuple(outs)
```

```python
import functools

import jax
import jax.numpy as jnp
from jax import lax
from jax.experimental import pallas as pl
from jax.experimental.pallas import tpu as pltpu

F32 = jnp.float32
BF16 = jnp.bfloat16

D_MODEL = 2048
HEAD_DIM = 128
ATTN_GROUPS = ((128, 1), (512, 4), (2048, 16))
N_GROUPS = 3
HEADS_PER_GROUP = 8
ATTN_QKV = N_GROUPS * HEADS_PER_GROUP * HEAD_DIM
ATTN_OUT = HEADS_PER_GROUP * HEAD_DIM
ROT_DIM = HEAD_DIM // 4
ROPE_THETA = 500000.0
GLA_HEADS = 4
GLA_KEY = 1024
GLA_VAL = 2048
GLA_DK = 256
GLA_DV = 512
GLA_RANK = 16
GLA_NORMALIZER = 16.0
D_FF = 4 * D_MODEL
EPS = 1e-6

W_COL_QG = 3 * ATTN_QKV
W_COL_LR = W_COL_QG + 2 * GLA_KEY + 2 * GLA_VAL
W_COL_GATES = W_COL_LR + 2 * GLA_RANK
COL_QG = 0
COL_KG = 1024
COL_VG = 2048
COL_RG = 4096
P1_COLS = 6144
COL_GA = 0
COL_GB = 2048
LR_COLS = 128
PROJ_CHUNK = 256

V7X_VMEM_LIMIT = 56 * 1024 * 1024
NEG = -1e30

ATTN_HALF = 64
ATTN_TQ = 128
ATTN_CHAINS = 8
GLA_BLOCK = 256
GLA_SUB = 64
GLA_HB = 4


def _dot(a, b):
    return jnp.dot(a, b, preferred_element_type=F32)


def _dot_nt(a, b):
    return lax.dot_general(a, b, (((1,), (1,)), ((), ())), preferred_element_type=F32)


def _dot_tn(a, b):
    return lax.dot_general(a, b, (((0,), (0,)), ((), ())), preferred_element_type=F32)


def _split_bf16(x):
    hi = x.astype(BF16)
    lo = (x - hi.astype(F32)).astype(BF16)
    return hi, lo


def _xnorm_kernel(x_ref, gain_ref, wlr_ref, xn_ref, lr_ref):
    x = x_ref[...]
    ms = jnp.mean(x * x, axis=-1, keepdims=True)
    xn = (x * lax.rsqrt(ms + EPS) * gain_ref[...]).astype(BF16)
    xn_ref[...] = xn
    lr_ref[...] = _dot_nt(xn, wlr_ref[...].astype(BF16))


def _xnorm(x, gain, w_t, *, tm=512):
    t = x.shape[0]
    return pl.pallas_call(
        _xnorm_kernel,
        grid=(t // tm,),
        in_specs=[
            pl.BlockSpec((tm, D_MODEL), lambda i: (i, 0)),
            pl.BlockSpec((1, D_MODEL), lambda i: (0, 0)),
            pl.BlockSpec((pl.Element(LR_COLS), pl.Element(D_MODEL)), lambda i: (W_COL_LR, 0)),
        ],
        out_specs=[
            pl.BlockSpec((tm, D_MODEL), lambda i: (i, 0)),
            pl.BlockSpec((tm, LR_COLS), lambda i: (i, 0)),
        ],
        out_shape=[
            jax.ShapeDtypeStruct((t, D_MODEL), BF16),
            jax.ShapeDtypeStruct((t, LR_COLS), F32),
        ],
        compiler_params=pltpu.CompilerParams(
            dimension_semantics=("parallel",),
            vmem_limit_bytes=V7X_VMEM_LIMIT),
        name="xnorm",
    )(x, gain, w_t)


def _proj_kernel(xn_ref, w_ref, o_ref, wb_s, *scratch, r):
    @pl.when(pl.program_id(1) == 0)
    def _():
        wb_s[...] = w_ref[...].astype(BF16)

    xn = xn_ref[...]
    tm, tn = xn.shape[0], wb_s.shape[0]

    def store(cb, acc):
        if r == 1:
            o_ref[:, cb * PROJ_CHUNK:(cb + 1) * PROJ_CHUNK] = acc.astype(BF16)
            return
        acc_s, = scratch
        for kb in range(PROJ_CHUNK // HEAD_DIM):
            lb = cb * (PROJ_CHUNK // HEAD_DIM) + kb
            acc_s[lb] = acc[:, kb * HEAD_DIM:(kb + 1) * HEAD_DIM]
            for c in range(r):
                o_ref[0, c, :, lb * HEAD_DIM:(lb + 1) * HEAD_DIM] = (
                    acc_s[lb, pl.ds(c, tm // r, stride=r), :].astype(BF16))

    pending = None
    for cb in range(tn // PROJ_CHUNK):
        acc = _dot_nt(xn, wb_s[cb * PROJ_CHUNK:(cb + 1) * PROJ_CHUNK, :])
        if pending is not None:
            store(*pending)
        pending = (cb, acc)
    store(*pending)


def _proj(xn, w_t, row_start, ncol_blocks, r, B, S, name, *, tm=1024, tn=1024):
    t = xn.shape[0]
    per_b = S // tm
    if r == 1:
        out_spec = pl.BlockSpec((tm, tn), lambda j, i: (i, j))
        out_shape = jax.ShapeDtypeStruct((t, ncol_blocks * tn), BF16)
        scratch = []
    else:
        out_spec = pl.BlockSpec((1, r, tm // r, tn), lambda j, i: (i // per_b, 0, i % per_b, j))
        out_shape = jax.ShapeDtypeStruct((B, r, S // r, ncol_blocks * tn), BF16)
        scratch = [pltpu.VMEM((tn // HEAD_DIM, tm, HEAD_DIM), F32)]
    return pl.pallas_call(
        functools.partial(_proj_kernel, r=r),
        grid=(ncol_blocks, t // tm),
        in_specs=[
            pl.BlockSpec((tm, D_MODEL), lambda j, i: (i, 0)),
            pl.BlockSpec((pl.Element(tn), pl.Element(D_MODEL)), lambda j, i: (pl.multiple_of(row_start(j), 8), 0)),
        ],
        out_specs=out_spec,
        out_shape=out_shape,
        scratch_shapes=[pltpu.VMEM((tn, D_MODEL), BF16)] + scratch,
        compiler_params=pltpu.CompilerParams(
            dimension_semantics=("parallel", "arbitrary"),
            vmem_limit_bytes=V7X_VMEM_LIMIT),
        name=name,
    )(xn, w_t)


def _attn_kernel(q_ref, k_ref, v_ref, cos_ref, sin_ref, qg_ref, kg_ref,
                 o_ref, lse_ref, qs, ks, vs, bias_s, *, L, hb, U):
    TQ, HALF = ATTN_TQ, ATTN_HALF
    TK = TQ + 2 * HALF
    NT = L // TQ
    TR = min(L, 256)
    hblk = pl.program_id(2)

    @pl.when(hblk == 0)
    def _():
        lse_ref[...] = jnp.zeros_like(lse_ref)

    ri = lax.broadcasted_iota(jnp.int32, (TQ, TK), 0)
    ci = lax.broadcasted_iota(jnp.int32, (TQ, TK), 1)
    d = ci - ri
    band = jnp.where(d < 0, NEG, jnp.where(d > 2 * HALF, NEG, 0.0)).astype(F32)
    first = jnp.where(ci < HALF, NEG, band)
    bias_s[0] = band
    bias_s[1] = first
    bias_s[2] = jnp.where(ci >= TQ + HALF, NEG, band)
    bias_s[3] = jnp.where(ci >= TQ + HALF, NEG, first)

    a = lax.broadcasted_iota(jnp.int32, (HEAD_DIM, HEAD_DIM), 0)
    b = lax.broadcasted_iota(jnp.int32, (HEAD_DIM, HEAD_DIM), 1)
    half = ROT_DIM // 2
    ones_m = jnp.ones((HEAD_DIM, HEAD_DIM), BF16)
    rot_m = jnp.where((b < half) & (a == b + half), -1.0,
                      jnp.where((b >= half) & (b < ROT_DIM) & (a == b - half), 1.0, 0.0)).astype(BF16)

    zpad = jnp.zeros((HALF, HEAD_DIM), BF16)
    for hh in range(hb):
        ks[hh, 0:HALF, :] = zpad
        ks[hh, L + HALF:L + 2 * HALF, :] = zpad
        vs[hh, 0:HALF, 0:HEAD_DIM] = zpad
        vs[hh, L + HALF:L + 2 * HALF, 0:HEAD_DIM] = zpad
        vs[hh, :, HEAD_DIM:2 * HEAD_DIM] = jnp.ones((L + 2 * HALF, HEAD_DIM), BF16)

    qgain = qg_ref[...] * (HEAD_DIM ** -0.5)
    kgain = kg_ref[...]

    def norm_rope(x, gain, cos, sin):
        ssq = _dot((x * x).astype(BF16), ones_m)
        xn = x * lax.rsqrt(ssq * (1.0 / HEAD_DIM) + EPS) * gain
        return xn * cos + _dot(xn.astype(BF16), rot_m) * sin

    def prep(t, carry):
        r0 = pl.multiple_of(t * TR, TR)
        rows = pl.ds(r0, TR)
        prow = pl.ds(pl.multiple_of(r0 + HALF, HALF), TR)
        cos, sin = cos_ref[rows, :], sin_ref[rows, :]
        for hh in range(hb):
            hs = slice(hh * HEAD_DIM, (hh + 1) * HEAD_DIM)
            qs[hh, rows, :] = norm_rope(q_ref[0, 0, rows, hs].astype(F32), qgain, cos, sin).astype(BF16)
            ks[hh, prow, :] = norm_rope(k_ref[0, 0, rows, hs].astype(F32), kgain, cos, sin).astype(BF16)
            vs[hh, prow, 0:HEAD_DIM] = v_ref[0, 0, rows, hs]
        return carry

    lax.fori_loop(0, L // TR, prep, 0)

    lane = lax.broadcasted_iota(jnp.int32, (TQ, HEAD_DIM), 1)

    def tiles(tt, carry):
        for u in range(U):
            t = tt * U + u
            q0 = pl.multiple_of(t * TQ, TQ)
            qrows = pl.ds(q0, TQ)
            krows = pl.ds(q0, TK)
            bias = bias_s[jnp.where(t == 0, 1, 0) + jnp.where(t == NT - 1, 2, 0)]
            lse_tile = lse_ref[0, 0, qrows, :]
            for hh in range(hb):
                hs = slice(hh * HEAD_DIM, (hh + 1) * HEAD_DIM)
                s = _dot_nt(qs[hh, qrows, :], ks[hh, krows, :]) + bias
                m = jnp.max(s, axis=-1, keepdims=True)
                p = jnp.exp(s - m).astype(BF16)
                acc = _dot(p, vs[hh, krows, :])
                den = acc[:, HEAD_DIM:]
                o_ref[0, 0, qrows, hs] = (acc[:, :HEAD_DIM] * (1.0 / den)).astype(BF16)
                lse_tile = jnp.where(lane == hblk * hb + hh, m + jnp.log(den), lse_tile)
            lse_ref[0, 0, qrows, :] = lse_tile
        return carry

    lax.fori_loop(0, NT // U, tiles, 0)


def _attention_group(a_g, tables, q_gain, k_gain, g, B, S):
    window, r = ATTN_GROUPS[g]
    assert window // (2 * r) == ATTN_HALF
    L = S // r
    assert L % ATTN_TQ == 0
    hb = max(1, min(HEADS_PER_GROUP, 8192 // L))
    nhb = HEADS_PER_GROUP // hb
    bw = hb * HEAD_DIM
    U = max(1, min(ATTN_CHAINS // hb, L // ATTN_TQ))
    assert (L // ATTN_TQ) % U == 0
    cos, sin = (t[:S].reshape(L, r * HEAD_DIM) for t in tables)

    def col_map(part):
        return lambda b, c, h: (b, c, 0, part * nhb + h)

    tab_spec = pl.BlockSpec((L, HEAD_DIM), lambda b, c, h: (0, c))
    gain_spec = pl.BlockSpec((1, HEAD_DIM), lambda b, c, h: (0, 0))
    return pl.pallas_call(
        functools.partial(_attn_kernel, L=L, hb=hb, U=U),
        grid=(B, r, nhb),
        in_specs=[
            pl.BlockSpec((1, 1, L, bw), col_map(0)),
            pl.BlockSpec((1, 1, L, bw), col_map(1)),
            pl.BlockSpec((1, 1, L, bw), col_map(2)),
            tab_spec, tab_spec, gain_spec, gain_spec,
        ],
        out_specs=[
            pl.BlockSpec((1, 1, L, bw), lambda b, c, h: (b, c, 0, h)),
            pl.BlockSpec((1, 1, L, HEAD_DIM), lambda b, c, h: (b, c, 0, 0)),
        ],
        out_shape=[
            jax.ShapeDtypeStruct((B, r, L, ATTN_OUT), BF16),
            jax.ShapeDtypeStruct((B, r, L, HEAD_DIM), F32),
        ],
        scratch_shapes=[
            pltpu.VMEM((hb, L, HEAD_DIM), BF16),
            pltpu.VMEM((hb, L + 2 * ATTN_HALF, HEAD_DIM), BF16),
            pltpu.VMEM((hb, L + 2 * ATTN_HALF, 2 * HEAD_DIM), BF16),
            pltpu.VMEM((4, ATTN_TQ, ATTN_TQ + 2 * ATTN_HALF), F32),
        ],
        compiler_params=pltpu.CompilerParams(
            dimension_semantics=("parallel", "parallel", "arbitrary"),
            vmem_limit_bytes=V7X_VMEM_LIMIT),
        name=f"attn_g{g}",
    )(a_g, a_g, a_g, cos, sin, q_gain[g:g + 1], k_gain[g:g + 1])


def _gla_kernel(qf_ref, kf_ref, vf_ref, lrf_ref, qb_ref, kb_ref, vb_ref, lrb_ref, wg_ref, bg_ref,
                of_ref, ob_ref, stf, stb):
    C, SC = GLA_BLOCK, GLA_SUB
    NS = C // SC
    n = pl.program_id(2)

    @pl.when(n == 0)
    def _():
        stf[...] = jnp.zeros_like(stf)
        stb[...] = jnp.zeros_like(stb)

    ri = lax.broadcasted_iota(jnp.int32, (C, C), 0)
    ci = lax.broadcasted_iota(jnp.int32, (C, C), 1)

    def direction(q_ref, k_ref, v_ref, lr_ref, o_ref, st, d, backward, h):
        ks = slice(h * GLA_DK, (h + 1) * GLA_DK)
        vs = slice(h * GLA_DV, (h + 1) * GLA_DV)
        z = _dot(lr_ref[0].astype(BF16), wg_ref[d, :, ks]) + bg_ref[d, :, ks]
        yield
        softplus = jnp.maximum(-z, 0.0) + jnp.log(1.0 + jnp.exp(-jnp.abs(z)))
        g_hi, g_lo = _split_bf16(softplus * (-1.0 / GLA_NORMALIZER))
        tri = jnp.where((ci >= ri) if backward else (ri >= ci), 1.0, 0.0).astype(BF16)
        cum = _dot(tri, g_hi) + _dot(tri, g_lo)
        yield
        mid_row = SC // 2 if backward else SC // 2 - 1
        end_row = 0 if backward else C - 1
        mids = [cum[I * SC + mid_row:I * SC + mid_row + 1, :] for I in range(NS)]
        end = cum[end_row:end_row + 1, :]
        sub = [slice(I * SC, (I + 1) * SC) for I in range(NS)]
        dl = jnp.concatenate([cum[sub[I], :] - mids[I] for I in range(NS)], axis=0)
        qd = q_ref[0, :, ks].astype(F32) * (GLA_DK ** -0.5) * jnp.exp(dl)
        kd = k_ref[0, :, ks].astype(F32) * jnp.exp(-dl)
        qd_b = qd.astype(BF16)
        kd_b = kd.astype(BF16)
        att_rows = []
        for I in range(NS):
            blocks = []
            for J in range(NS):
                if (J > I) if backward else (J < I):
                    blocks.append((kd[sub[J], :] * jnp.exp(mids[I] - mids[J])).astype(BF16))
                else:
                    blocks.append(kd_b[sub[J], :])
            att_rows.append(_dot_nt(qd_b[sub[I], :], jnp.concatenate(blocks, axis=0)))
        yield
        mask = (ci > ri) if backward else (ri >= ci)
        att = jnp.where(mask, jnp.concatenate(att_rows, axis=0), 0.0).astype(BF16)
        qi = jnp.concatenate([qd[sub[I], :] * jnp.exp(mids[I]) for I in range(NS)], axis=0).astype(BF16)
        k2 = jnp.concatenate([kd[sub[I], :] * jnp.exp(end - mids[I]) for I in range(NS)], axis=0).astype(BF16)
        v = v_ref[0, :, vs]
        state = st[h]
        o = _dot(att, v) + _dot_nt(qi, state.astype(BF16))
        upd = _dot_tn(v, k2)
        yield
        o_ref[0, :, vs] = o.astype(BF16)
        st[h] = state * jnp.exp(end) + upd

    chains = []
    for h in range(GLA_HB):
        chains.append(direction(qf_ref, kf_ref, vf_ref, lrf_ref, of_ref, stf, 0, False, h))
        chains.append(direction(qb_ref, kb_ref, vb_ref, lrb_ref, ob_ref, stb, 1, True, h))
    while chains:
        alive = []
        for chain in chains:
            if next(chain, chain) is not chain:
                alive.append(chain)
        chains = alive


def _gla(p3, lr3, wg_pad, bg, B, S):
    TS = GLA_BLOCK
    NB = S // TS
    HB = GLA_HB
    kq, kk, kv = COL_QG // (HB * GLA_DK), COL_KG // (HB * GLA_DK), COL_VG // (HB * GLA_DV)

    def fwd(base):
        return lambda b, h, n: (b, n, base + h)

    def bwd(base):
        return lambda b, h, n: (b, NB - 1 - n, base + h)

    def specs(m, lr_map):
        return [
            pl.BlockSpec((1, TS, HB * GLA_DK), m(kq)),
            pl.BlockSpec((1, TS, HB * GLA_DK), m(kk)),
            pl.BlockSpec((1, TS, HB * GLA_DV), m(kv)),
            pl.BlockSpec((1, TS, LR_COLS), lr_map),
        ]

    return pl.pallas_call(
        _gla_kernel,
        grid=(B, GLA_HEADS // HB, NB),
        in_specs=specs(fwd, lambda b, h, n: (b, n, 0)) + specs(bwd, lambda b, h, n: (b, NB - 1 - n, 0)) + [
            pl.BlockSpec((2, LR_COLS, HB * GLA_DK), lambda b, h, n: (0, 0, h)),
            pl.BlockSpec((2, 1, HB * GLA_DK), lambda b, h, n: (0, 0, h)),
        ],
        out_specs=[
            pl.BlockSpec((1, TS, HB * GLA_DV), lambda b, h, n: (b, n, h)),
            pl.BlockSpec((1, TS, HB * GLA_DV), lambda b, h, n: (b, NB - 1 - n, h)),
        ],
        out_shape=[jax.ShapeDtypeStruct((B, S, GLA_VAL), BF16)] * 2,
        scratch_shapes=[
            pltpu.VMEM((HB, GLA_DV, GLA_DK), F32),
            pltpu.VMEM((HB, GLA_DV, GLA_DK), F32),
        ],
        compiler_params=pltpu.CompilerParams(
            dimension_semantics=("parallel", "parallel", "arbitrary"),
            vmem_limit_bytes=V7X_VMEM_LIMIT),
        name="gla",
    )(p3, p3, p3, lr3, p3, p3, p3, lr3, wg_pad, bg)


def _sigmoid(x):
    return 1.0 / (1.0 + jnp.exp(-x))


def _merge_kernel(o0_ref, o1_ref, o2_ref, l0_ref, l1_ref, l2_ref, of_ref, ob_ref, rg_ref, ga_ref, gb_ref,
                  x_ref, wba_ref, wbg_ref, wout_ref, gnorm_ref, nffn_ref, h_ref, hn_ref,
                  oa_s, og_s, oil_s, lil_s):
    tm = x_ref.shape[0]
    for gi, (o_ref, l_ref) in enumerate(((o1_ref, l1_ref), (o2_ref, l2_ref))):
        r = o_ref.shape[1]
        for c in range(r):
            dst = pl.ds(c, tm // r, stride=r)
            lil_s[gi, dst, :] = l_ref[0, c]
            for hh in range(HEADS_PER_GROUP):
                oil_s[gi, hh, dst, :] = o_ref[0, c, :, hh * HEAD_DIM:(hh + 1) * HEAD_DIM].astype(F32)

    l0, l1, l2 = l0_ref[0, 0], lil_s[0], lil_s[1]
    m = jnp.maximum(jnp.maximum(l0, l1), l2)
    e0, e1, e2 = jnp.exp(l0 - m), jnp.exp(l1 - m), jnp.exp(l2 - m)
    inv = 1.0 / (e0 + e1 + e2)
    w0, w1, w2 = e0 * inv, e1 * inv, e2 * inv
    for hh in range(HEADS_PER_GROUP):
        hs = slice(hh * HEAD_DIM, (hh + 1) * HEAD_DIM)
        comb = (w0[:, hh:hh + 1] * o0_ref[0, 0, :, hs].astype(F32)
                + w1[:, hh:hh + 1] * oil_s[0, hh]
                + w2[:, hh:hh + 1] * oil_s[1, hh])
        oa_s[:, hs] = comb.astype(BF16)
    u_a = _dot(oa_s[...], wba_ref[...])

    for h in range(GLA_HEADS):
        vs = slice(h * GLA_DV, (h + 1) * GLA_DV)
        og = of_ref[:, vs].astype(F32) + ob_ref[:, vs].astype(F32)
        ms = jnp.mean(og * og, axis=-1, keepdims=True)
        ogn = og * lax.rsqrt(ms + EPS) * gnorm_ref[...]
        rg = rg_ref[:, vs].astype(F32)
        og_s[:, vs] = (ogn * (rg * _sigmoid(rg))).astype(BF16)
    u_b = _dot(og_s[...], wbg_ref[...])

    merged = _sigmoid(ga_ref[...].astype(F32)) * u_a + _sigmoid(gb_ref[...].astype(F32)) * u_b
    h = x_ref[...] + _dot(merged.astype(BF16), wout_ref[...])
    h_ref[...] = h
    ms = jnp.mean(h * h, axis=-1, keepdims=True)
    hn_ref[...] = (h * lax.rsqrt(ms + EPS) * nffn_ref[...]).astype(BF16)


def _merge(o_groups, lse_groups, o_fwd, o_bwd, p1, p2, x, wba, wbg, wout, gnorm, nffn, B, S, *, tm=256):
    t = x.shape[0]
    per_b = S // tm
    row = lambda b, i: (b * per_b + i, 0)
    const = lambda b, i: (0, 0)

    def resident(shape):
        return pl.BlockSpec(shape, const, pipeline_mode=pl.Buffered(1))

    def split_spec(r, width):
        return pl.BlockSpec((1, r, tm // r, width), lambda b, i: (b, 0, i, 0))

    rs = [r for _, r in ATTN_GROUPS]
    return pl.pallas_call(
        _merge_kernel,
        grid=(B, per_b),
        in_specs=[split_spec(r, ATTN_OUT) for r in rs] + [split_spec(r, HEAD_DIM) for r in rs] + [
            pl.BlockSpec((tm, GLA_VAL), row),
            pl.BlockSpec((tm, GLA_VAL), row),
            pl.BlockSpec((tm, GLA_VAL), lambda b, i: (b * per_b + i, COL_RG // GLA_VAL)),
            pl.BlockSpec((tm, D_MODEL), lambda b, i: (b * per_b + i, COL_GA // D_MODEL)),
            pl.BlockSpec((tm, D_MODEL), lambda b, i: (b * per_b + i, COL_GB // D_MODEL)),
            pl.BlockSpec((tm, D_MODEL), row),
            resident((ATTN_OUT, D_MODEL)),
            resident((GLA_VAL, D_MODEL)),
            resident((D_MODEL, D_MODEL)),
            resident((1, GLA_DV)),
            resident((1, D_MODEL)),
        ],
        out_specs=[pl.BlockSpec((tm, D_MODEL), row), pl.BlockSpec((tm, D_MODEL), row)],
        out_shape=[jax.ShapeDtypeStruct((t, D_MODEL), F32), jax.ShapeDtypeStruct((t, D_MODEL), BF16)],
        scratch_shapes=[
            pltpu.VMEM((tm, ATTN_OUT), BF16),
            pltpu.VMEM((tm, GLA_VAL), BF16),
            pltpu.VMEM((2, HEADS_PER_GROUP, tm, HEAD_DIM), F32),
            pltpu.VMEM((2, tm, HEAD_DIM), F32),
        ],
        compiler_params=pltpu.CompilerParams(
            dimension_semantics=("parallel", "parallel"),
            vmem_limit_bytes=V7X_VMEM_LIMIT),
        name="merge",
    )(*o_groups, *lse_groups, o_fwd, o_bwd, p1, p2, p2, x, wba, wbg, wout, gnorm, nffn)


def _ffn_kernel(hn_ref, w1_ref, w2_ref, h_ref, y_ref):
    j = pl.program_id(1)
    a = _dot(hn_ref[...], w1_ref[...])
    a = jnp.square(jnp.maximum(a, 0.0)).astype(BF16)
    y = _dot(a, w2_ref[...])

    @pl.when(j == 0)
    def _():
        y_ref[...] = h_ref[...] + y

    @pl.when(j > 0)
    def _():
        y_ref[...] += y


def _ffn(hn, h, w1, w2, *, tm=512, tf=1024):
    t = hn.shape[0]
    return pl.pallas_call(
        _ffn_kernel,
        grid=(t // tm, D_FF // tf),
        in_specs=[
            pl.BlockSpec((tm, D_MODEL), lambda i, j: (i, 0)),
            pl.BlockSpec((D_MODEL, tf), lambda i, j: (0, j)),
            pl.BlockSpec((tf, D_MODEL), lambda i, j: (j, 0)),
            pl.BlockSpec((tm, D_MODEL), lambda i, j: (i, 0)),
        ],
        out_specs=pl.BlockSpec((tm, D_MODEL), lambda i, j: (i, 0)),
        out_shape=jax.ShapeDtypeStruct((t, D_MODEL), F32),
        compiler_params=pltpu.CompilerParams(
            dimension_semantics=("parallel", "arbitrary"),
            vmem_limit_bytes=V7X_VMEM_LIMIT),
        name="ffn",
    )(hn, w1, w2, h)


def _rope_tables(s_max):
    inv_freq = ROPE_THETA ** (-jnp.arange(0, ROT_DIM, 2, dtype=F32) / ROT_DIM)
    ang = jnp.arange(s_max, dtype=F32)[:, None] * inv_freq[None, :]
    cos, sin = jnp.cos(ang), jnp.sin(ang)
    rest = HEAD_DIM - ROT_DIM
    cos_t = jnp.concatenate([cos, cos, jnp.ones((s_max, rest), F32)], axis=1)
    sin_t = jnp.concatenate([sin, sin, jnp.zeros((s_max, rest), F32)], axis=1)
    return cos_t, sin_t


def _prepare_layer(w_in, w_gla_gate, b_gla_gate):
    w_t = jnp.swapaxes(w_in, 0, 1)
    wg_pad = jnp.zeros((2, LR_COLS, GLA_KEY), F32)
    wg_pad = wg_pad.at[0, 0:GLA_RANK].set(w_gla_gate[0].astype(F32))
    wg_pad = wg_pad.at[1, GLA_RANK:2 * GLA_RANK].set(w_gla_gate[1].astype(F32)).astype(BF16)
    bg = b_gla_gate.astype(F32).reshape(2, 1, GLA_KEY)
    return w_t, wg_pad, bg


def _layer(x3, tables, norm_mix, prepared, q_norm, k_norm, gla_norm, wba, wbg, wout, norm_ffn, w1, w2):
    B, S, _ = x3.shape
    T = B * S
    w_t, wg_pad, bg = prepared
    tn = 1024
    x = x3.reshape(T, D_MODEL)
    xn, lr = _xnorm(x, norm_mix.reshape(1, D_MODEL), w_t)
    o_groups, lse_groups = [], []
    for g in range(N_GROUPS):
        r = ATTN_GROUPS[g][1]
        a_g = _proj(xn, w_t, lambda j, g=g: j * ATTN_QKV + g * ATTN_OUT, 3, r, B, S, f"proj_attn_r{r}", tn=tn)
        o_g, lse_g = _attention_group(a_g.reshape(B, r, S // r, 3 * ATTN_OUT), tables, q_norm, k_norm, g, B, S)
        o_groups.append(o_g)
        lse_groups.append(lse_g)
    p1 = _proj(xn, w_t, lambda j: W_COL_QG + j * tn, P1_COLS // tn, 1, B, S, "proj_gla", tn=tn)
    p2 = _proj(xn, w_t, lambda j: W_COL_GATES + j * tn, 2 * D_MODEL // tn, 1, B, S, "proj_gates", tn=tn)
    o_fwd, o_bwd = _gla(p1.reshape(B, S, P1_COLS), lr.reshape(B, S, LR_COLS), wg_pad, bg, B, S)
    h, hn = _merge(o_groups, lse_groups, o_fwd.reshape(T, GLA_VAL), o_bwd.reshape(T, GLA_VAL), p1, p2, x,
                   wba, wbg, wout, gla_norm.reshape(1, GLA_DV), norm_ffn.reshape(1, D_MODEL), B, S)
    y = _ffn(hn, h, w1, w2)
    return y.reshape(B, S, D_MODEL)


def kernel(x_prompt, x_sample, norm_mix, w_in, q_norm, k_norm, w_gla_gate, b_gla_gate, gla_norm,
           w_branch_attn, w_branch_gla, w_out, norm_ffn, w_ff1, w_ff2):
    depth = w_in.shape[0]
    tables = _rope_tables(max(x_prompt.shape[1], x_sample.shape[1]))
    layers = []
    for l in range(depth):
        layers.append((
            norm_mix[l], _prepare_layer(w_in[l], w_gla_gate[l], b_gla_gate[l]),
            q_norm[l].astype(F32), k_norm[l].astype(F32), gla_norm[l].astype(F32),
            w_branch_attn[l].astype(BF16), w_branch_gla[l].astype(BF16), w_out[l].astype(BF16),
            norm_ffn[l].astype(F32), w_ff1[l].astype(BF16), w_ff2[l].astype(BF16)))
    outs = []
    for x in (x_prompt, x_sample):
        for layer in layers:
            x = _layer(x, tables, *layer)
        outs.append(x)
    return tuple(outs)
```

```python
import functools

import jax
import jax.numpy as jnp
from jax import lax
from jax.experimental import pallas as pl
from jax.experimental.pallas import tpu as pltpu

F32 = jnp.float32
BF16 = jnp.bfloat16

D_MODEL = 2048
HEAD_DIM = 128
ATTN_GROUPS = ((128, 1), (512, 4), (2048, 16))
N_GROUPS = 3
HEADS_PER_GROUP = 8
ATTN_QKV = N_GROUPS * HEADS_PER_GROUP * HEAD_DIM
ATTN_OUT = HEADS_PER_GROUP * HEAD_DIM
ROT_DIM = HEAD_DIM // 4
ROPE_THETA = 500000.0
GLA_HEADS = 4
GLA_KEY = 1024
GLA_VAL = 2048
GLA_DK = 256
GLA_DV = 512
GLA_RANK = 16
GLA_NORMALIZER = 16.0
D_FF = 4 * D_MODEL
EPS = 1e-6

W_COL_QG = 3 * ATTN_QKV
W_COL_LR = W_COL_QG + 2 * GLA_KEY + 2 * GLA_VAL
W_COL_GATES = W_COL_LR + 2 * GLA_RANK
COL_QG = 0
COL_KG = 1024
COL_VG = 2048
COL_RG = 4096
P1_COLS = 6144
COL_GA = 0
COL_GB = 2048
LR_COLS = 128

V7X_VMEM_LIMIT = 56 * 1024 * 1024
NEG = -1e30

ATTN_HALF = 64
ATTN_TQ = 128
ATTN_CHAINS = 8
GLA_BLOCK = 256
GLA_SUB = 64
GLA_HB = 4


def _dot(a, b):
    return jnp.dot(a, b, preferred_element_type=F32)


def _dot_nt(a, b):
    return lax.dot_general(a, b, (((1,), (1,)), ((), ())), preferred_element_type=F32)


def _dot_tn(a, b):
    return lax.dot_general(a, b, (((0,), (0,)), ((), ())), preferred_element_type=F32)


def _split_bf16(x):
    hi = x.astype(BF16)
    lo = (x - hi.astype(F32)).astype(BF16)
    return hi, lo


def _xnorm_kernel(x_ref, gain_ref, wlr_ref, xn_ref, lr_ref):
    x = x_ref[...]
    ms = jnp.mean(x * x, axis=-1, keepdims=True)
    xn = (x * lax.rsqrt(ms + EPS) * gain_ref[...]).astype(BF16)
    xn_ref[...] = xn
    lr_ref[...] = _dot_nt(xn, wlr_ref[...].astype(BF16))


def _xnorm(x, gain, w_t, *, tm=512):
    t = x.shape[0]
    return pl.pallas_call(
        _xnorm_kernel,
        grid=(t // tm,),
        in_specs=[
            pl.BlockSpec((tm, D_MODEL), lambda i: (i, 0)),
            pl.BlockSpec((1, D_MODEL), lambda i: (0, 0)),
            pl.BlockSpec((pl.Element(LR_COLS), pl.Element(D_MODEL)), lambda i: (W_COL_LR, 0)),
        ],
        out_specs=[
            pl.BlockSpec((tm, D_MODEL), lambda i: (i, 0)),
            pl.BlockSpec((tm, LR_COLS), lambda i: (i, 0)),
        ],
        out_shape=[
            jax.ShapeDtypeStruct((t, D_MODEL), BF16),
            jax.ShapeDtypeStruct((t, LR_COLS), F32),
        ],
        compiler_params=pltpu.CompilerParams(
            dimension_semantics=("parallel",),
            vmem_limit_bytes=V7X_VMEM_LIMIT),
        name="xnorm",
    )(x, gain, w_t)


def _proj_kernel(xn_ref, w_ref, o_ref, wb_s, *scratch, r):
    @pl.when(pl.program_id(1) == 0)
    def _():
        wb_s[...] = w_ref[...].astype(BF16)

    acc = _dot_nt(xn_ref[...], wb_s[...])
    if r == 1:
        o_ref[...] = acc.astype(BF16)
        return
    acc_s, = scratch
    tm = acc.shape[0]
    for lb in range(acc_s.shape[0]):
        ls = slice(lb * HEAD_DIM, (lb + 1) * HEAD_DIM)
        acc_s[lb] = acc[:, ls]
        for c in range(r):
            o_ref[0, c, :, ls] = acc_s[lb, pl.ds(c, tm // r, stride=r), :].astype(BF16)


def _proj(xn, w_t, row_start, ncol_blocks, r, B, S, name, *, tm=1024, tn=1024):
    t = xn.shape[0]
    per_b = S // tm
    if r == 1:
        out_spec = pl.BlockSpec((tm, tn), lambda j, i: (i, j))
        out_shape = jax.ShapeDtypeStruct((t, ncol_blocks * tn), BF16)
        scratch = []
    else:
        out_spec = pl.BlockSpec((1, r, tm // r, tn), lambda j, i: (i // per_b, 0, i % per_b, j))
        out_shape = jax.ShapeDtypeStruct((B, r, S // r, ncol_blocks * tn), BF16)
        scratch = [pltpu.VMEM((tn // HEAD_DIM, tm, HEAD_DIM), F32)]
    return pl.pallas_call(
        functools.partial(_proj_kernel, r=r),
        grid=(ncol_blocks, t // tm),
        in_specs=[
            pl.BlockSpec((tm, D_MODEL), lambda j, i: (i, 0)),
            pl.BlockSpec((pl.Element(tn), pl.Element(D_MODEL)), lambda j, i: (pl.multiple_of(row_start(j), 8), 0)),
        ],
        out_specs=out_spec,
        out_shape=out_shape,
        scratch_shapes=[pltpu.VMEM((tn, D_MODEL), BF16)] + scratch,
        compiler_params=pltpu.CompilerParams(
            dimension_semantics=("parallel", "arbitrary"),
            vmem_limit_bytes=V7X_VMEM_LIMIT),
        name=name,
    )(xn, w_t)


def _attn_kernel(q_ref, k_ref, v_ref, cos_ref, sin_ref, qg_ref, kg_ref,
                 o_ref, ml_ref, qs, ks, vs, bias_s, *, L, hb, U):
    TQ, HALF = ATTN_TQ, ATTN_HALF
    TK = TQ + 2 * HALF
    NT = L // TQ
    NI = NT // U
    hblk = pl.program_id(2)

    @pl.when(hblk == 0)
    def _():
        ml_ref[...] = jnp.zeros_like(ml_ref)

    ri = lax.broadcasted_iota(jnp.int32, (TQ, TK), 0)
    ci = lax.broadcasted_iota(jnp.int32, (TQ, TK), 1)
    d = ci - ri
    band = jnp.where(d < 0, NEG, jnp.where(d > 2 * HALF, NEG, 0.0)).astype(F32)
    first = jnp.where(ci < HALF, NEG, band)
    bias_s[0] = band
    bias_s[1] = first
    bias_s[2] = jnp.where(ci >= TQ + HALF, NEG, band)
    bias_s[3] = jnp.where(ci >= TQ + HALF, NEG, first)

    a = lax.broadcasted_iota(jnp.int32, (HEAD_DIM, HEAD_DIM), 0)
    b = lax.broadcasted_iota(jnp.int32, (HEAD_DIM, HEAD_DIM), 1)
    half = ROT_DIM // 2
    ones_m = jnp.ones((HEAD_DIM, HEAD_DIM), BF16)
    rot_m = jnp.where((b < half) & (a == b + half), -1.0,
                      jnp.where((b >= half) & (b < ROT_DIM) & (a == b - half), 1.0, 0.0)).astype(BF16)

    zpad = jnp.zeros((HALF, HEAD_DIM), BF16)
    ztail = jnp.zeros((TQ, HEAD_DIM), BF16)
    for hh in range(hb):
        ks[hh, 0:HALF, :] = zpad
        ks[hh, L + HALF:L + HALF + TQ, :] = ztail
        vs[hh, 0:HALF, 0:HEAD_DIM] = zpad
        vs[hh, L + HALF:L + HALF + TQ, 0:HEAD_DIM] = ztail
        vs[hh, :, HEAD_DIM:2 * HEAD_DIM] = jnp.ones((vs.shape[1], HEAD_DIM), BF16)

    qgain = qg_ref[...] * (HEAD_DIM ** -0.5)
    kgain = kg_ref[...]

    def norm_rope(x, gain, cos, sin):
        ssq = _dot((x * x).astype(BF16), ones_m)
        xn = x * lax.rsqrt(ssq * (1.0 / HEAD_DIM) + EPS) * gain
        return xn * cos + _dot(xn.astype(BF16), rot_m) * sin

    def prep(blk):
        if isinstance(blk, int):
            src, dst, koff = min(blk, NT - 1) * TQ, blk * TQ, blk * TQ + HALF
        else:
            src = pl.multiple_of(jnp.minimum(blk, NT - 1) * TQ, TQ)
            dst = pl.multiple_of(blk * TQ, TQ)
            koff = pl.multiple_of(dst + HALF, HALF)
        rows = pl.ds(src, TQ)
        qdst = pl.ds(dst, TQ)
        kdst = pl.ds(koff, TQ)
        cos, sin = cos_ref[rows, :], sin_ref[rows, :]
        for hh in range(hb):
            hs = slice(hh * HEAD_DIM, (hh + 1) * HEAD_DIM)
            qs[hh, qdst, :] = norm_rope(q_ref[0, 0, rows, hs].astype(F32), qgain, cos, sin).astype(BF16)
            ks[hh, kdst, :] = norm_rope(k_ref[0, 0, rows, hs].astype(F32), kgain, cos, sin).astype(BF16)
            vs[hh, kdst, 0:HEAD_DIM] = v_ref[0, 0, rows, hs]

    lane = lax.broadcasted_iota(jnp.int32, (TQ, HEAD_DIM), 1)

    def score_tiles(tt):
        for u in range(U):
            t = tt * U + u
            q0 = t * TQ if isinstance(t, int) else pl.multiple_of(t * TQ, TQ)
            qrows = pl.ds(q0, TQ)
            krows = pl.ds(q0, TK)
            bias = bias_s[jnp.where(t == 0, 1, 0) + jnp.where(t == NT - 1, 2, 0)]
            ml_tile = ml_ref[0, 0, qrows, :]
            for hh in range(hb):
                hs = slice(hh * HEAD_DIM, (hh + 1) * HEAD_DIM)
                h = hblk * hb + hh
                s = _dot_nt(qs[hh, qrows, :], ks[hh, krows, :]) + bias
                m = jnp.max(s, axis=-1, keepdims=True)
                p = jnp.exp(s - m).astype(BF16)
                acc = _dot(p, vs[hh, krows, :])
                o_ref[0, 0, qrows, hs] = acc[:, :HEAD_DIM].astype(BF16)
                ml_tile = jnp.where(lane == h, m, jnp.where(lane == HEADS_PER_GROUP + h, acc[:, HEAD_DIM:], ml_tile))
            ml_ref[0, 0, qrows, :] = ml_tile

    for blk in range(min(U + 1, NT)):
        prep(blk)

    def body(tt, carry):
        score_tiles(tt)
        for u in range(U):
            prep((tt + 1) * U + 1 + u)
        return carry

    lax.fori_loop(0, NI - 1, body, 0)
    score_tiles(NI - 1)


def _attention_group(a_g, tables, q_gain, k_gain, g, B, S):
    window, r = ATTN_GROUPS[g]
    assert window // (2 * r) == ATTN_HALF
    L = S // r
    assert L % ATTN_TQ == 0
    hb = max(1, min(HEADS_PER_GROUP, 8192 // L))
    nhb = HEADS_PER_GROUP // hb
    bw = hb * HEAD_DIM
    U = max(1, min(ATTN_CHAINS // hb, L // ATTN_TQ))
    assert (L // ATTN_TQ) % U == 0
    cos, sin = (t[:S].reshape(L, r * HEAD_DIM) for t in tables)

    def col_map(part):
        return lambda b, c, h: (b, c, 0, part * nhb + h)

    tab_spec = pl.BlockSpec((L, HEAD_DIM), lambda b, c, h: (0, c))
    gain_spec = pl.BlockSpec((1, HEAD_DIM), lambda b, c, h: (0, 0))
    return pl.pallas_call(
        functools.partial(_attn_kernel, L=L, hb=hb, U=U),
        grid=(B, r, nhb),
        in_specs=[
            pl.BlockSpec((1, 1, L, bw), col_map(0)),
            pl.BlockSpec((1, 1, L, bw), col_map(1)),
            pl.BlockSpec((1, 1, L, bw), col_map(2)),
            tab_spec, tab_spec, gain_spec, gain_spec,
        ],
        out_specs=[
            pl.BlockSpec((1, 1, L, bw), lambda b, c, h: (b, c, 0, h)),
            pl.BlockSpec((1, 1, L, HEAD_DIM), lambda b, c, h: (b, c, 0, 0)),
        ],
        out_shape=[
            jax.ShapeDtypeStruct((B, r, L, ATTN_OUT), BF16),
            jax.ShapeDtypeStruct((B, r, L, HEAD_DIM), F32),
        ],
        scratch_shapes=[
            pltpu.VMEM((hb, L + ATTN_TQ, HEAD_DIM), BF16),
            pltpu.VMEM((hb, L + ATTN_TQ + ATTN_HALF, HEAD_DIM), BF16),
            pltpu.VMEM((hb, L + ATTN_TQ + ATTN_HALF, 2 * HEAD_DIM), BF16),
            pltpu.VMEM((4, ATTN_TQ, ATTN_TQ + 2 * ATTN_HALF), F32),
        ],
        compiler_params=pltpu.CompilerParams(
            dimension_semantics=("parallel", "parallel", "arbitrary"),
            vmem_limit_bytes=V7X_VMEM_LIMIT),
        name=f"attn_g{g}",
    )(a_g, a_g, a_g, cos, sin, q_gain[g:g + 1], k_gain[g:g + 1])


def _gla_kernel(qf_ref, kf_ref, vf_ref, lrf_ref, qb_ref, kb_ref, vb_ref, lrb_ref, wg_ref, bg_ref,
                of_ref, ob_ref, stf, stb):
    C, SC = GLA_BLOCK, GLA_SUB
    NS = C // SC
    n = pl.program_id(2)

    @pl.when(n == 0)
    def _():
        stf[...] = jnp.zeros_like(stf)
        stb[...] = jnp.zeros_like(stb)

    ri = lax.broadcasted_iota(jnp.int32, (C, C), 0)
    ci = lax.broadcasted_iota(jnp.int32, (C, C), 1)

    def direction(q_ref, k_ref, v_ref, lr_ref, o_ref, st, d, backward, h):
        ks = slice(h * GLA_DK, (h + 1) * GLA_DK)
        vs = slice(h * GLA_DV, (h + 1) * GLA_DV)
        z = _dot(lr_ref[0].astype(BF16), wg_ref[d, :, ks]) + bg_ref[d, :, ks]
        yield
        softplus = jnp.maximum(-z, 0.0) + jnp.log(1.0 + jnp.exp(-jnp.abs(z)))
        g_hi, g_lo = _split_bf16(softplus * (-1.0 / GLA_NORMALIZER))
        tri = jnp.where((ci >= ri) if backward else (ri >= ci), 1.0, 0.0).astype(BF16)
        cum = _dot(tri, g_hi) + _dot(tri, g_lo)
        yield
        mid_row = SC // 2 if backward else SC // 2 - 1
        end_row = 0 if backward else C - 1
        mids = [cum[I * SC + mid_row:I * SC + mid_row + 1, :] for I in range(NS)]
        end = cum[end_row:end_row + 1, :]
        sub = [slice(I * SC, (I + 1) * SC) for I in range(NS)]
        dl = jnp.concatenate([cum[sub[I], :] - mids[I] for I in range(NS)], axis=0)
        qd = q_ref[0, :, ks].astype(F32) * (GLA_DK ** -0.5) * jnp.exp(dl)
        kd = k_ref[0, :, ks].astype(F32) * jnp.exp(-dl)
        qd_b = qd.astype(BF16)
        kd_b = kd.astype(BF16)
        att_rows = []
        for I in range(NS):
            blocks = []
            for J in range(NS):
                if (J > I) if backward else (J < I):
                    blocks.append((kd[sub[J], :] * jnp.exp(mids[I] - mids[J])).astype(BF16))
                else:
                    blocks.append(kd_b[sub[J], :])
            att_rows.append(_dot_nt(qd_b[sub[I], :], jnp.concatenate(blocks, axis=0)))
        yield
        mask = (ci > ri) if backward else (ri >= ci)
        att = jnp.where(mask, jnp.concatenate(att_rows, axis=0), 0.0).astype(BF16)
        qi = jnp.concatenate([qd[sub[I], :] * jnp.exp(mids[I]) for I in range(NS)], axis=0).astype(BF16)
        k2 = jnp.concatenate([kd[sub[I], :] * jnp.exp(end - mids[I]) for I in range(NS)], axis=0).astype(BF16)
        v = v_ref[0, :, vs]
        state = st[h]
        o = _dot(att, v) + _dot_nt(qi, state.astype(BF16))
        upd = _dot_tn(v, k2)
        yield
        o_ref[0, :, vs] = o.astype(BF16)
        st[h] = state * jnp.exp(end) + upd

    chains = []
    for h in range(GLA_HB):
        chains.append(direction(qf_ref, kf_ref, vf_ref, lrf_ref, of_ref, stf, 0, False, h))
        chains.append(direction(qb_ref, kb_ref, vb_ref, lrb_ref, ob_ref, stb, 1, True, h))
    while chains:
        alive = []
        for chain in chains:
            if next(chain, chain) is not chain:
                alive.append(chain)
        chains = alive


def _gla(p3, lr3, wg_pad, bg, B, S):
    TS = GLA_BLOCK
    NB = S // TS
    HB = GLA_HB
    kq, kk, kv = COL_QG // (HB * GLA_DK), COL_KG // (HB * GLA_DK), COL_VG // (HB * GLA_DV)

    def fwd(base):
        return lambda b, h, n: (b, n, base + h)

    def bwd(base):
        return lambda b, h, n: (b, NB - 1 - n, base + h)

    def specs(m, lr_map):
        return [
            pl.BlockSpec((1, TS, HB * GLA_DK), m(kq)),
            pl.BlockSpec((1, TS, HB * GLA_DK), m(kk)),
            pl.BlockSpec((1, TS, HB * GLA_DV), m(kv)),
            pl.BlockSpec((1, TS, LR_COLS), lr_map),
        ]

    return pl.pallas_call(
        _gla_kernel,
        grid=(B, GLA_HEADS // HB, NB),
        in_specs=specs(fwd, lambda b, h, n: (b, n, 0)) + specs(bwd, lambda b, h, n: (b, NB - 1 - n, 0)) + [
            pl.BlockSpec((2, LR_COLS, HB * GLA_DK), lambda b, h, n: (0, 0, h)),
            pl.BlockSpec((2, 1, HB * GLA_DK), lambda b, h, n: (0, 0, h)),
        ],
        out_specs=[
            pl.BlockSpec((1, TS, HB * GLA_DV), lambda b, h, n: (b, n, h)),
            pl.BlockSpec((1, TS, HB * GLA_DV), lambda b, h, n: (b, NB - 1 - n, h)),
        ],
        out_shape=[jax.ShapeDtypeStruct((B, S, GLA_VAL), BF16)] * 2,
        scratch_shapes=[
            pltpu.VMEM((HB, GLA_DV, GLA_DK), F32),
            pltpu.VMEM((HB, GLA_DV, GLA_DK), F32),
        ],
        compiler_params=pltpu.CompilerParams(
            dimension_semantics=("parallel", "parallel", "arbitrary"),
            vmem_limit_bytes=V7X_VMEM_LIMIT),
        name="gla",
    )(p3, p3, p3, lr3, p3, p3, p3, lr3, wg_pad, bg)


def _sigmoid(x):
    return 1.0 / (1.0 + jnp.exp(-x))


def _merge_kernel(o0_ref, o1_ref, o2_ref, l0_ref, l1_ref, l2_ref, of_ref, ob_ref, rg_ref, ga_ref, gb_ref,
                  x_ref, wba_ref, wbg_ref, wout_ref, gnorm_ref, nffn_ref, h_ref, hn_ref,
                  oa_s, og_s, oil_s, lil_s):
    tm = x_ref.shape[0]
    for gi, (o_ref, l_ref) in enumerate(((o1_ref, l1_ref), (o2_ref, l2_ref))):
        r = o_ref.shape[1]
        for c in range(r):
            dst = pl.ds(c, tm // r, stride=r)
            lil_s[gi, dst, :] = l_ref[0, c]
            for hh in range(HEADS_PER_GROUP):
                oil_s[gi, hh, dst, :] = o_ref[0, c, :, hh * HEAD_DIM:(hh + 1) * HEAD_DIM].astype(F32)

    mls = (l0_ref[0, 0], lil_s[0], lil_s[1])
    m = jnp.maximum(jnp.maximum(mls[0], mls[1]), mls[2])
    es = [jnp.exp(ml - m) for ml in mls]
    dens = [pltpu.roll(ml, HEAD_DIM - HEADS_PER_GROUP, 1) for ml in mls]
    inv = 1.0 / (es[0] * dens[0] + es[1] * dens[1] + es[2] * dens[2])
    w0, w1, w2 = es[0] * inv, es[1] * inv, es[2] * inv
    for hh in range(HEADS_PER_GROUP):
        hs = slice(hh * HEAD_DIM, (hh + 1) * HEAD_DIM)
        comb = (w0[:, hh:hh + 1] * o0_ref[0, 0, :, hs].astype(F32)
                + w1[:, hh:hh + 1] * oil_s[0, hh]
                + w2[:, hh:hh + 1] * oil_s[1, hh])
        oa_s[:, hs] = comb.astype(BF16)
    u_a = _dot(oa_s[...], wba_ref[...])

    for h in range(GLA_HEADS):
        vs = slice(h * GLA_DV, (h + 1) * GLA_DV)
        og = of_ref[:, vs].astype(F32) + ob_ref[:, vs].astype(F32)
        ms = jnp.mean(og * og, axis=-1, keepdims=True)
        ogn = og * lax.rsqrt(ms + EPS) * gnorm_ref[...]
        rg = rg_ref[:, vs].astype(F32)
        og_s[:, vs] = (ogn * (rg * _sigmoid(rg))).astype(BF16)
    u_b = _dot(og_s[...], wbg_ref[...])

    merged = _sigmoid(ga_ref[...].astype(F32)) * u_a + _sigmoid(gb_ref[...].astype(F32)) * u_b
    h = x_ref[...] + _dot(merged.astype(BF16), wout_ref[...])
    h_ref[...] = h
    ms = jnp.mean(h * h, axis=-1, keepdims=True)
    hn_ref[...] = (h * lax.rsqrt(ms + EPS) * nffn_ref[...]).astype(BF16)


def _merge(o_groups, ml_groups, o_fwd, o_bwd, p1, p2, x, wba, wbg, wout, gnorm, nffn, B, S, *, tm=256):
    t = x.shape[0]
    per_b = S // tm
    row = lambda b, i: (b * per_b + i, 0)
    const = lambda b, i: (0, 0)

    def resident(shape):
        return pl.BlockSpec(shape, const, pipeline_mode=pl.Buffered(1))

    def split_spec(r, width):
        return pl.BlockSpec((1, r, tm // r, width), lambda b, i: (b, 0, i, 0))

    rs = [r for _, r in ATTN_GROUPS]
    return pl.pallas_call(
        _merge_kernel,
        grid=(B, per_b),
        in_specs=[split_spec(r, ATTN_OUT) for r in rs] + [split_spec(r, HEAD_DIM) for r in rs] + [
            pl.BlockSpec((tm, GLA_VAL), row),
            pl.BlockSpec((tm, GLA_VAL), row),
            pl.BlockSpec((tm, GLA_VAL), lambda b, i: (b * per_b + i, COL_RG // GLA_VAL)),
            pl.BlockSpec((tm, D_MODEL), lambda b, i: (b * per_b + i, COL_GA // D_MODEL)),
            pl.BlockSpec((tm, D_MODEL), lambda b, i: (b * per_b + i, COL_GB // D_MODEL)),
            pl.BlockSpec((tm, D_MODEL), row),
            resident((ATTN_OUT, D_MODEL)),
            resident((GLA_VAL, D_MODEL)),
            resident((D_MODEL, D_MODEL)),
            resident((1, GLA_DV)),
            resident((1, D_MODEL)),
        ],
        out_specs=[pl.BlockSpec((tm, D_MODEL), row), pl.BlockSpec((tm, D_MODEL), row)],
        out_shape=[jax.ShapeDtypeStruct((t, D_MODEL), F32), jax.ShapeDtypeStruct((t, D_MODEL), BF16)],
        scratch_shapes=[
            pltpu.VMEM((tm, ATTN_OUT), BF16),
            pltpu.VMEM((tm, GLA_VAL), BF16),
            pltpu.VMEM((2, HEADS_PER_GROUP, tm, HEAD_DIM), F32),
            pltpu.VMEM((2, tm, HEAD_DIM), F32),
        ],
        compiler_params=pltpu.CompilerParams(
            dimension_semantics=("parallel", "parallel"),
            vmem_limit_bytes=V7X_VMEM_LIMIT),
        name="merge",
    )(*o_groups, *ml_groups, o_fwd, o_bwd, p1, p2, p2, x, wba, wbg, wout, gnorm, nffn)


def _ffn_kernel(hn_ref, w1_ref, w2_ref, h_ref, y_ref):
    j = pl.program_id(1)
    a = _dot(hn_ref[...], w1_ref[...])
    a = jnp.square(jnp.maximum(a, 0.0)).astype(BF16)
    y = _dot(a, w2_ref[...])

    @pl.when(j == 0)
    def _():
        y_ref[...] = h_ref[...] + y

    @pl.when(j > 0)
    def _():
        y_ref[...] += y


def _ffn(hn, h, w1, w2, *, tm=512, tf=1024):
    t = hn.shape[0]
    return pl.pallas_call(
        _ffn_kernel,
        grid=(t // tm, D_FF // tf),
        in_specs=[
            pl.BlockSpec((tm, D_MODEL), lambda i, j: (i, 0)),
            pl.BlockSpec((D_MODEL, tf), lambda i, j: (0, j)),
            pl.BlockSpec((tf, D_MODEL), lambda i, j: (j, 0)),
            pl.BlockSpec((tm, D_MODEL), lambda i, j: (i, 0)),
        ],
        out_specs=pl.BlockSpec((tm, D_MODEL), lambda i, j: (i, 0)),
        out_shape=jax.ShapeDtypeStruct((t, D_MODEL), F32),
        compiler_params=pltpu.CompilerParams(
            dimension_semantics=("parallel", "arbitrary"),
            vmem_limit_bytes=V7X_VMEM_LIMIT),
        name="ffn",
    )(hn, w1, w2, h)


def _rope_tables(s_max):
    inv_freq = ROPE_THETA ** (-jnp.arange(0, ROT_DIM, 2, dtype=F32) / ROT_DIM)
    ang = jnp.arange(s_max, dtype=F32)[:, None] * inv_freq[None, :]
    cos, sin = jnp.cos(ang), jnp.sin(ang)
    rest = HEAD_DIM - ROT_DIM
    cos_t = jnp.concatenate([cos, cos, jnp.ones((s_max, rest), F32)], axis=1)
    sin_t = jnp.concatenate([sin, sin, jnp.zeros((s_max, rest), F32)], axis=1)
    return cos_t, sin_t


def _prepare_layer(w_in, w_gla_gate, b_gla_gate):
    w_t = jnp.swapaxes(w_in, 0, 1)
    wg_pad = jnp.zeros((2, LR_COLS, GLA_KEY), F32)
    wg_pad = wg_pad.at[0, 0:GLA_RANK].set(w_gla_gate[0].astype(F32))
    wg_pad = wg_pad.at[1, GLA_RANK:2 * GLA_RANK].set(w_gla_gate[1].astype(F32)).astype(BF16)
    bg = b_gla_gate.astype(F32).reshape(2, 1, GLA_KEY)
    return w_t, wg_pad, bg


def _layer(x3, tables, norm_mix, prepared, q_norm, k_norm, gla_norm, wba, wbg, wout, norm_ffn, w1, w2):
    B, S, _ = x3.shape
    T = B * S
    w_t, wg_pad, bg = prepared
    tn = 1024
    x = x3.reshape(T, D_MODEL)
    xn, lr = _xnorm(x, norm_mix.reshape(1, D_MODEL), w_t)
    o_groups, ml_groups = [], []
    for g in range(N_GROUPS):
        r = ATTN_GROUPS[g][1]
        a_g = _proj(xn, w_t, lambda j, g=g: j * ATTN_QKV + g * ATTN_OUT, 3, r, B, S, f"proj_attn_r{r}", tn=tn)
        o_g, ml_g = _attention_group(a_g.reshape(B, r, S // r, 3 * ATTN_OUT), tables, q_norm, k_norm, g, B, S)
        o_groups.append(o_g)
        ml_groups.append(ml_g)
    p1 = _proj(xn, w_t, lambda j: W_COL_QG + j * tn, P1_COLS // tn, 1, B, S, "proj_gla", tn=tn)
    p2 = _proj(xn, w_t, lambda j: W_COL_GATES + j * tn, 2 * D_MODEL // tn, 1, B, S, "proj_gates", tn=tn)
    o_fwd, o_bwd = _gla(p1.reshape(B, S, P1_COLS), lr.reshape(B, S, LR_COLS), wg_pad, bg, B, S)
    h, hn = _merge(o_groups, ml_groups, o_fwd.reshape(T, GLA_VAL), o_bwd.reshape(T, GLA_VAL), p1, p2, x,
                   wba, wbg, wout, gla_norm.reshape(1, GLA_DV), norm_ffn.reshape(1, D_MODEL), B, S)
    y = _ffn(hn, h, w1, w2)
    return y.reshape(B, S, D_MODEL)


def kernel(x_prompt, x_sample, norm_mix, w_in, q_norm, k_norm, w_gla_gate, b_gla_gate, gla_norm,
           w_branch_attn, w_branch_gla, w_out, norm_ffn, w_ff1, w_ff2):
    depth = w_in.shape[0]
    tables = _rope_tables(max(x_prompt.shape[1], x_sample.shape[1]))
    layers = []
    for l in range(depth):
        layers.append((
            norm_mix[l], _prepare_layer(w_in[l], w_gla_gate[l], b_gla_gate[l]),
            q_norm[l].astype(F32), k_norm[l].astype(F32), gla_norm[l].astype(F32),
            w_branch_attn[l].astype(BF16), w_branch_gla[l].astype(BF16), w_out[l].astype(BF16),
            norm_ffn[l].astype(F32), w_ff1[l].astype(BF16), w_ff2[l].astype(BF16)))
    outs = []
    for x in (x_prompt, x_sample):
        for layer in layers:
            x = _layer(x, tables, *layer)
        outs.append(x)
    return tuple(outs)
```

```python
import functools

import jax
import jax.numpy as jnp
from jax import lax
from jax.experimental import pallas as pl
from jax.experimental.pallas import tpu as pltpu

F32 = jnp.float32
BF16 = jnp.bfloat16

D_MODEL = 2048
HEAD_DIM = 128
ATTN_GROUPS = ((128, 1), (512, 4), (2048, 16))
N_GROUPS = 3
HEADS_PER_GROUP = 8
ATTN_QKV = N_GROUPS * HEADS_PER_GROUP * HEAD_DIM
ATTN_OUT = HEADS_PER_GROUP * HEAD_DIM
ROT_DIM = HEAD_DIM // 4
ROPE_THETA = 500000.0
GLA_HEADS = 4
GLA_KEY = 1024
GLA_VAL = 2048
GLA_DK = 256
GLA_DV = 512
GLA_RANK = 16
GLA_NORMALIZER = 16.0
D_FF = 4 * D_MODEL
EPS = 1e-6

W_COL_QG = 3 * ATTN_QKV
W_COL_LR = W_COL_QG + 2 * GLA_KEY + 2 * GLA_VAL
W_COL_GATES = W_COL_LR + 2 * GLA_RANK
COL_QG = 0
COL_KG = 1024
COL_VG = 2048
COL_RG = 4096
P1_COLS = 6144
COL_GA = 0
COL_GB = 2048
LR_COLS = 128

V7X_VMEM_LIMIT = 56 * 1024 * 1024
NEG = -1e30

ATTN_HALF = 64
ATTN_TQ = 128
ATTN_CHAINS = 8
GLA_BLOCK = 256
GLA_SUB = 64
GLA_HB = 4


def _dot(a, b):
    return jnp.dot(a, b, preferred_element_type=F32)


def _dot_nt(a, b):
    return lax.dot_general(a, b, (((1,), (1,)), ((), ())), preferred_element_type=F32)


def _dot_tn(a, b):
    return lax.dot_general(a, b, (((0,), (0,)), ((), ())), preferred_element_type=F32)


def _split_bf16(x):
    hi = x.astype(BF16)
    lo = (x - hi.astype(F32)).astype(BF16)
    return hi, lo


def _xnorm_kernel(x_ref, gain_ref, wlr_ref, xn_ref, lr_ref):
    x = x_ref[...]
    ms = jnp.mean(x * x, axis=-1, keepdims=True)
    xn = (x * lax.rsqrt(ms + EPS) * gain_ref[...]).astype(BF16)
    xn_ref[...] = xn
    lr_ref[...] = _dot_nt(xn, wlr_ref[...].astype(BF16))


def _xnorm(x, gain, w_t, *, tm=512):
    t = x.shape[0]
    return pl.pallas_call(
        _xnorm_kernel,
        grid=(t // tm,),
        in_specs=[
            pl.BlockSpec((tm, D_MODEL), lambda i: (i, 0)),
            pl.BlockSpec((1, D_MODEL), lambda i: (0, 0)),
            pl.BlockSpec((pl.Element(LR_COLS), pl.Element(D_MODEL)), lambda i: (W_COL_LR, 0)),
        ],
        out_specs=[
            pl.BlockSpec((tm, D_MODEL), lambda i: (i, 0)),
            pl.BlockSpec((tm, LR_COLS), lambda i: (i, 0)),
        ],
        out_shape=[
            jax.ShapeDtypeStruct((t, D_MODEL), BF16),
            jax.ShapeDtypeStruct((t, LR_COLS), F32),
        ],
        compiler_params=pltpu.CompilerParams(
            dimension_semantics=("parallel",),
            vmem_limit_bytes=V7X_VMEM_LIMIT),
        name="xnorm",
    )(x, gain, w_t)


def _proj_kernel(xn_ref, w_ref, o_ref, wb_s, *scratch, r):
    @pl.when(pl.program_id(1) == 0)
    def _():
        wb_s[...] = w_ref[...].astype(BF16)

    acc = _dot_nt(xn_ref[...], wb_s[...])
    if r == 1:
        o_ref[...] = acc.astype(BF16)
        return
    acc_s, = scratch
    tm = acc.shape[0]
    for lb in range(acc_s.shape[0]):
        ls = slice(lb * HEAD_DIM, (lb + 1) * HEAD_DIM)
        acc_s[lb] = acc[:, ls]
        for c in range(r):
            o_ref[0, c, :, ls] = acc_s[lb, pl.ds(c, tm // r, stride=r), :].astype(BF16)


def _proj(xn, w_t, row_start, ncol_blocks, r, B, S, name, *, tm=1024, tn=1024):
    t = xn.shape[0]
    per_b = S // tm
    if r == 1:
        out_spec = pl.BlockSpec((tm, tn), lambda j, i: (i, j))
        out_shape = jax.ShapeDtypeStruct((t, ncol_blocks * tn), BF16)
        scratch = []
    else:
        out_spec = pl.BlockSpec((1, r, tm // r, tn), lambda j, i: (i // per_b, 0, i % per_b, j))
        out_shape = jax.ShapeDtypeStruct((B, r, S // r, ncol_blocks * tn), BF16)
        scratch = [pltpu.VMEM((tn // HEAD_DIM, tm, HEAD_DIM), F32)]
    return pl.pallas_call(
        functools.partial(_proj_kernel, r=r),
        grid=(ncol_blocks, t // tm),
        in_specs=[
            pl.BlockSpec((tm, D_MODEL), lambda j, i: (i, 0)),
            pl.BlockSpec((pl.Element(tn), pl.Element(D_MODEL)), lambda j, i: (pl.multiple_of(row_start(j), 8), 0)),
        ],
        out_specs=out_spec,
        out_shape=out_shape,
        scratch_shapes=[pltpu.VMEM((tn, D_MODEL), BF16)] + scratch,
        compiler_params=pltpu.CompilerParams(
            dimension_semantics=("parallel", "arbitrary"),
            vmem_limit_bytes=V7X_VMEM_LIMIT),
        name=name,
    )(xn, w_t)


def _attn_kernel(q_ref, k_ref, v_ref, cos_ref, sin_ref, qg_ref, kg_ref,
                 o_ref, ml_ref, qs, ks, vs, bias_s, *, L, hb, U):
    TQ, HALF = ATTN_TQ, ATTN_HALF
    TK = TQ + 2 * HALF
    NT = L // TQ
    NI = NT // U
    hblk = pl.program_id(2)

    @pl.when(hblk == 0)
    def _():
        ml_ref[...] = jnp.zeros_like(ml_ref)

    ri = lax.broadcasted_iota(jnp.int32, (TQ, TK), 0)
    ci = lax.broadcasted_iota(jnp.int32, (TQ, TK), 1)
    d = ci - ri
    band = jnp.where(d < 0, NEG, jnp.where(d > 2 * HALF, NEG, 0.0)).astype(F32)
    first = jnp.where(ci < HALF, NEG, band)
    bias_s[0] = band
    bias_s[1] = first
    bias_s[2] = jnp.where(ci >= TQ + HALF, NEG, band)
    bias_s[3] = jnp.where(ci >= TQ + HALF, NEG, first)

    a = lax.broadcasted_iota(jnp.int32, (HEAD_DIM, HEAD_DIM), 0)
    b = lax.broadcasted_iota(jnp.int32, (HEAD_DIM, HEAD_DIM), 1)
    half = ROT_DIM // 2
    ones_m = jnp.ones((HEAD_DIM, HEAD_DIM), BF16)
    rot_m = jnp.where((b < half) & (a == b + half), -1.0,
                      jnp.where((b >= half) & (b < ROT_DIM) & (a == b - half), 1.0, 0.0)).astype(BF16)

    zpad = jnp.zeros((HALF, HEAD_DIM), BF16)
    ztail = jnp.zeros((TQ, HEAD_DIM), BF16)
    for hh in range(hb):
        ks[hh, 0:HALF, :] = zpad
        ks[hh, L + HALF:L + HALF + TQ, :] = ztail
        vs[hh, 0:HALF, 0:HEAD_DIM] = zpad
        vs[hh, L + HALF:L + HALF + TQ, 0:HEAD_DIM] = ztail
        vs[hh, :, HEAD_DIM:2 * HEAD_DIM] = jnp.ones((vs.shape[1], HEAD_DIM), BF16)

    qgain = qg_ref[...] * (HEAD_DIM ** -0.5)
    kgain = kg_ref[...]

    def norm_rope(x, gain, cos, sin):
        ssq = _dot((x * x).astype(BF16), ones_m)
        xn = x * lax.rsqrt(ssq * (1.0 / HEAD_DIM) + EPS) * gain
        return xn * cos + _dot(xn.astype(BF16), rot_m) * sin

    def prep(blk):
        if isinstance(blk, int):
            src, dst, koff = min(blk, NT - 1) * TQ, blk * TQ, blk * TQ + HALF
        else:
            src = pl.multiple_of(jnp.minimum(blk, NT - 1) * TQ, TQ)
            dst = pl.multiple_of(blk * TQ, TQ)
            koff = pl.multiple_of(dst + HALF, HALF)
        rows = pl.ds(src, TQ)
        qdst = pl.ds(dst, TQ)
        kdst = pl.ds(koff, TQ)
        cos, sin = cos_ref[rows, :], sin_ref[rows, :]
        for hh in range(hb):
            hs = slice(hh * HEAD_DIM, (hh + 1) * HEAD_DIM)
            qs[hh, qdst, :] = norm_rope(q_ref[0, 0, rows, hs].astype(F32), qgain, cos, sin).astype(BF16)
            ks[hh, kdst, :] = norm_rope(k_ref[0, 0, rows, hs].astype(F32), kgain, cos, sin).astype(BF16)
            vs[hh, kdst, 0:HEAD_DIM] = v_ref[0, 0, rows, hs]

    lane = lax.broadcasted_iota(jnp.int32, (TQ, HEAD_DIM), 1)

    def score_tiles(tt):
        for u in range(U):
            t = tt * U + u
            q0 = t * TQ if isinstance(t, int) else pl.multiple_of(t * TQ, TQ)
            qrows = pl.ds(q0, TQ)
            krows = pl.ds(q0, TK)
            bias = bias_s[jnp.where(t == 0, 1, 0) + jnp.where(t == NT - 1, 2, 0)]
            ml_tile = ml_ref[0, 0, qrows, :]
            for hh in range(hb):
                hs = slice(hh * HEAD_DIM, (hh + 1) * HEAD_DIM)
                h = hblk * hb + hh
                s = _dot_nt(qs[hh, qrows, :], ks[hh, krows, :]) + bias
                m = jnp.max(s, axis=-1, keepdims=True)
                p = jnp.exp(s - m).astype(BF16)
                acc = _dot(p, vs[hh, krows, :])
                o_ref[0, 0, qrows, hs] = acc[:, :HEAD_DIM].astype(BF16)
                ml_tile = jnp.where(lane == h, m, jnp.where(lane == HEADS_PER_GROUP + h, acc[:, HEAD_DIM:], ml_tile))
            ml_ref[0, 0, qrows, :] = ml_tile

    for blk in range(min(U + 1, NT)):
        prep(blk)

    def body(tt, carry):
        score_tiles(tt)
        for u in range(U):
            prep((tt + 1) * U + 1 + u)
        return carry

    lax.fori_loop(0, NI - 1, body, 0)
    score_tiles(NI - 1)


def _attention_group(a_g, tables, q_gain, k_gain, g, B, S):
    window, r = ATTN_GROUPS[g]
    assert window // (2 * r) == ATTN_HALF
    L = S // r
    assert L % ATTN_TQ == 0
    hb = max(1, min(HEADS_PER_GROUP, 8192 // L))
    nhb = HEADS_PER_GROUP // hb
    bw = hb * HEAD_DIM
    U = max(1, min(ATTN_CHAINS // hb, L // ATTN_TQ))
    assert (L // ATTN_TQ) % U == 0
    cos, sin = (t[:S].reshape(L, r * HEAD_DIM) for t in tables)

    def col_map(part):
        return lambda b, c, h: (b, c, 0, part * nhb + h)

    tab_spec = pl.BlockSpec((L, HEAD_DIM), lambda b, c, h: (0, c))
    gain_spec = pl.BlockSpec((1, HEAD_DIM), lambda b, c, h: (0, 0))
    return pl.pallas_call(
        functools.partial(_attn_kernel, L=L, hb=hb, U=U),
        grid=(B, r, nhb),
        in_specs=[
            pl.BlockSpec((1, 1, L, bw), col_map(0)),
            pl.BlockSpec((1, 1, L, bw), col_map(1)),
            pl.BlockSpec((1, 1, L, bw), col_map(2)),
            tab_spec, tab_spec, gain_spec, gain_spec,
        ],
        out_specs=[
            pl.BlockSpec((1, 1, L, bw), lambda b, c, h: (b, c, 0, h)),
            pl.BlockSpec((1, 1, L, HEAD_DIM), lambda b, c, h: (b, c, 0, 0)),
        ],
        out_shape=[
            jax.ShapeDtypeStruct((B, r, L, ATTN_OUT), BF16),
            jax.ShapeDtypeStruct((B, r, L, HEAD_DIM), F32),
        ],
        scratch_shapes=[
            pltpu.VMEM((hb, L + ATTN_TQ, HEAD_DIM), BF16),
            pltpu.VMEM((hb, L + ATTN_TQ + ATTN_HALF, HEAD_DIM), BF16),
            pltpu.VMEM((hb, L + ATTN_TQ + ATTN_HALF, 2 * HEAD_DIM), BF16),
            pltpu.VMEM((4, ATTN_TQ, ATTN_TQ + 2 * ATTN_HALF), F32),
        ],
        compiler_params=pltpu.CompilerParams(
            dimension_semantics=("parallel", "parallel", "arbitrary"),
            vmem_limit_bytes=V7X_VMEM_LIMIT),
        name=f"attn_g{g}",
    )(a_g, a_g, a_g, cos, sin, q_gain[g:g + 1], k_gain[g:g + 1])


def _gla_kernel(qf_ref, kf_ref, vf_ref, lrf_ref, qb_ref, kb_ref, vb_ref, lrb_ref, wg_ref, bg_ref,
                of_ref, ob_ref, stf, stb):
    C, SC = GLA_BLOCK, GLA_SUB
    NS = C // SC
    n = pl.program_id(2)

    @pl.when(n == 0)
    def _():
        stf[...] = jnp.zeros_like(stf)
        stb[...] = jnp.zeros_like(stb)

    ri = lax.broadcasted_iota(jnp.int32, (C, C), 0)
    ci = lax.broadcasted_iota(jnp.int32, (C, C), 1)

    def direction(q_ref, k_ref, v_ref, lr_ref, o_ref, st, d, backward, h):
        ks = slice(h * GLA_DK, (h + 1) * GLA_DK)
        vs = slice(h * GLA_DV, (h + 1) * GLA_DV)
        z = _dot(lr_ref[0].astype(BF16), wg_ref[d, :, ks]) + bg_ref[d, :, ks]
        yield
        softplus = jnp.maximum(-z, 0.0) + jnp.log(1.0 + jnp.exp(-jnp.abs(z)))
        g_hi, g_lo = _split_bf16(softplus * (-1.0 / GLA_NORMALIZER))
        tri = jnp.where((ci >= ri) if backward else (ri >= ci), 1.0, 0.0).astype(BF16)
        cum = _dot(tri, g_hi) + _dot(tri, g_lo)
        yield
        mid_row = SC // 2 if backward else SC // 2 - 1
        end_row = 0 if backward else C - 1
        mids = [cum[I * SC + mid_row:I * SC + mid_row + 1, :] for I in range(NS)]
        end = cum[end_row:end_row + 1, :]
        sub = [slice(I * SC, (I + 1) * SC) for I in range(NS)]
        dl = jnp.concatenate([cum[sub[I], :] - mids[I] for I in range(NS)], axis=0)
        qd = q_ref[0, :, ks].astype(F32) * (GLA_DK ** -0.5) * jnp.exp(dl)
        kd = k_ref[0, :, ks].astype(F32) * jnp.exp(-dl)
        qd_b = qd.astype(BF16)
        kd_b = kd.astype(BF16)
        att_rows = []
        for I in range(NS):
            blocks = []
            for J in range(NS):
                if (J > I) if backward else (J < I):
                    blocks.append((kd[sub[J], :] * jnp.exp(mids[I] - mids[J])).astype(BF16))
                else:
                    blocks.append(kd_b[sub[J], :])
            att_rows.append(_dot_nt(qd_b[sub[I], :], jnp.concatenate(blocks, axis=0)))
        yield
        mask = (ci > ri) if backward else (ri >= ci)
        att = jnp.where(mask, jnp.concatenate(att_rows, axis=0), 0.0).astype(BF16)
        qi = jnp.concatenate([qd[sub[I], :] * jnp.exp(mids[I]) for I in range(NS)], axis=0).astype(BF16)
        k2 = jnp.concatenate([kd[sub[I], :] * jnp.exp(end - mids[I]) for I in range(NS)], axis=0).astype(BF16)
        v = v_ref[0, :, vs]
        state = st[h]
        o = _dot(att, v) + _dot_nt(qi, state.astype(BF16))
        upd = _dot_tn(v, k2)
        yield
        o_ref[0, :, vs] = o.astype(BF16)
        st[h] = state * jnp.exp(end) + upd

    chains = []
    for h in range(GLA_HB):
        chains.append(direction(qf_ref, kf_ref, vf_ref, lrf_ref, of_ref, stf, 0, False, h))
        chains.append(direction(qb_ref, kb_ref, vb_ref, lrb_ref, ob_ref, stb, 1, True, h))
    while chains:
        alive = []
        for chain in chains:
            if next(chain, chain) is not chain:
                alive.append(chain)
        chains = alive


def _gla(p3, lr3, wg_pad, bg, B, S):
    TS = GLA_BLOCK
    NB = S // TS
    HB = GLA_HB
    kq, kk, kv = COL_QG // (HB * GLA_DK), COL_KG // (HB * GLA_DK), COL_VG // (HB * GLA_DV)

    def fwd(base):
        return lambda b, h, n: (b, n, base + h)

    def bwd(base):
        return lambda b, h, n: (b, NB - 1 - n, base + h)

    def specs(m, lr_map):
        return [
            pl.BlockSpec((1, TS, HB * GLA_DK), m(kq)),
            pl.BlockSpec((1, TS, HB * GLA_DK), m(kk)),
            pl.BlockSpec((1, TS, HB * GLA_DV), m(kv)),
            pl.BlockSpec((1, TS, LR_COLS), lr_map),
        ]

    return pl.pallas_call(
        _gla_kernel,
        grid=(B, GLA_HEADS // HB, NB),
        in_specs=specs(fwd, lambda b, h, n: (b, n, 0)) + specs(bwd, lambda b, h, n: (b, NB - 1 - n, 0)) + [
            pl.BlockSpec((2, LR_COLS, HB * GLA_DK), lambda b, h, n: (0, 0, h)),
            pl.BlockSpec((2, 1, HB * GLA_DK), lambda b, h, n: (0, 0, h)),
        ],
        out_specs=[
            pl.BlockSpec((1, TS, HB * GLA_DV), lambda b, h, n: (b, n, h)),
            pl.BlockSpec((1, TS, HB * GLA_DV), lambda b, h, n: (b, NB - 1 - n, h)),
        ],
        out_shape=[jax.ShapeDtypeStruct((B, S, GLA_VAL), BF16)] * 2,
        scratch_shapes=[
            pltpu.VMEM((HB, GLA_DV, GLA_DK), F32),
            pltpu.VMEM((HB, GLA_DV, GLA_DK), F32),
        ],
        compiler_params=pltpu.CompilerParams(
            dimension_semantics=("parallel", "parallel", "arbitrary"),
            vmem_limit_bytes=V7X_VMEM_LIMIT),
        name="gla",
    )(p3, p3, p3, lr3, p3, p3, p3, lr3, wg_pad, bg)


def _sigmoid(x):
    return 1.0 / (1.0 + jnp.exp(-x))


def _merge_kernel(o0_ref, o1_ref, o2_ref, l0_ref, l1_ref, l2_ref, of_ref, ob_ref, rg_ref, ga_ref, gb_ref,
                  x_ref, wba_ref, wbg_ref, wout_ref, gnorm_ref, nffn_ref, h_ref, hn_ref,
                  oa_s, og_s, oil_s, lil_s):
    tm = x_ref.shape[0]
    for gi, (o_ref, l_ref) in enumerate(((o1_ref, l1_ref), (o2_ref, l2_ref))):
        r = o_ref.shape[1]
        for c in range(r):
            dst = pl.ds(c, tm // r, stride=r)
            lil_s[gi, dst, :] = l_ref[0, c]
            for hh in range(HEADS_PER_GROUP):
                oil_s[gi, hh, dst, :] = o_ref[0, c, :, hh * HEAD_DIM:(hh + 1) * HEAD_DIM].astype(F32)

    mls = (l0_ref[0, 0], lil_s[0], lil_s[1])
    m = jnp.maximum(jnp.maximum(mls[0], mls[1]), mls[2])
    es = [jnp.exp(ml - m) for ml in mls]
    dens = [pltpu.roll(ml, HEAD_DIM - HEADS_PER_GROUP, 1) for ml in mls]
    inv = 1.0 / (es[0] * dens[0] + es[1] * dens[1] + es[2] * dens[2])
    w0, w1, w2 = es[0] * inv, es[1] * inv, es[2] * inv
    for hh in range(HEADS_PER_GROUP):
        hs = slice(hh * HEAD_DIM, (hh + 1) * HEAD_DIM)
        comb = (w0[:, hh:hh + 1] * o0_ref[0, 0, :, hs].astype(F32)
                + w1[:, hh:hh + 1] * oil_s[0, hh]
                + w2[:, hh:hh + 1] * oil_s[1, hh])
        oa_s[:, hs] = comb.astype(BF16)
    u_a = _dot(oa_s[...], wba_ref[...])

    for h in range(GLA_HEADS):
        vs = slice(h * GLA_DV, (h + 1) * GLA_DV)
        og = of_ref[:, vs].astype(F32) + ob_ref[:, vs].astype(F32)
        ms = jnp.mean(og * og, axis=-1, keepdims=True)
        ogn = og * lax.rsqrt(ms + EPS) * gnorm_ref[...]
        rg = rg_ref[:, vs].astype(F32)
        og_s[:, vs] = (ogn * (rg * _sigmoid(rg))).astype(BF16)
    u_b = _dot(og_s[...], wbg_ref[...])

    merged = _sigmoid(ga_ref[...].astype(F32)) * u_a + _sigmoid(gb_ref[...].astype(F32)) * u_b
    h = x_ref[...] + _dot(merged.astype(BF16), wout_ref[...])
    h_ref[...] = h
    ms = jnp.mean(h * h, axis=-1, keepdims=True)
    hn_ref[...] = (h * lax.rsqrt(ms + EPS) * nffn_ref[...]).astype(BF16)


def _merge(o_groups, ml_groups, o_fwd, o_bwd, p1, p2, x, wba, wbg, wout, gnorm, nffn, B, S, *, tm=256):
    t = x.shape[0]
    per_b = S // tm
    row = lambda b, i: (b * per_b + i, 0)
    const = lambda b, i: (0, 0)

    def resident(shape):
        return pl.BlockSpec(shape, const, pipeline_mode=pl.Buffered(1))

    def split_spec(r, width):
        return pl.BlockSpec((1, r, tm // r, width), lambda b, i: (b, 0, i, 0))

    rs = [r for _, r in ATTN_GROUPS]
    return pl.pallas_call(
        _merge_kernel,
        grid=(B, per_b),
        in_specs=[split_spec(r, ATTN_OUT) for r in rs] + [split_spec(r, HEAD_DIM) for r in rs] + [
            pl.BlockSpec((tm, GLA_VAL), row),
            pl.BlockSpec((tm, GLA_VAL), row),
            pl.BlockSpec((tm, GLA_VAL), lambda b, i: (b * per_b + i, COL_RG // GLA_VAL)),
            pl.BlockSpec((tm, D_MODEL), lambda b, i: (b * per_b + i, COL_GA // D_MODEL)),
            pl.BlockSpec((tm, D_MODEL), lambda b, i: (b * per_b + i, COL_GB // D_MODEL)),
            pl.BlockSpec((tm, D_MODEL), row),
            resident((ATTN_OUT, D_MODEL)),
            resident((GLA_VAL, D_MODEL)),
            resident((D_MODEL, D_MODEL)),
            resident((1, GLA_DV)),
            resident((1, D_MODEL)),
        ],
        out_specs=[pl.BlockSpec((tm, D_MODEL), row), pl.BlockSpec((tm, D_MODEL), row)],
        out_shape=[jax.ShapeDtypeStruct((t, D_MODEL), F32), jax.ShapeDtypeStruct((t, D_MODEL), BF16)],
        scratch_shapes=[
            pltpu.VMEM((tm, ATTN_OUT), BF16),
            pltpu.VMEM((tm, GLA_VAL), BF16),
            pltpu.VMEM((2, HEADS_PER_GROUP, tm, HEAD_DIM), F32),
            pltpu.VMEM((2, tm, HEAD_DIM), F32),
        ],
        compiler_params=pltpu.CompilerParams(
            dimension_semantics=("parallel", "parallel"),
            vmem_limit_bytes=V7X_VMEM_LIMIT),
        name="merge",
    )(*o_groups, *ml_groups, o_fwd, o_bwd, p1, p2, p2, x, wba, wbg, wout, gnorm, nffn)


def _ffn_kernel(hn_ref, w1_ref, w2_ref, h_ref, y_ref):
    @pl.when(pl.program_id(1) == 0)
    def _():
        y_ref[...] = h_ref[...]

    a = _dot(hn_ref[...], w1_ref[...])
    a = jnp.square(jnp.maximum(a, 0.0)).astype(BF16)
    y_ref[...] += _dot(a, w2_ref[...])


def _ffn(hn, h, w1, w2, *, tm=512, tf=1024):
    t = hn.shape[0]
    return pl.pallas_call(
        _ffn_kernel,
        grid=(t // tm, D_FF // tf),
        in_specs=[
            pl.BlockSpec((tm, D_MODEL), lambda i, j: (i, 0)),
            pl.BlockSpec((D_MODEL, tf), lambda i, j: (0, j)),
            pl.BlockSpec((tf, D_MODEL), lambda i, j: (j, 0)),
            pl.BlockSpec((tm, D_MODEL), lambda i, j: (i, 0)),
        ],
        out_specs=pl.BlockSpec((tm, D_MODEL), lambda i, j: (i, 0)),
        out_shape=jax.ShapeDtypeStruct((t, D_MODEL), F32),
        compiler_params=pltpu.CompilerParams(
            dimension_semantics=("parallel", "arbitrary"),
            vmem_limit_bytes=V7X_VMEM_LIMIT),
        name="ffn",
    )(hn, w1, w2, h)


def _rope_tables(s_max):
    inv_freq = ROPE_THETA ** (-jnp.arange(0, ROT_DIM, 2, dtype=F32) / ROT_DIM)
    ang = jnp.arange(s_max, dtype=F32)[:, None] * inv_freq[None, :]
    cos, sin = jnp.cos(ang), jnp.sin(ang)
    rest = HEAD_DIM - ROT_DIM
    cos_t = jnp.concatenate([cos, cos, jnp.ones((s_max, rest), F32)], axis=1)
    sin_t = jnp.concatenate([sin, sin, jnp.zeros((s_max, rest), F32)], axis=1)
    return cos_t, sin_t


def _prepare_layer(w_in, w_gla_gate, b_gla_gate):
    w_t = jnp.swapaxes(w_in, 0, 1)
    wg_pad = jnp.zeros((2, LR_COLS, GLA_KEY), F32)
    wg_pad = wg_pad.at[0, 0:GLA_RANK].set(w_gla_gate[0].astype(F32))
    wg_pad = wg_pad.at[1, GLA_RANK:2 * GLA_RANK].set(w_gla_gate[1].astype(F32)).astype(BF16)
    bg = b_gla_gate.astype(F32).reshape(2, 1, GLA_KEY)
    return w_t, wg_pad, bg


def _layer(x3, tables, norm_mix, prepared, q_norm, k_norm, gla_norm, wba, wbg, wout, norm_ffn, w1, w2):
    B, S, _ = x3.shape
    T = B * S
    w_t, wg_pad, bg = prepared
    tn = 1024
    x = x3.reshape(T, D_MODEL)
    xn, lr = _xnorm(x, norm_mix.reshape(1, D_MODEL), w_t)
    o_groups, ml_groups = [], []
    for g in range(N_GROUPS):
        r = ATTN_GROUPS[g][1]
        a_g = _proj(xn, w_t, lambda j, g=g: j * ATTN_QKV + g * ATTN_OUT, 3, r, B, S, f"proj_attn_r{r}", tn=tn)
        o_g, ml_g = _attention_group(a_g.reshape(B, r, S // r, 3 * ATTN_OUT), tables, q_norm, k_norm, g, B, S)
        o_groups.append(o_g)
        ml_groups.append(ml_g)
    p1 = _proj(xn, w_t, lambda j: W_COL_QG + j * tn, P1_COLS // tn, 1, B, S, "proj_gla", tn=tn)
    p2 = _proj(xn, w_t, lambda j: W_COL_GATES + j * tn, 2 * D_MODEL // tn, 1, B, S, "proj_gates", tn=tn)
    o_fwd, o_bwd = _gla(p1.reshape(B, S, P1_COLS), lr.reshape(B, S, LR_COLS), wg_pad, bg, B, S)
    h, hn = _merge(o_groups, ml_groups, o_fwd.reshape(T, GLA_VAL), o_bwd.reshape(T, GLA_VAL), p1, p2, x,
                   wba, wbg, wout, gla_norm.reshape(1, GLA_DV), norm_ffn.reshape(1, D_MODEL), B, S)
    y = _ffn(hn, h, w1, w2)
    return y.reshape(B, S, D_MODEL)


def kernel(x_prompt, x_sample, norm_mix, w_in, q_norm, k_norm, w_gla_gate, b_gla_gate, gla_norm,
           w_branch_attn, w_branch_gla, w_out, norm_ffn, w_ff1, w_ff2):
    depth = w_in.shape[0]
    tables = _rope_tables(max(x_prompt.shape[1], x_sample.shape[1]))
    layers = []
    for l in range(depth):
        layers.append((
            norm_mix[l], _prepare_layer(w_in[l], w_gla_gate[l], b_gla_gate[l]),
            q_norm[l].astype(F32), k_norm[l].astype(F32), gla_norm[l].astype(F32),
            w_branch_attn[l].astype(BF16), w_branch_gla[l].astype(BF16), w_out[l].astype(BF16),
            norm_ffn[l].astype(F32), w_ff1[l].astype(BF16), w_ff2[l].astype(BF16)))
    outs = []
    for x in (x_prompt, x_sample):
        for layer in layers:
            x = _layer(x, tables, *layer)
        outs.append(x)
    return tuple(outs)
```

```python
import functools

import jax
import jax.numpy as jnp
from jax import lax
from jax.experimental import pallas as pl
from jax.experimental.pallas import tpu as pltpu

F32 = jnp.float32
BF16 = jnp.bfloat16

D_MODEL = 2048
HEAD_DIM = 128
ATTN_GROUPS = ((128, 1), (512, 4), (2048, 16))
N_GROUPS = 3
HEADS_PER_GROUP = 8
ATTN_QKV = N_GROUPS * HEADS_PER_GROUP * HEAD_DIM
ATTN_OUT = HEADS_PER_GROUP * HEAD_DIM
ROT_DIM = HEAD_DIM // 4
ROPE_THETA = 500000.0
GLA_HEADS = 4
GLA_KEY = 1024
GLA_VAL = 2048
GLA_DK = 256
GLA_DV = 512
GLA_RANK = 16
GLA_NORMALIZER = 16.0
D_FF = 4 * D_MODEL
EPS = 1e-6

W_COL_QG = 3 * ATTN_QKV
W_COL_LR = W_COL_QG + 2 * GLA_KEY + 2 * GLA_VAL
W_COL_GATES = W_COL_LR + 2 * GLA_RANK
COL_QG = 0
COL_KG = 1024
COL_VG = 2048
COL_RG = 4096
P1_COLS = 6144
COL_GA = 0
COL_GB = 2048
LR_COLS = 128

V7X_VMEM_LIMIT = 56 * 1024 * 1024
NEG = -1e30

PROJ_TM = 2048
PROJ_TM_SPLIT = 1024
ATTN_HALF = 64
ATTN_TQ = 128
ATTN_CHAINS = 8
GLA_BLOCK = 256
GLA_SUB = 64
GLA_HB = 4


def _dot(a, b):
    return jnp.dot(a, b, preferred_element_type=F32)


def _dot_nt(a, b):
    return lax.dot_general(a, b, (((1,), (1,)), ((), ())), preferred_element_type=F32)


def _dot_tn(a, b):
    return lax.dot_general(a, b, (((0,), (0,)), ((), ())), preferred_element_type=F32)


def _split_bf16(x):
    hi = x.astype(BF16)
    lo = (x - hi.astype(F32)).astype(BF16)
    return hi, lo


def _xnorm_kernel(x_ref, gain_ref, wlr_ref, xn_ref, lr_ref):
    x = x_ref[...]
    ms = jnp.mean(x * x, axis=-1, keepdims=True)
    xn = (x * lax.rsqrt(ms + EPS) * gain_ref[...]).astype(BF16)
    xn_ref[...] = xn
    lr_ref[...] = _dot_nt(xn, wlr_ref[...].astype(BF16))


def _xnorm(x, gain, w_t, *, tm=512):
    t = x.shape[0]
    return pl.pallas_call(
        _xnorm_kernel,
        grid=(t // tm,),
        in_specs=[
            pl.BlockSpec((tm, D_MODEL), lambda i: (i, 0)),
            pl.BlockSpec((1, D_MODEL), lambda i: (0, 0)),
            pl.BlockSpec((pl.Element(LR_COLS), pl.Element(D_MODEL)), lambda i: (W_COL_LR, 0)),
        ],
        out_specs=[
            pl.BlockSpec((tm, D_MODEL), lambda i: (i, 0)),
            pl.BlockSpec((tm, LR_COLS), lambda i: (i, 0)),
        ],
        out_shape=[
            jax.ShapeDtypeStruct((t, D_MODEL), BF16),
            jax.ShapeDtypeStruct((t, LR_COLS), F32),
        ],
        compiler_params=pltpu.CompilerParams(
            dimension_semantics=("parallel",),
            vmem_limit_bytes=V7X_VMEM_LIMIT),
        name="xnorm",
    )(x, gain, w_t)


def _proj_kernel(xn_ref, w_ref, o_ref, wb_s, *scratch, r):
    @pl.when(pl.program_id(1) == 0)
    def _():
        wb_s[...] = w_ref[...].astype(BF16)

    acc = _dot_nt(xn_ref[...], wb_s[...])
    if r == 1:
        o_ref[...] = acc.astype(BF16)
        return
    acc_s, = scratch
    tm = acc.shape[0]
    for lb in range(acc_s.shape[0]):
        ls = slice(lb * HEAD_DIM, (lb + 1) * HEAD_DIM)
        acc_s[lb] = acc[:, ls]
        for c in range(r):
            o_ref[0, c, :, ls] = acc_s[lb, pl.ds(c, tm // r, stride=r), :].astype(BF16)


def _proj(xn, w_t, row_start, ncol_blocks, r, B, S, name, *, tm=1024, tn=1024):
    t = xn.shape[0]
    per_b = S // tm
    if r == 1:
        out_spec = pl.BlockSpec((tm, tn), lambda j, i: (i, j))
        out_shape = jax.ShapeDtypeStruct((t, ncol_blocks * tn), BF16)
        scratch = []
    else:
        out_spec = pl.BlockSpec((1, r, tm // r, tn), lambda j, i: (i // per_b, 0, i % per_b, j))
        out_shape = jax.ShapeDtypeStruct((B, r, S // r, ncol_blocks * tn), BF16)
        scratch = [pltpu.VMEM((tn // HEAD_DIM, tm, HEAD_DIM), F32)]
    return pl.pallas_call(
        functools.partial(_proj_kernel, r=r),
        grid=(ncol_blocks, t // tm),
        in_specs=[
            pl.BlockSpec((tm, D_MODEL), lambda j, i: (i, 0)),
            pl.BlockSpec((pl.Element(tn), pl.Element(D_MODEL)), lambda j, i: (pl.multiple_of(row_start(j), 8), 0)),
        ],
        out_specs=out_spec,
        out_shape=out_shape,
        scratch_shapes=[pltpu.VMEM((tn, D_MODEL), BF16)] + scratch,
        compiler_params=pltpu.CompilerParams(
            dimension_semantics=("parallel", "arbitrary"),
            vmem_limit_bytes=V7X_VMEM_LIMIT),
        name=name,
    )(xn, w_t)


def _attn_kernel(q_ref, k_ref, v_ref, cos_ref, sin_ref, qg_ref, kg_ref,
                 o_ref, ml_ref, qs, ks, vs, bias_s, *, L, hb, U):
    TQ, HALF = ATTN_TQ, ATTN_HALF
    TK = TQ + 2 * HALF
    NT = L // TQ
    NI = NT // U
    hblk = pl.program_id(2)

    @pl.when(hblk == 0)
    def _():
        ml_ref[...] = jnp.zeros_like(ml_ref)

    ri = lax.broadcasted_iota(jnp.int32, (TQ, TK), 0)
    ci = lax.broadcasted_iota(jnp.int32, (TQ, TK), 1)
    d = ci - ri
    band = jnp.where(d < 0, NEG, jnp.where(d > 2 * HALF, NEG, 0.0)).astype(F32)
    first = jnp.where(ci < HALF, NEG, band)
    bias_s[0] = band
    bias_s[1] = first
    bias_s[2] = jnp.where(ci >= TQ + HALF, NEG, band)
    bias_s[3] = jnp.where(ci >= TQ + HALF, NEG, first)

    a = lax.broadcasted_iota(jnp.int32, (HEAD_DIM, HEAD_DIM), 0)
    b = lax.broadcasted_iota(jnp.int32, (HEAD_DIM, HEAD_DIM), 1)
    half = ROT_DIM // 2
    ones_m = jnp.ones((HEAD_DIM, HEAD_DIM), BF16)
    rot_m = jnp.where((b < half) & (a == b + half), -1.0,
                      jnp.where((b >= half) & (b < ROT_DIM) & (a == b - half), 1.0, 0.0)).astype(BF16)

    zpad = jnp.zeros((HALF, HEAD_DIM), BF16)
    ztail = jnp.zeros((TQ, HEAD_DIM), BF16)
    for hh in range(hb):
        ks[hh, 0:HALF, :] = zpad
        ks[hh, L + HALF:L + HALF + TQ, :] = ztail
        vs[hh, 0:HALF, 0:HEAD_DIM] = zpad
        vs[hh, L + HALF:L + HALF + TQ, 0:HEAD_DIM] = ztail
        vs[hh, :, HEAD_DIM:2 * HEAD_DIM] = jnp.ones((vs.shape[1], HEAD_DIM), BF16)

    qgain = qg_ref[...] * (HEAD_DIM ** -0.5)
    kgain = kg_ref[...]

    def norm_rope(x, gain, cos, sin):
        ssq = _dot((x * x).astype(BF16), ones_m)
        xn = x * lax.rsqrt(ssq * (1.0 / HEAD_DIM) + EPS) * gain
        return xn * cos + _dot(xn.astype(BF16), rot_m) * sin

    def prep(blk):
        if isinstance(blk, int):
            src, dst, koff = min(blk, NT - 1) * TQ, blk * TQ, blk * TQ + HALF
        else:
            src = pl.multiple_of(jnp.minimum(blk, NT - 1) * TQ, TQ)
            dst = pl.multiple_of(blk * TQ, TQ)
            koff = pl.multiple_of(dst + HALF, HALF)
        rows = pl.ds(src, TQ)
        qdst = pl.ds(dst, TQ)
        kdst = pl.ds(koff, TQ)
        cos, sin = cos_ref[rows, :], sin_ref[rows, :]
        for hh in range(hb):
            hs = slice(hh * HEAD_DIM, (hh + 1) * HEAD_DIM)
            qs[hh, qdst, :] = norm_rope(q_ref[0, 0, rows, hs].astype(F32), qgain, cos, sin).astype(BF16)
            ks[hh, kdst, :] = norm_rope(k_ref[0, 0, rows, hs].astype(F32), kgain, cos, sin).astype(BF16)
            vs[hh, kdst, 0:HEAD_DIM] = v_ref[0, 0, rows, hs]

    lane = lax.broadcasted_iota(jnp.int32, (TQ, HEAD_DIM), 1)

    def score_tiles(tt):
        for u in range(U):
            t = tt * U + u
            q0 = t * TQ if isinstance(t, int) else pl.multiple_of(t * TQ, TQ)
            qrows = pl.ds(q0, TQ)
            krows = pl.ds(q0, TK)
            bias = bias_s[jnp.where(t == 0, 1, 0) + jnp.where(t == NT - 1, 2, 0)]
            ml_tile = ml_ref[0, 0, qrows, :]
            for hh in range(hb):
                hs = slice(hh * HEAD_DIM, (hh + 1) * HEAD_DIM)
                h = hblk * hb + hh
                s = _dot_nt(qs[hh, qrows, :], ks[hh, krows, :]) + bias
                m = jnp.max(s, axis=-1, keepdims=True)
                p = jnp.exp(s - m).astype(BF16)
                acc = _dot(p, vs[hh, krows, :])
                o_ref[0, 0, qrows, hs] = acc[:, :HEAD_DIM].astype(BF16)
                ml_tile = jnp.where(lane == h, m, jnp.where(lane == HEADS_PER_GROUP + h, acc[:, HEAD_DIM:], ml_tile))
            ml_ref[0, 0, qrows, :] = ml_tile

    for blk in range(min(U + 1, NT)):
        prep(blk)

    def body(tt, carry):
        score_tiles(tt)
        for u in range(U):
            prep((tt + 1) * U + 1 + u)
        return carry

    lax.fori_loop(0, NI - 1, body, 0)
    score_tiles(NI - 1)


def _attention_group(a_g, tables, q_gain, k_gain, g, B, S):
    window, r = ATTN_GROUPS[g]
    assert window // (2 * r) == ATTN_HALF
    L = S // r
    assert L % ATTN_TQ == 0
    hb = max(1, min(HEADS_PER_GROUP, 8192 // L))
    nhb = HEADS_PER_GROUP // hb
    bw = hb * HEAD_DIM
    U = max(1, min(ATTN_CHAINS // hb, L // ATTN_TQ))
    assert (L // ATTN_TQ) % U == 0
    cos, sin = (t[:S].reshape(L, r * HEAD_DIM) for t in tables)

    def col_map(part):
        return lambda b, c, h: (b, c, 0, part * nhb + h)

    tab_spec = pl.BlockSpec((L, HEAD_DIM), lambda b, c, h: (0, c))
    gain_spec = pl.BlockSpec((1, HEAD_DIM), lambda b, c, h: (0, 0))
    return pl.pallas_call(
        functools.partial(_attn_kernel, L=L, hb=hb, U=U),
        grid=(B, r, nhb),
        in_specs=[
            pl.BlockSpec((1, 1, L, bw), col_map(0)),
            pl.BlockSpec((1, 1, L, bw), col_map(1)),
            pl.BlockSpec((1, 1, L, bw), col_map(2)),
            tab_spec, tab_spec, gain_spec, gain_spec,
        ],
        out_specs=[
            pl.BlockSpec((1, 1, L, bw), lambda b, c, h: (b, c, 0, h)),
            pl.BlockSpec((1, 1, L, HEAD_DIM), lambda b, c, h: (b, c, 0, 0)),
        ],
        out_shape=[
            jax.ShapeDtypeStruct((B, r, L, ATTN_OUT), BF16),
            jax.ShapeDtypeStruct((B, r, L, HEAD_DIM), F32),
        ],
        scratch_shapes=[
            pltpu.VMEM((hb, L + ATTN_TQ, HEAD_DIM), BF16),
            pltpu.VMEM((hb, L + ATTN_TQ + ATTN_HALF, HEAD_DIM), BF16),
            pltpu.VMEM((hb, L + ATTN_TQ + ATTN_HALF, 2 * HEAD_DIM), BF16),
            pltpu.VMEM((4, ATTN_TQ, ATTN_TQ + 2 * ATTN_HALF), F32),
        ],
        compiler_params=pltpu.CompilerParams(
            dimension_semantics=("parallel", "parallel", "arbitrary"),
            vmem_limit_bytes=V7X_VMEM_LIMIT),
        name=f"attn_g{g}",
    )(a_g, a_g, a_g, cos, sin, q_gain[g:g + 1], k_gain[g:g + 1])


def _gla_kernel(qf_ref, kf_ref, vf_ref, lrf_ref, qb_ref, kb_ref, vb_ref, lrb_ref, wg_ref, bg_ref,
                of_ref, ob_ref, stf, stb):
    C, SC = GLA_BLOCK, GLA_SUB
    NS = C // SC
    n = pl.program_id(2)

    @pl.when(n == 0)
    def _():
        stf[...] = jnp.zeros_like(stf)
        stb[...] = jnp.zeros_like(stb)

    ri = lax.broadcasted_iota(jnp.int32, (C, C), 0)
    ci = lax.broadcasted_iota(jnp.int32, (C, C), 1)

    def direction(q_ref, k_ref, v_ref, lr_ref, o_ref, st, d, backward, h):
        ks = slice(h * GLA_DK, (h + 1) * GLA_DK)
        vs = slice(h * GLA_DV, (h + 1) * GLA_DV)
        z = _dot(lr_ref[0].astype(BF16), wg_ref[d, :, ks]) + bg_ref[d, :, ks]
        yield
        log_sig = jnp.minimum(z, 0.0) - jnp.log(1.0 + jnp.exp(-jnp.abs(z)))
        g_hi, g_lo = _split_bf16(log_sig * (1.0 / GLA_NORMALIZER))
        tri = jnp.where((ci >= ri) if backward else (ri >= ci), 1.0, 0.0).astype(BF16)
        cum = _dot(tri, g_hi) + _dot(tri, g_lo)
        yield
        mid_row = SC // 2 if backward else SC // 2 - 1
        end_row = 0 if backward else C - 1
        mids = [cum[I * SC + mid_row:I * SC + mid_row + 1, :] for I in range(NS)]
        end = cum[end_row:end_row + 1, :]
        sub = [slice(I * SC, (I + 1) * SC) for I in range(NS)]
        dl = jnp.concatenate([cum[sub[I], :] - mids[I] for I in range(NS)], axis=0)
        qd = q_ref[0, :, ks].astype(F32) * (GLA_DK ** -0.5) * jnp.exp(dl)
        kd = k_ref[0, :, ks].astype(F32) * jnp.exp(-dl)
        qd_b = qd.astype(BF16)
        kd_b = kd.astype(BF16)
        att_rows = []
        for I in range(NS):
            blocks = []
            for J in range(NS):
                if (J > I) if backward else (J < I):
                    blocks.append((kd[sub[J], :] * jnp.exp(mids[I] - mids[J])).astype(BF16))
                else:
                    blocks.append(kd_b[sub[J], :])
            att_rows.append(_dot_nt(qd_b[sub[I], :], jnp.concatenate(blocks, axis=0)))
        yield
        mask = (ci > ri) if backward else (ri >= ci)
        att = jnp.where(mask, jnp.concatenate(att_rows, axis=0), 0.0).astype(BF16)
        qi = jnp.concatenate([qd[sub[I], :] * jnp.exp(mids[I]) for I in range(NS)], axis=0).astype(BF16)
        k2 = jnp.concatenate([kd[sub[I], :] * jnp.exp(end - mids[I]) for I in range(NS)], axis=0).astype(BF16)
        v = v_ref[0, :, vs]
        state = st[h]
        o = _dot(att, v) + _dot_nt(qi, state.astype(BF16))
        upd = _dot_tn(v, k2)
        yield
        o_ref[0, :, vs] = o.astype(BF16)
        st[h] = state * jnp.exp(end) + upd

    chains = []
    for h in range(GLA_HB):
        chains.append(direction(qf_ref, kf_ref, vf_ref, lrf_ref, of_ref, stf, 0, False, h))
        chains.append(direction(qb_ref, kb_ref, vb_ref, lrb_ref, ob_ref, stb, 1, True, h))
    while chains:
        alive = []
        for chain in chains:
            if next(chain, chain) is not chain:
                alive.append(chain)
        chains = alive


def _gla(p3, lr3, wg_pad, bg, B, S):
    TS = GLA_BLOCK
    NB = S // TS
    HB = GLA_HB
    kq, kk, kv = COL_QG // (HB * GLA_DK), COL_KG // (HB * GLA_DK), COL_VG // (HB * GLA_DV)

    def fwd(base):
        return lambda b, h, n: (b, n, base + h)

    def bwd(base):
        return lambda b, h, n: (b, NB - 1 - n, base + h)

    def specs(m, lr_map):
        return [
            pl.BlockSpec((1, TS, HB * GLA_DK), m(kq)),
            pl.BlockSpec((1, TS, HB * GLA_DK), m(kk)),
            pl.BlockSpec((1, TS, HB * GLA_DV), m(kv)),
            pl.BlockSpec((1, TS, LR_COLS), lr_map),
        ]

    return pl.pallas_call(
        _gla_kernel,
        grid=(B, GLA_HEADS // HB, NB),
        in_specs=specs(fwd, lambda b, h, n: (b, n, 0)) + specs(bwd, lambda b, h, n: (b, NB - 1 - n, 0)) + [
            pl.BlockSpec((2, LR_COLS, HB * GLA_DK), lambda b, h, n: (0, 0, h)),
            pl.BlockSpec((2, 1, HB * GLA_DK), lambda b, h, n: (0, 0, h)),
        ],
        out_specs=[
            pl.BlockSpec((1, TS, HB * GLA_DV), lambda b, h, n: (b, n, h)),
            pl.BlockSpec((1, TS, HB * GLA_DV), lambda b, h, n: (b, NB - 1 - n, h)),
        ],
        out_shape=[jax.ShapeDtypeStruct((B, S, GLA_VAL), BF16)] * 2,
        scratch_shapes=[
            pltpu.VMEM((HB, GLA_DV, GLA_DK), F32),
            pltpu.VMEM((HB, GLA_DV, GLA_DK), F32),
        ],
        compiler_params=pltpu.CompilerParams(
            dimension_semantics=("parallel", "parallel", "arbitrary"),
            vmem_limit_bytes=V7X_VMEM_LIMIT),
        name="gla",
    )(p3, p3, p3, lr3, p3, p3, p3, lr3, wg_pad, bg)


def _sigmoid(x):
    return 1.0 / (1.0 + jnp.exp(-x))


def _merge_kernel(o0_ref, o1_ref, o2_ref, l0_ref, l1_ref, l2_ref, of_ref, ob_ref, rg_ref, ga_ref, gb_ref,
                  x_ref, wba_ref, wbg_ref, wout_ref, gnorm_ref, nffn_ref, h_ref, hn_ref,
                  oa_s, og_s, oil_s, lil_s):
    tm = x_ref.shape[0]
    for gi, (o_ref, l_ref) in enumerate(((o1_ref, l1_ref), (o2_ref, l2_ref))):
        r = o_ref.shape[1]
        for c in range(r):
            dst = pl.ds(c, tm // r, stride=r)
            lil_s[gi, dst, :] = l_ref[0, c]
            for hh in range(HEADS_PER_GROUP):
                oil_s[gi, hh, dst, :] = o_ref[0, c, :, hh * HEAD_DIM:(hh + 1) * HEAD_DIM].astype(F32)

    mls = (l0_ref[0, 0], lil_s[0], lil_s[1])
    m = jnp.maximum(jnp.maximum(mls[0], mls[1]), mls[2])
    es = [jnp.exp(ml - m) for ml in mls]
    dens = [pltpu.roll(ml, HEAD_DIM - HEADS_PER_GROUP, 1) for ml in mls]
    inv = 1.0 / (es[0] * dens[0] + es[1] * dens[1] + es[2] * dens[2])
    w0, w1, w2 = es[0] * inv, es[1] * inv, es[2] * inv
    for hh in range(HEADS_PER_GROUP):
        hs = slice(hh * HEAD_DIM, (hh + 1) * HEAD_DIM)
        comb = (w0[:, hh:hh + 1] * o0_ref[0, 0, :, hs].astype(F32)
                + w1[:, hh:hh + 1] * oil_s[0, hh]
                + w2[:, hh:hh + 1] * oil_s[1, hh])
        oa_s[:, hs] = comb.astype(BF16)
    u_a = _dot(oa_s[...], wba_ref[...])

    for h in range(GLA_HEADS):
        vs = slice(h * GLA_DV, (h + 1) * GLA_DV)
        og = of_ref[:, vs].astype(F32) + ob_ref[:, vs].astype(F32)
        ms = jnp.mean(og * og, axis=-1, keepdims=True)
        ogn = og * lax.rsqrt(ms + EPS) * gnorm_ref[...]
        rg = rg_ref[:, vs].astype(F32)
        og_s[:, vs] = (ogn * (rg * _sigmoid(rg))).astype(BF16)
    u_b = _dot(og_s[...], wbg_ref[...])

    merged = _sigmoid(ga_ref[...].astype(F32)) * u_a + _sigmoid(gb_ref[...].astype(F32)) * u_b
    h = x_ref[...] + _dot(merged.astype(BF16), wout_ref[...])
    h_ref[...] = h
    ms = jnp.mean(h * h, axis=-1, keepdims=True)
    hn_ref[...] = (h * lax.rsqrt(ms + EPS) * nffn_ref[...]).astype(BF16)


def _merge(o_groups, ml_groups, o_fwd, o_bwd, p1, p2, x, wba, wbg, wout, gnorm, nffn, B, S, *, tm=256):
    t = x.shape[0]
    per_b = S // tm
    row = lambda b, i: (b * per_b + i, 0)
    const = lambda b, i: (0, 0)

    def resident(shape):
        return pl.BlockSpec(shape, const, pipeline_mode=pl.Buffered(1))

    def split_spec(r, width):
        return pl.BlockSpec((1, r, tm // r, width), lambda b, i: (b, 0, i, 0))

    rs = [r for _, r in ATTN_GROUPS]
    return pl.pallas_call(
        _merge_kernel,
        grid=(B, per_b),
        in_specs=[split_spec(r, ATTN_OUT) for r in rs] + [split_spec(r, HEAD_DIM) for r in rs] + [
            pl.BlockSpec((tm, GLA_VAL), row),
            pl.BlockSpec((tm, GLA_VAL), row),
            pl.BlockSpec((tm, GLA_VAL), lambda b, i: (b * per_b + i, COL_RG // GLA_VAL)),
            pl.BlockSpec((tm, D_MODEL), lambda b, i: (b * per_b + i, COL_GA // D_MODEL)),
            pl.BlockSpec((tm, D_MODEL), lambda b, i: (b * per_b + i, COL_GB // D_MODEL)),
            pl.BlockSpec((tm, D_MODEL), row),
            resident((ATTN_OUT, D_MODEL)),
            resident((GLA_VAL, D_MODEL)),
            resident((D_MODEL, D_MODEL)),
            resident((1, GLA_DV)),
            resident((1, D_MODEL)),
        ],
        out_specs=[pl.BlockSpec((tm, D_MODEL), row), pl.BlockSpec((tm, D_MODEL), row)],
        out_shape=[jax.ShapeDtypeStruct((t, D_MODEL), F32), jax.ShapeDtypeStruct((t, D_MODEL), BF16)],
        scratch_shapes=[
            pltpu.VMEM((tm, ATTN_OUT), BF16),
            pltpu.VMEM((tm, GLA_VAL), BF16),
            pltpu.VMEM((2, HEADS_PER_GROUP, tm, HEAD_DIM), F32),
            pltpu.VMEM((2, tm, HEAD_DIM), F32),
        ],
        compiler_params=pltpu.CompilerParams(
            dimension_semantics=("parallel", "parallel"),
            vmem_limit_bytes=V7X_VMEM_LIMIT),
        name="merge",
    )(*o_groups, *ml_groups, o_fwd, o_bwd, p1, p2, p2, x, wba, wbg, wout, gnorm, nffn)


def _ffn_kernel(hn_ref, w1_ref, w2_ref, h_ref, y_ref):
    @pl.when(pl.program_id(1) == 0)
    def _():
        y_ref[...] = h_ref[...]

    a = _dot(hn_ref[...], w1_ref[...])
    a = jnp.square(jnp.maximum(a, 0.0)).astype(BF16)
    y_ref[...] += _dot(a, w2_ref[...])


def _ffn(hn, h, w1, w2, *, tm=512, tf=1024):
    t = hn.shape[0]
    return pl.pallas_call(
        _ffn_kernel,
        grid=(t // tm, D_FF // tf),
        in_specs=[
            pl.BlockSpec((tm, D_MODEL), lambda i, j: (i, 0)),
            pl.BlockSpec((D_MODEL, tf), lambda i, j: (0, j)),
            pl.BlockSpec((tf, D_MODEL), lambda i, j: (j, 0)),
            pl.BlockSpec((tm, D_MODEL), lambda i, j: (i, 0)),
        ],
        out_specs=pl.BlockSpec((tm, D_MODEL), lambda i, j: (i, 0)),
        out_shape=jax.ShapeDtypeStruct((t, D_MODEL), F32),
        compiler_params=pltpu.CompilerParams(
            dimension_semantics=("parallel", "arbitrary"),
            vmem_limit_bytes=V7X_VMEM_LIMIT),
        name="ffn",
    )(hn, w1, w2, h)


def _rope_tables(s_max):
    inv_freq = ROPE_THETA ** (-jnp.arange(0, ROT_DIM, 2, dtype=F32) / ROT_DIM)
    ang = jnp.arange(s_max, dtype=F32)[:, None] * inv_freq[None, :]
    cos, sin = jnp.cos(ang), jnp.sin(ang)
    rest = HEAD_DIM - ROT_DIM
    cos_t = jnp.concatenate([cos, cos, jnp.ones((s_max, rest), F32)], axis=1)
    sin_t = jnp.concatenate([sin, sin, jnp.zeros((s_max, rest), F32)], axis=1)
    return cos_t, sin_t


def _prepare_layer(w_in, w_gla_gate, b_gla_gate):
    w_t = jnp.swapaxes(w_in, 0, 1)
    wg_pad = jnp.zeros((2, LR_COLS, GLA_KEY), F32)
    wg_pad = wg_pad.at[0, 0:GLA_RANK].set(w_gla_gate[0].astype(F32))
    wg_pad = wg_pad.at[1, GLA_RANK:2 * GLA_RANK].set(w_gla_gate[1].astype(F32)).astype(BF16)
    bg = b_gla_gate.astype(F32).reshape(2, 1, GLA_KEY)
    return w_t, wg_pad, bg


def _layer(x3, tables, norm_mix, prepared, q_norm, k_norm, gla_norm, wba, wbg, wout, norm_ffn, w1, w2):
    B, S, _ = x3.shape
    T = B * S
    w_t, wg_pad, bg = prepared
    tn = 1024
    x = x3.reshape(T, D_MODEL)
    xn, lr = _xnorm(x, norm_mix.reshape(1, D_MODEL), w_t)
    o_groups, ml_groups = [], []
    for g in range(N_GROUPS):
        r = ATTN_GROUPS[g][1]
        a_g = _proj(xn, w_t, lambda j, g=g: j * ATTN_QKV + g * ATTN_OUT, 3, r, B, S, f"proj_attn_r{r}", tn=tn,
                    tm=PROJ_TM if r == 1 else PROJ_TM_SPLIT)
        o_g, ml_g = _attention_group(a_g.reshape(B, r, S // r, 3 * ATTN_OUT), tables, q_norm, k_norm, g, B, S)
        o_groups.append(o_g)
        ml_groups.append(ml_g)
    p1 = _proj(xn, w_t, lambda j: W_COL_QG + j * tn, P1_COLS // tn, 1, B, S, "proj_gla", tn=tn, tm=PROJ_TM)
    p2 = _proj(xn, w_t, lambda j: W_COL_GATES + j * tn, 2 * D_MODEL // tn, 1, B, S, "proj_gates", tn=tn, tm=PROJ_TM)
    o_fwd, o_bwd = _gla(p1.reshape(B, S, P1_COLS), lr.reshape(B, S, LR_COLS), wg_pad, bg, B, S)
    h, hn = _merge(o_groups, ml_groups, o_fwd.reshape(T, GLA_VAL), o_bwd.reshape(T, GLA_VAL), p1, p2, x,
                   wba, wbg, wout, gla_norm.reshape(1, GLA_DV), norm_ffn.reshape(1, D_MODEL), B, S)
    y = _ffn(hn, h, w1, w2)
    return y.reshape(B, S, D_MODEL)


def kernel(x_prompt, x_sample, norm_mix, w_in, q_norm, k_norm, w_gla_gate, b_gla_gate, gla_norm,
           w_branch_attn, w_branch_gla, w_out, norm_ffn, w_ff1, w_ff2):
    depth = w_in.shape[0]
    tables = _rope_tables(max(x_prompt.shape[1], x_sample.shape[1]))
    layers = []
    for l in range(depth):
        layers.append((
            norm_mix[l], _prepare_layer(w_in[l], w_gla_gate[l], b_gla_gate[l]),
            q_norm[l].astype(F32), k_norm[l].astype(F32), gla_norm[l].astype(F32),
            w_branch_attn[l].astype(BF16), w_branch_gla[l].astype(BF16), w_out[l].astype(BF16),
            norm_ffn[l].astype(F32), w_ff1[l].astype(BF16), w_ff2[l].astype(BF16)))
    outs = []
    for x in (x_prompt, x_sample):
        for layer in layers:
            x = _layer(x, tables, *layer)
        outs.append(x)
    return tuple(outs)
```

```python
import functools

import jax
import jax.numpy as jnp
from jax import lax
from jax.experimental import pallas as pl
from jax.experimental.pallas import tpu as pltpu

F32 = jnp.float32
BF16 = jnp.bfloat16

D_MODEL = 2048
HEAD_DIM = 128
ATTN_GROUPS = ((128, 1), (512, 4), (2048, 16))
N_GROUPS = 3
HEADS_PER_GROUP = 8
ATTN_QKV = N_GROUPS * HEADS_PER_GROUP * HEAD_DIM
ATTN_OUT = HEADS_PER_GROUP * HEAD_DIM
ROT_DIM = HEAD_DIM // 4
ROPE_THETA = 500000.0
GLA_HEADS = 4
GLA_KEY = 1024
GLA_VAL = 2048
GLA_DK = 256
GLA_DV = 512
GLA_RANK = 16
GLA_NORMALIZER = 16.0
D_FF = 4 * D_MODEL
EPS = 1e-6

W_COL_QG = 3 * ATTN_QKV
W_COL_LR = W_COL_QG + 2 * GLA_KEY + 2 * GLA_VAL
W_COL_GATES = W_COL_LR + 2 * GLA_RANK
COL_QG = 0
COL_KG = 1024
COL_VG = 2048
COL_RG = 4096
P1_COLS = 6144
COL_GA = 0
COL_GB = 2048
LR_COLS = 128

V7X_VMEM_LIMIT = 56 * 1024 * 1024
NEG = -1e30

PROJ_TM = 2048
PROJ_TM_SPLIT = 1024
PERM_ROWS = 256
ATTN_HALF = 64
ATTN_TQ = 128
ATTN_CHAINS = 8
GLA_BLOCK = 256
GLA_SUB = 64
GLA_HB = 4


def _dot(a, b):
    return jnp.dot(a, b, preferred_element_type=F32)


def _dot_nt(a, b):
    return lax.dot_general(a, b, (((1,), (1,)), ((), ())), preferred_element_type=F32)


def _dot_tn(a, b):
    return lax.dot_general(a, b, (((0,), (0,)), ((), ())), preferred_element_type=F32)


def _split_bf16(x):
    hi = x.astype(BF16)
    lo = (x - hi.astype(F32)).astype(BF16)
    return hi, lo


def _xnorm_kernel(x_ref, gain_ref, wlr_ref, xn_ref, lr_ref):
    x = x_ref[...]
    ms = jnp.mean(x * x, axis=-1, keepdims=True)
    xn = (x * lax.rsqrt(ms + EPS) * gain_ref[...]).astype(BF16)
    xn_ref[...] = xn
    lr_ref[...] = _dot_nt(xn, wlr_ref[...].astype(BF16))


def _xnorm(x, gain, w_t, *, tm=512):
    t = x.shape[0]
    return pl.pallas_call(
        _xnorm_kernel,
        grid=(t // tm,),
        in_specs=[
            pl.BlockSpec((tm, D_MODEL), lambda i: (i, 0)),
            pl.BlockSpec((1, D_MODEL), lambda i: (0, 0)),
            pl.BlockSpec((pl.Element(LR_COLS), pl.Element(D_MODEL)), lambda i: (W_COL_LR, 0)),
        ],
        out_specs=[
            pl.BlockSpec((tm, D_MODEL), lambda i: (i, 0)),
            pl.BlockSpec((tm, LR_COLS), lambda i: (i, 0)),
        ],
        out_shape=[
            jax.ShapeDtypeStruct((t, D_MODEL), BF16),
            jax.ShapeDtypeStruct((t, LR_COLS), F32),
        ],
        compiler_params=pltpu.CompilerParams(
            dimension_semantics=("parallel",),
            vmem_limit_bytes=V7X_VMEM_LIMIT),
        name="xnorm",
    )(x, gain, w_t)


def _proj_kernel(xn_ref, w_ref, o_ref, wb_s, *, r):
    @pl.when(pl.program_id(1) == 0)
    def _():
        wb_s[...] = w_ref[...].astype(BF16)

    acc = _dot_nt(xn_ref[...], wb_s[...]).astype(BF16)
    if r == 1:
        o_ref[...] = acc
        return
    n = PERM_ROWS // r
    dst = lax.broadcasted_iota(jnp.int32, (PERM_ROWS, PERM_ROWS), 0)
    src = lax.broadcasted_iota(jnp.int32, (PERM_ROWS, PERM_ROWS), 1)
    perm = jnp.where(src == (dst % n) * r + dst // n, 1.0, 0.0).astype(BF16)
    for g in range(acc.shape[0] // PERM_ROWS):
        grouped = _dot(perm, acc[g * PERM_ROWS:(g + 1) * PERM_ROWS, :]).astype(BF16)
        for c in range(r):
            o_ref[0, c, g * n:(g + 1) * n, :] = grouped[c * n:(c + 1) * n, :]


def _proj(xn, w_t, row_start, ncol_blocks, r, B, S, name, *, tm=1024, tn=1024):
    t = xn.shape[0]
    per_b = S // tm
    if r == 1:
        out_spec = pl.BlockSpec((tm, tn), lambda j, i: (i, j))
        out_shape = jax.ShapeDtypeStruct((t, ncol_blocks * tn), BF16)
    else:
        out_spec = pl.BlockSpec((1, r, tm // r, tn), lambda j, i: (i // per_b, 0, i % per_b, j))
        out_shape = jax.ShapeDtypeStruct((B, r, S // r, ncol_blocks * tn), BF16)
    return pl.pallas_call(
        functools.partial(_proj_kernel, r=r),
        grid=(ncol_blocks, t // tm),
        in_specs=[
            pl.BlockSpec((tm, D_MODEL), lambda j, i: (i, 0)),
            pl.BlockSpec((pl.Element(tn), pl.Element(D_MODEL)), lambda j, i: (pl.multiple_of(row_start(j), 8), 0)),
        ],
        out_specs=out_spec,
        out_shape=out_shape,
        scratch_shapes=[pltpu.VMEM((tn, D_MODEL), BF16)],
        compiler_params=pltpu.CompilerParams(
            dimension_semantics=("parallel", "arbitrary"),
            vmem_limit_bytes=V7X_VMEM_LIMIT),
        name=name,
    )(xn, w_t)


def _attn_kernel(q_ref, k_ref, v_ref, cos_ref, sin_ref, qg_ref, kg_ref,
                 o_ref, ml_ref, qs, ks, vs, bias_s, *, L, hb, U):
    TQ, HALF = ATTN_TQ, ATTN_HALF
    TK = TQ + 2 * HALF
    NT = L // TQ
    NI = NT // U
    hblk = pl.program_id(2)

    @pl.when(hblk == 0)
    def _():
        ml_ref[...] = jnp.zeros_like(ml_ref)

    ri = lax.broadcasted_iota(jnp.int32, (TQ, TK), 0)
    ci = lax.broadcasted_iota(jnp.int32, (TQ, TK), 1)
    d = ci - ri
    band = jnp.where(d < 0, NEG, jnp.where(d > 2 * HALF, NEG, 0.0)).astype(F32)
    first = jnp.where(ci < HALF, NEG, band)
    bias_s[0] = band
    bias_s[1] = first
    bias_s[2] = jnp.where(ci >= TQ + HALF, NEG, band)
    bias_s[3] = jnp.where(ci >= TQ + HALF, NEG, first)

    a = lax.broadcasted_iota(jnp.int32, (HEAD_DIM, HEAD_DIM), 0)
    b = lax.broadcasted_iota(jnp.int32, (HEAD_DIM, HEAD_DIM), 1)
    half = ROT_DIM // 2
    ones_m = jnp.ones((HEAD_DIM, HEAD_DIM), BF16)
    rot_m = jnp.where((b < half) & (a == b + half), -1.0,
                      jnp.where((b >= half) & (b < ROT_DIM) & (a == b - half), 1.0, 0.0)).astype(BF16)

    zpad = jnp.zeros((HALF, HEAD_DIM), BF16)
    ztail = jnp.zeros((TQ, HEAD_DIM), BF16)
    for hh in range(hb):
        ks[hh, 0:HALF, :] = zpad
        ks[hh, L + HALF:L + HALF + TQ, :] = ztail
        vs[hh, 0:HALF, 0:HEAD_DIM] = zpad
        vs[hh, L + HALF:L + HALF + TQ, 0:HEAD_DIM] = ztail
        vs[hh, :, HEAD_DIM:2 * HEAD_DIM] = jnp.ones((vs.shape[1], HEAD_DIM), BF16)

    qgain = qg_ref[...] * (HEAD_DIM ** -0.5)
    kgain = kg_ref[...]

    def norm_rope(x, gain, cos, sin):
        ssq = _dot((x * x).astype(BF16), ones_m)
        xn = x * lax.rsqrt(ssq * (1.0 / HEAD_DIM) + EPS) * gain
        return xn * cos + _dot(xn.astype(BF16), rot_m) * sin

    def prep(blk):
        if isinstance(blk, int):
            src, dst, koff = min(blk, NT - 1) * TQ, blk * TQ, blk * TQ + HALF
        else:
            src = pl.multiple_of(jnp.minimum(blk, NT - 1) * TQ, TQ)
            dst = pl.multiple_of(blk * TQ, TQ)
            koff = pl.multiple_of(dst + HALF, HALF)
        rows = pl.ds(src, TQ)
        qdst = pl.ds(dst, TQ)
        kdst = pl.ds(koff, TQ)
        cos, sin = cos_ref[rows, :], sin_ref[rows, :]
        for hh in range(hb):
            hs = slice(hh * HEAD_DIM, (hh + 1) * HEAD_DIM)
            qs[hh, qdst, :] = norm_rope(q_ref[0, 0, rows, hs].astype(F32), qgain, cos, sin).astype(BF16)
            ks[hh, kdst, :] = norm_rope(k_ref[0, 0, rows, hs].astype(F32), kgain, cos, sin).astype(BF16)
            vs[hh, kdst, 0:HEAD_DIM] = v_ref[0, 0, rows, hs]

    lane = lax.broadcasted_iota(jnp.int32, (TQ, HEAD_DIM), 1)

    def score_tiles(tt):
        for u in range(U):
            t = tt * U + u
            q0 = t * TQ if isinstance(t, int) else pl.multiple_of(t * TQ, TQ)
            qrows = pl.ds(q0, TQ)
            krows = pl.ds(q0, TK)
            bias = bias_s[jnp.where(t == 0, 1, 0) + jnp.where(t == NT - 1, 2, 0)]
            ml_tile = ml_ref[0, 0, qrows, :]
            for hh in range(hb):
                hs = slice(hh * HEAD_DIM, (hh + 1) * HEAD_DIM)
                h = hblk * hb + hh
                s = _dot_nt(qs[hh, qrows, :], ks[hh, krows, :]) + bias
                m = jnp.max(s, axis=-1, keepdims=True)
                p = jnp.exp(s - m).astype(BF16)
                acc = _dot(p, vs[hh, krows, :])
                o_ref[0, 0, qrows, hs] = acc[:, :HEAD_DIM].astype(BF16)
                ml_tile = jnp.where(lane == h, m, jnp.where(lane == HEADS_PER_GROUP + h, acc[:, HEAD_DIM:], ml_tile))
            ml_ref[0, 0, qrows, :] = ml_tile

    for blk in range(min(U + 1, NT)):
        prep(blk)

    def body(tt, carry):
        score_tiles(tt)
        for u in range(U):
            prep((tt + 1) * U + 1 + u)
        return carry

    lax.fori_loop(0, NI - 1, body, 0)
    score_tiles(NI - 1)


def _attention_group(a_g, tables, q_gain, k_gain, g, B, S):
    window, r = ATTN_GROUPS[g]
    assert window // (2 * r) == ATTN_HALF
    L = S // r
    assert L % ATTN_TQ == 0
    hb = max(1, min(HEADS_PER_GROUP, 8192 // L))
    nhb = HEADS_PER_GROUP // hb
    bw = hb * HEAD_DIM
    U = max(1, min(ATTN_CHAINS // hb, L // ATTN_TQ))
    assert (L // ATTN_TQ) % U == 0
    cos, sin = (t[:S].reshape(L, r * HEAD_DIM) for t in tables)

    def col_map(part):
        return lambda b, c, h: (b, c, 0, part * nhb + h)

    tab_spec = pl.BlockSpec((L, HEAD_DIM), lambda b, c, h: (0, c))
    gain_spec = pl.BlockSpec((1, HEAD_DIM), lambda b, c, h: (0, 0))
    return pl.pallas_call(
        functools.partial(_attn_kernel, L=L, hb=hb, U=U),
        grid=(B, r, nhb),
        in_specs=[
            pl.BlockSpec((1, 1, L, bw), col_map(0)),
            pl.BlockSpec((1, 1, L, bw), col_map(1)),
            pl.BlockSpec((1, 1, L, bw), col_map(2)),
            tab_spec, tab_spec, gain_spec, gain_spec,
        ],
        out_specs=[
            pl.BlockSpec((1, 1, L, bw), lambda b, c, h: (b, c, 0, h)),
            pl.BlockSpec((1, 1, L, HEAD_DIM), lambda b, c, h: (b, c, 0, 0)),
        ],
        out_shape=[
            jax.ShapeDtypeStruct((B, r, L, ATTN_OUT), BF16),
            jax.ShapeDtypeStruct((B, r, L, HEAD_DIM), F32),
        ],
        scratch_shapes=[
            pltpu.VMEM((hb, L + ATTN_TQ, HEAD_DIM), BF16),
            pltpu.VMEM((hb, L + ATTN_TQ + ATTN_HALF, HEAD_DIM), BF16),
            pltpu.VMEM((hb, L + ATTN_TQ + ATTN_HALF, 2 * HEAD_DIM), BF16),
            pltpu.VMEM((4, ATTN_TQ, ATTN_TQ + 2 * ATTN_HALF), F32),
        ],
        compiler_params=pltpu.CompilerParams(
            dimension_semantics=("parallel", "parallel", "arbitrary"),
            vmem_limit_bytes=V7X_VMEM_LIMIT),
        name=f"attn_g{g}",
    )(a_g, a_g, a_g, cos, sin, q_gain[g:g + 1], k_gain[g:g + 1])


def _gla_kernel(qf_ref, kf_ref, vf_ref, lrf_ref, qb_ref, kb_ref, vb_ref, lrb_ref, wg_ref, bg_ref,
                of_ref, ob_ref, stf, stb):
    C, SC = GLA_BLOCK, GLA_SUB
    NS = C // SC
    n = pl.program_id(2)

    @pl.when(n == 0)
    def _():
        stf[...] = jnp.zeros_like(stf)
        stb[...] = jnp.zeros_like(stb)

    ri = lax.broadcasted_iota(jnp.int32, (C, C), 0)
    ci = lax.broadcasted_iota(jnp.int32, (C, C), 1)

    def direction(q_ref, k_ref, v_ref, lr_ref, o_ref, st, d, backward, h):
        ks = slice(h * GLA_DK, (h + 1) * GLA_DK)
        vs = slice(h * GLA_DV, (h + 1) * GLA_DV)
        z = _dot(lr_ref[0].astype(BF16), wg_ref[d, :, ks]) + bg_ref[d, :, ks]
        yield
        log_sig = jnp.minimum(z, 0.0) - jnp.log(1.0 + jnp.exp(-jnp.abs(z)))
        g_hi, g_lo = _split_bf16(log_sig * (1.0 / GLA_NORMALIZER))
        tri = jnp.where((ci >= ri) if backward else (ri >= ci), 1.0, 0.0).astype(BF16)
        cum = _dot(tri, g_hi) + _dot(tri, g_lo)
        yield
        mid_row = SC // 2 if backward else SC // 2 - 1
        end_row = 0 if backward else C - 1
        mids = [cum[I * SC + mid_row:I * SC + mid_row + 1, :] for I in range(NS)]
        end = cum[end_row:end_row + 1, :]
        sub = [slice(I * SC, (I + 1) * SC) for I in range(NS)]
        dl = jnp.concatenate([cum[sub[I], :] - mids[I] for I in range(NS)], axis=0)
        qd = q_ref[0, :, ks].astype(F32) * (GLA_DK ** -0.5) * jnp.exp(dl)
        kd = k_ref[0, :, ks].astype(F32) * jnp.exp(-dl)
        qd_b = qd.astype(BF16)
        kd_b = kd.astype(BF16)
        att_rows = []
        for I in range(NS):
            blocks = []
            for J in range(NS):
                if (J > I) if backward else (J < I):
                    blocks.append((kd[sub[J], :] * jnp.exp(mids[I] - mids[J])).astype(BF16))
                else:
                    blocks.append(kd_b[sub[J], :])
            att_rows.append(_dot_nt(qd_b[sub[I], :], jnp.concatenate(blocks, axis=0)))
        yield
        mask = (ci > ri) if backward else (ri >= ci)
        att = jnp.where(mask, jnp.concatenate(att_rows, axis=0), 0.0).astype(BF16)
        qi = jnp.concatenate([qd[sub[I], :] * jnp.exp(mids[I]) for I in range(NS)], axis=0).astype(BF16)
        k2 = jnp.concatenate([kd[sub[I], :] * jnp.exp(end - mids[I]) for I in range(NS)], axis=0).astype(BF16)
        v = v_ref[0, :, vs]
        state = st[h]
        o = _dot(att, v) + _dot_nt(qi, state.astype(BF16))
        upd = _dot_tn(v, k2)
        yield
        o_ref[0, :, vs] = o.astype(BF16)
        st[h] = state * jnp.exp(end) + upd

    chains = []
    for h in range(GLA_HB):
        chains.append(direction(qf_ref, kf_ref, vf_ref, lrf_ref, of_ref, stf, 0, False, h))
        chains.append(direction(qb_ref, kb_ref, vb_ref, lrb_ref, ob_ref, stb, 1, True, h))
    while chains:
        alive = []
        for chain in chains:
            if next(chain, chain) is not chain:
                alive.append(chain)
        chains = alive


def _gla(p3, lr3, wg_pad, bg, B, S):
    TS = GLA_BLOCK
    NB = S // TS
    HB = GLA_HB
    kq, kk, kv = COL_QG // (HB * GLA_DK), COL_KG // (HB * GLA_DK), COL_VG // (HB * GLA_DV)

    def fwd(base):
        return lambda b, h, n: (b, n, base + h)

    def bwd(base):
        return lambda b, h, n: (b, NB - 1 - n, base + h)

    def specs(m, lr_map):
        return [
            pl.BlockSpec((1, TS, HB * GLA_DK), m(kq)),
            pl.BlockSpec((1, TS, HB * GLA_DK), m(kk)),
            pl.BlockSpec((1, TS, HB * GLA_DV), m(kv)),
            pl.BlockSpec((1, TS, LR_COLS), lr_map),
        ]

    return pl.pallas_call(
        _gla_kernel,
        grid=(B, GLA_HEADS // HB, NB),
        in_specs=specs(fwd, lambda b, h, n: (b, n, 0)) + specs(bwd, lambda b, h, n: (b, NB - 1 - n, 0)) + [
            pl.BlockSpec((2, LR_COLS, HB * GLA_DK), lambda b, h, n: (0, 0, h)),
            pl.BlockSpec((2, 1, HB * GLA_DK), lambda b, h, n: (0, 0, h)),
        ],
        out_specs=[
            pl.BlockSpec((1, TS, HB * GLA_DV), lambda b, h, n: (b, n, h)),
            pl.BlockSpec((1, TS, HB * GLA_DV), lambda b, h, n: (b, NB - 1 - n, h)),
        ],
        out_shape=[jax.ShapeDtypeStruct((B, S, GLA_VAL), BF16)] * 2,
        scratch_shapes=[
            pltpu.VMEM((HB, GLA_DV, GLA_DK), F32),
            pltpu.VMEM((HB, GLA_DV, GLA_DK), F32),
        ],
        compiler_params=pltpu.CompilerParams(
            dimension_semantics=("parallel", "parallel", "arbitrary"),
            vmem_limit_bytes=V7X_VMEM_LIMIT),
        name="gla",
    )(p3, p3, p3, lr3, p3, p3, p3, lr3, wg_pad, bg)


def _sigmoid(x):
    return 1.0 / (1.0 + jnp.exp(-x))


def _merge_kernel(o0_ref, o1_ref, o2_ref, l0_ref, l1_ref, l2_ref, of_ref, ob_ref, rg_ref, ga_ref, gb_ref,
                  x_ref, wba_ref, wbg_ref, wout_ref, gnorm_ref, nffn_ref, h_ref, hn_ref,
                  oa_s, og_s, oil_s, lil_s):
    tm = x_ref.shape[0]
    for gi, (o_ref, l_ref) in enumerate(((o1_ref, l1_ref), (o2_ref, l2_ref))):
        r = o_ref.shape[1]
        for c in range(r):
            dst = pl.ds(c, tm // r, stride=r)
            lil_s[gi, dst, :] = l_ref[0, c]
            for hh in range(HEADS_PER_GROUP):
                oil_s[gi, hh, dst, :] = o_ref[0, c, :, hh * HEAD_DIM:(hh + 1) * HEAD_DIM].astype(F32)

    mls = (l0_ref[0, 0], lil_s[0], lil_s[1])
    m = jnp.maximum(jnp.maximum(mls[0], mls[1]), mls[2])
    es = [jnp.exp(ml - m) for ml in mls]
    dens = [pltpu.roll(ml, HEAD_DIM - HEADS_PER_GROUP, 1) for ml in mls]
    inv = 1.0 / (es[0] * dens[0] + es[1] * dens[1] + es[2] * dens[2])
    w0, w1, w2 = es[0] * inv, es[1] * inv, es[2] * inv
    for hh in range(HEADS_PER_GROUP):
        hs = slice(hh * HEAD_DIM, (hh + 1) * HEAD_DIM)
        comb = (w0[:, hh:hh + 1] * o0_ref[0, 0, :, hs].astype(F32)
                + w1[:, hh:hh + 1] * oil_s[0, hh]
                + w2[:, hh:hh + 1] * oil_s[1, hh])
        oa_s[:, hs] = comb.astype(BF16)
    u_a = _dot(oa_s[...], wba_ref[...])

    for h in range(GLA_HEADS):
        vs = slice(h * GLA_DV, (h + 1) * GLA_DV)
        og = of_ref[:, vs].astype(F32) + ob_ref[:, vs].astype(F32)
        ms = jnp.mean(og * og, axis=-1, keepdims=True)
        ogn = og * lax.rsqrt(ms + EPS) * gnorm_ref[...]
        rg = rg_ref[:, vs].astype(F32)
        og_s[:, vs] = (ogn * (rg * _sigmoid(rg))).astype(BF16)
    u_b = _dot(og_s[...], wbg_ref[...])

    merged = _sigmoid(ga_ref[...].astype(F32)) * u_a + _sigmoid(gb_ref[...].astype(F32)) * u_b
    h = x_ref[...] + _dot(merged.astype(BF16), wout_ref[...])
    h_ref[...] = h
    ms = jnp.mean(h * h, axis=-1, keepdims=True)
    hn_ref[...] = (h * lax.rsqrt(ms + EPS) * nffn_ref[...]).astype(BF16)


def _merge(o_groups, ml_groups, o_fwd, o_bwd, p1, p2, x, wba, wbg, wout, gnorm, nffn, B, S, *, tm=256):
    t = x.shape[0]
    per_b = S // tm
    row = lambda b, i: (b * per_b + i, 0)
    const = lambda b, i: (0, 0)

    def resident(shape):
        return pl.BlockSpec(shape, const, pipeline_mode=pl.Buffered(1))

    def split_spec(r, width):
        return pl.BlockSpec((1, r, tm // r, width), lambda b, i: (b, 0, i, 0))

    rs = [r for _, r in ATTN_GROUPS]
    return pl.pallas_call(
        _merge_kernel,
        grid=(B, per_b),
        in_specs=[split_spec(r, ATTN_OUT) for r in rs] + [split_spec(r, HEAD_DIM) for r in rs] + [
            pl.BlockSpec((tm, GLA_VAL), row),
            pl.BlockSpec((tm, GLA_VAL), row),
            pl.BlockSpec((tm, GLA_VAL), lambda b, i: (b * per_b + i, COL_RG // GLA_VAL)),
            pl.BlockSpec((tm, D_MODEL), lambda b, i: (b * per_b + i, COL_GA // D_MODEL)),
            pl.BlockSpec((tm, D_MODEL), lambda b, i: (b * per_b + i, COL_GB // D_MODEL)),
            pl.BlockSpec((tm, D_MODEL), row),
            resident((ATTN_OUT, D_MODEL)),
            resident((GLA_VAL, D_MODEL)),
            resident((D_MODEL, D_MODEL)),
            resident((1, GLA_DV)),
            resident((1, D_MODEL)),
        ],
        out_specs=[pl.BlockSpec((tm, D_MODEL), row), pl.BlockSpec((tm, D_MODEL), row)],
        out_shape=[jax.ShapeDtypeStruct((t, D_MODEL), F32), jax.ShapeDtypeStruct((t, D_MODEL), BF16)],
        scratch_shapes=[
            pltpu.VMEM((tm, ATTN_OUT), BF16),
            pltpu.VMEM((tm, GLA_VAL), BF16),
            pltpu.VMEM((2, HEADS_PER_GROUP, tm, HEAD_DIM), F32),
            pltpu.VMEM((2, tm, HEAD_DIM), F32),
        ],
        compiler_params=pltpu.CompilerParams(
            dimension_semantics=("parallel", "parallel"),
            vmem_limit_bytes=V7X_VMEM_LIMIT),
        name="merge",
    )(*o_groups, *ml_groups, o_fwd, o_bwd, p1, p2, p2, x, wba, wbg, wout, gnorm, nffn)


def _ffn_kernel(hn_ref, w1_ref, w2_ref, h_ref, y_ref):
    @pl.when(pl.program_id(1) == 0)
    def _():
        y_ref[...] = h_ref[...]

    a = _dot(hn_ref[...], w1_ref[...])
    a = jnp.square(jnp.maximum(a, 0.0)).astype(BF16)
    y_ref[...] += _dot(a, w2_ref[...])


def _ffn(hn, h, w1, w2, *, tm=512, tf=1024):
    t = hn.shape[0]
    return pl.pallas_call(
        _ffn_kernel,
        grid=(t // tm, D_FF // tf),
        in_specs=[
            pl.BlockSpec((tm, D_MODEL), lambda i, j: (i, 0)),
            pl.BlockSpec((D_MODEL, tf), lambda i, j: (0, j)),
            pl.BlockSpec((tf, D_MODEL), lambda i, j: (j, 0)),
            pl.BlockSpec((tm, D_MODEL), lambda i, j: (i, 0)),
        ],
        out_specs=pl.BlockSpec((tm, D_MODEL), lambda i, j: (i, 0)),
        out_shape=jax.ShapeDtypeStruct((t, D_MODEL), F32),
        compiler_params=pltpu.CompilerParams(
            dimension_semantics=("parallel", "arbitrary"),
            vmem_limit_bytes=V7X_VMEM_LIMIT),
        name="ffn",
    )(hn, w1, w2, h)


def _rope_tables(s_max):
    inv_freq = ROPE_THETA ** (-jnp.arange(0, ROT_DIM, 2, dtype=F32) / ROT_DIM)
    ang = jnp.arange(s_max, dtype=F32)[:, None] * inv_freq[None, :]
    cos, sin = jnp.cos(ang), jnp.sin(ang)
    rest = HEAD_DIM - ROT_DIM
    cos_t = jnp.concatenate([cos, cos, jnp.ones((s_max, rest), F32)], axis=1)
    sin_t = jnp.concatenate([sin, sin, jnp.zeros((s_max, rest), F32)], axis=1)
    return cos_t, sin_t


def _prepare_layer(w_in, w_gla_gate, b_gla_gate):
    w_t = jnp.swapaxes(w_in, 0, 1)
    wg_pad = jnp.zeros((2, LR_COLS, GLA_KEY), F32)
    wg_pad = wg_pad.at[0, 0:GLA_RANK].set(w_gla_gate[0].astype(F32))
    wg_pad = wg_pad.at[1, GLA_RANK:2 * GLA_RANK].set(w_gla_gate[1].astype(F32)).astype(BF16)
    bg = b_gla_gate.astype(F32).reshape(2, 1, GLA_KEY)
    return w_t, wg_pad, bg


def _layer(x3, tables, norm_mix, prepared, q_norm, k_norm, gla_norm, wba, wbg, wout, norm_ffn, w1, w2):
    B, S, _ = x3.shape
    T = B * S
    w_t, wg_pad, bg = prepared
    tn = 1024
    x = x3.reshape(T, D_MODEL)
    xn, lr = _xnorm(x, norm_mix.reshape(1, D_MODEL), w_t)
    o_groups, ml_groups = [], []
    for g in range(N_GROUPS):
        r = ATTN_GROUPS[g][1]
        a_g = _proj(xn, w_t, lambda j, g=g: j * ATTN_QKV + g * ATTN_OUT, 3, r, B, S, f"proj_attn_r{r}", tn=tn,
                    tm=PROJ_TM if r == 1 else PROJ_TM_SPLIT)
        o_g, ml_g = _attention_group(a_g.reshape(B, r, S // r, 3 * ATTN_OUT), tables, q_norm, k_norm, g, B, S)
        o_groups.append(o_g)
        ml_groups.append(ml_g)
    p1 = _proj(xn, w_t, lambda j: W_COL_QG + j * tn, P1_COLS // tn, 1, B, S, "proj_gla", tn=tn, tm=PROJ_TM)
    p2 = _proj(xn, w_t, lambda j: W_COL_GATES + j * tn, 2 * D_MODEL // tn, 1, B, S, "proj_gates", tn=tn, tm=PROJ_TM)
    o_fwd, o_bwd = _gla(p1.reshape(B, S, P1_COLS), lr.reshape(B, S, LR_COLS), wg_pad, bg, B, S)
    h, hn = _merge(o_groups, ml_groups, o_fwd.reshape(T, GLA_VAL), o_bwd.reshape(T, GLA_VAL), p1, p2, x,
                   wba, wbg, wout, gla_norm.reshape(1, GLA_DV), norm_ffn.reshape(1, D_MODEL), B, S)
    y = _ffn(hn, h, w1, w2)
    return y.reshape(B, S, D_MODEL)


def kernel(x_prompt, x_sample, norm_mix, w_in, q_norm, k_norm, w_gla_gate, b_gla_gate, gla_norm,
           w_branch_attn, w_branch_gla, w_out, norm_ffn, w_ff1, w_ff2):
    depth = w_in.shape[0]
    tables = _rope_tables(max(x_prompt.shape[1], x_sample.shape[1]))
    layers = []
    for l in range(depth):
        layers.append((
            norm_mix[l], _prepare_layer(w_in[l], w_gla_gate[l], b_gla_gate[l]),
            q_norm[l].astype(F32), k_norm[l].astype(F32), gla_norm[l].astype(F32),
            w_branch_attn[l].astype(BF16), w_branch_gla[l].astype(BF16), w_out[l].astype(BF16),
            norm_ffn[l].astype(F32), w_ff1[l].astype(BF16), w_ff2[l].astype(BF16)))
    outs = []
    for x in (x_prompt, x_sample):
        for layer in layers:
            x = _layer(x, tables, *layer)
        outs.append(x)
    return tuple(outs)
```

```python
import functools

import jax
import jax.numpy as jnp
from jax import lax
from jax.experimental import pallas as pl
from jax.experimental.pallas import tpu as pltpu

F32 = jnp.float32
BF16 = jnp.bfloat16

D_MODEL = 2048
HEAD_DIM = 128
ATTN_GROUPS = ((128, 1), (512, 4), (2048, 16))
N_GROUPS = 3
HEADS_PER_GROUP = 8
ATTN_QKV = N_GROUPS * HEADS_PER_GROUP * HEAD_DIM
ATTN_OUT = HEADS_PER_GROUP * HEAD_DIM
ROT_DIM = HEAD_DIM // 4
ROPE_THETA = 500000.0
GLA_HEADS = 4
GLA_KEY = 1024
GLA_VAL = 2048
GLA_DK = 256
GLA_DV = 512
GLA_RANK = 16
GLA_NORMALIZER = 16.0
D_FF = 4 * D_MODEL
EPS = 1e-6

W_COL_QG = 3 * ATTN_QKV
W_COL_LR = W_COL_QG + 2 * GLA_KEY + 2 * GLA_VAL
W_COL_GATES = W_COL_LR + 2 * GLA_RANK
COL_QG = 0
COL_KG = 1024
COL_VG = 2048
COL_RG = 4096
P1_COLS = 6144
COL_GA = 0
COL_GB = 2048
LR_COLS = 128

V7X_VMEM_LIMIT = 56 * 1024 * 1024
NEG = -1e30

PROJ_TM = 2048
PROJ_TM_SPLIT = 1024
PERM_ROWS = 256
SIDE_CAST_BLOCKS = 16
ATTN_HALF = 64
ATTN_TQ = 128
ATTN_CHAINS = 8
GLA_BLOCK = 256
GLA_SUB = 64
GLA_HB = 4


def _dot(a, b):
    return jnp.dot(a, b, preferred_element_type=F32)


def _dot_nt(a, b):
    return lax.dot_general(a, b, (((1,), (1,)), ((), ())), preferred_element_type=F32)


def _dot_tn(a, b):
    return lax.dot_general(a, b, (((0,), (0,)), ((), ())), preferred_element_type=F32)


def _split_bf16(x):
    hi = x.astype(BF16)
    lo = (x - hi.astype(F32)).astype(BF16)
    return hi, lo


def _xnorm_kernel(x_ref, gain_ref, wlr_ref, xn_ref, lr_ref):
    x = x_ref[...]
    ms = jnp.mean(x * x, axis=-1, keepdims=True)
    xn = (x * lax.rsqrt(ms + EPS) * gain_ref[...]).astype(BF16)
    xn_ref[...] = xn
    lr_ref[...] = _dot_nt(xn, wlr_ref[...].astype(BF16))


def _xnorm(x, gain, w_t, *, tm=512):
    t = x.shape[0]
    return pl.pallas_call(
        _xnorm_kernel,
        grid=(t // tm,),
        in_specs=[
            pl.BlockSpec((tm, D_MODEL), lambda i: (i, 0)),
            pl.BlockSpec((1, D_MODEL), lambda i: (0, 0)),
            pl.BlockSpec((pl.Element(LR_COLS), pl.Element(D_MODEL)), lambda i: (W_COL_LR, 0)),
        ],
        out_specs=[
            pl.BlockSpec((tm, D_MODEL), lambda i: (i, 0)),
            pl.BlockSpec((tm, LR_COLS), lambda i: (i, 0)),
        ],
        out_shape=[
            jax.ShapeDtypeStruct((t, D_MODEL), BF16),
            jax.ShapeDtypeStruct((t, LR_COLS), F32),
        ],
        compiler_params=pltpu.CompilerParams(
            dimension_semantics=("parallel",),
            vmem_limit_bytes=V7X_VMEM_LIMIT),
        name="xnorm",
    )(x, gain, w_t)


def _proj_kernel(*refs, r, side_cast):
    if side_cast:
        xn_ref, w_ref, cast_in_ref, o_ref, cast_out_ref, wb_s = refs
        cast_out_ref[...] = cast_in_ref[...].astype(BF16)
    else:
        xn_ref, w_ref, o_ref, wb_s = refs

    @pl.when(pl.program_id(1) == 0)
    def _():
        wb_s[...] = w_ref[...].astype(BF16)

    acc = _dot_nt(xn_ref[...], wb_s[...]).astype(BF16)
    if r == 1:
        o_ref[...] = acc
        return
    n = PERM_ROWS // r
    dst = lax.broadcasted_iota(jnp.int32, (PERM_ROWS, PERM_ROWS), 0)
    src = lax.broadcasted_iota(jnp.int32, (PERM_ROWS, PERM_ROWS), 1)
    perm = jnp.where(src == (dst % n) * r + dst // n, 1.0, 0.0).astype(BF16)
    for g in range(acc.shape[0] // PERM_ROWS):
        grouped = _dot(perm, acc[g * PERM_ROWS:(g + 1) * PERM_ROWS, :]).astype(BF16)
        for c in range(r):
            o_ref[0, c, g * n:(g + 1) * n, :] = grouped[c * n:(c + 1) * n, :]


def _proj(xn, w_t, row_start, ncol_blocks, r, B, S, name, *, tm=1024, tn=1024, side_cast=None):
    t = xn.shape[0]
    per_b = S // tm
    ntiles = t // tm
    if r == 1:
        out_specs = [pl.BlockSpec((tm, tn), lambda j, i: (i, j))]
        out_shapes = [jax.ShapeDtypeStruct((t, ncol_blocks * tn), BF16)]
    else:
        out_specs = [pl.BlockSpec((1, r, tm // r, tn), lambda j, i: (i // per_b, 0, i % per_b, j))]
        out_shapes = [jax.ShapeDtypeStruct((B, r, S // r, ncol_blocks * tn), BF16)]
    in_specs = [
        pl.BlockSpec((tm, D_MODEL), lambda j, i: (i, 0)),
        pl.BlockSpec((pl.Element(tn), pl.Element(D_MODEL)), lambda j, i: (pl.multiple_of(row_start(j), 8), 0)),
    ]
    operands = [xn, w_t]
    if side_cast is not None:
        w, axis = side_cast
        assert ncol_blocks * ntiles >= SIDE_CAST_BLOCKS
        blk = list(w.shape)
        blk[axis] //= SIDE_CAST_BLOCKS

        def cast_map(j, i):
            idx = [0, 0]
            idx[axis] = jnp.minimum(j * ntiles + i, SIDE_CAST_BLOCKS - 1)
            return tuple(idx)

        in_specs.append(pl.BlockSpec(tuple(blk), cast_map))
        out_specs.append(pl.BlockSpec(tuple(blk), cast_map))
        out_shapes.append(jax.ShapeDtypeStruct(w.shape, BF16))
        operands.append(w)
    outs = pl.pallas_call(
        functools.partial(_proj_kernel, r=r, side_cast=side_cast is not None),
        grid=(ncol_blocks, ntiles),
        in_specs=in_specs,
        out_specs=out_specs,
        out_shape=out_shapes,
        scratch_shapes=[pltpu.VMEM((tn, D_MODEL), BF16)],
        compiler_params=pltpu.CompilerParams(
            dimension_semantics=("arbitrary", "arbitrary"),
            vmem_limit_bytes=V7X_VMEM_LIMIT),
        name=name,
    )(*operands)
    return outs[0] if side_cast is None else tuple(outs)


def _attn_kernel(q_ref, k_ref, v_ref, cos_ref, sin_ref, qg_ref, kg_ref,
                 o_ref, ml_ref, qs, ks, vs, bias_s, *, L, hb, U):
    TQ, HALF = ATTN_TQ, ATTN_HALF
    TK = TQ + 2 * HALF
    NT = L // TQ
    NI = NT // U
    hblk = pl.program_id(2)

    @pl.when(hblk == 0)
    def _():
        ml_ref[...] = jnp.zeros_like(ml_ref)

    ri = lax.broadcasted_iota(jnp.int32, (TQ, TK), 0)
    ci = lax.broadcasted_iota(jnp.int32, (TQ, TK), 1)
    d = ci - ri
    band = jnp.where(d < 0, NEG, jnp.where(d > 2 * HALF, NEG, 0.0)).astype(F32)
    first = jnp.where(ci < HALF, NEG, band)
    bias_s[0] = band
    bias_s[1] = first
    bias_s[2] = jnp.where(ci >= TQ + HALF, NEG, band)
    bias_s[3] = jnp.where(ci >= TQ + HALF, NEG, first)

    a = lax.broadcasted_iota(jnp.int32, (HEAD_DIM, HEAD_DIM), 0)
    b = lax.broadcasted_iota(jnp.int32, (HEAD_DIM, HEAD_DIM), 1)
    half = ROT_DIM // 2
    ones_m = jnp.ones((HEAD_DIM, HEAD_DIM), BF16)
    rot_m = jnp.where((b < half) & (a == b + half), -1.0,
                      jnp.where((b >= half) & (b < ROT_DIM) & (a == b - half), 1.0, 0.0)).astype(BF16)

    zpad = jnp.zeros((HALF, HEAD_DIM), BF16)
    ztail = jnp.zeros((TQ, HEAD_DIM), BF16)
    for hh in range(hb):
        ks[hh, 0:HALF, :] = zpad
        ks[hh, L + HALF:L + HALF + TQ, :] = ztail
        vs[hh, 0:HALF, 0:HEAD_DIM] = zpad
        vs[hh, L + HALF:L + HALF + TQ, 0:HEAD_DIM] = ztail
        vs[hh, :, HEAD_DIM:2 * HEAD_DIM] = jnp.ones((vs.shape[1], HEAD_DIM), BF16)

    qgain = qg_ref[...] * (HEAD_DIM ** -0.5)
    kgain = kg_ref[...]

    def norm_rope(x, gain, cos, sin):
        ssq = _dot((x * x).astype(BF16), ones_m)
        xn = x * lax.rsqrt(ssq * (1.0 / HEAD_DIM) + EPS) * gain
        return xn * cos + _dot(xn.astype(BF16), rot_m) * sin

    def prep(blk):
        if isinstance(blk, int):
            src, dst, koff = min(blk, NT - 1) * TQ, blk * TQ, blk * TQ + HALF
        else:
            src = pl.multiple_of(jnp.minimum(blk, NT - 1) * TQ, TQ)
            dst = pl.multiple_of(blk * TQ, TQ)
            koff = pl.multiple_of(dst + HALF, HALF)
        rows = pl.ds(src, TQ)
        qdst = pl.ds(dst, TQ)
        kdst = pl.ds(koff, TQ)
        cos, sin = cos_ref[rows, :], sin_ref[rows, :]
        for hh in range(hb):
            hs = slice(hh * HEAD_DIM, (hh + 1) * HEAD_DIM)
            qs[hh, qdst, :] = norm_rope(q_ref[0, 0, rows, hs].astype(F32), qgain, cos, sin).astype(BF16)
            ks[hh, kdst, :] = norm_rope(k_ref[0, 0, rows, hs].astype(F32), kgain, cos, sin).astype(BF16)
            vs[hh, kdst, 0:HEAD_DIM] = v_ref[0, 0, rows, hs]

    lane = lax.broadcasted_iota(jnp.int32, (TQ, HEAD_DIM), 1)

    def score_tiles(tt):
        for u in range(U):
            t = tt * U + u
            q0 = t * TQ if isinstance(t, int) else pl.multiple_of(t * TQ, TQ)
            qrows = pl.ds(q0, TQ)
            krows = pl.ds(q0, TK)
            bias = bias_s[jnp.where(t == 0, 1, 0) + jnp.where(t == NT - 1, 2, 0)]
            ml_tile = ml_ref[0, 0, qrows, :]
            for hh in range(hb):
                hs = slice(hh * HEAD_DIM, (hh + 1) * HEAD_DIM)
                h = hblk * hb + hh
                s = _dot_nt(qs[hh, qrows, :], ks[hh, krows, :]) + bias
                m = jnp.max(s, axis=-1, keepdims=True)
                p = jnp.exp(s - m).astype(BF16)
                acc = _dot(p, vs[hh, krows, :])
                o_ref[0, 0, qrows, hs] = acc[:, :HEAD_DIM].astype(BF16)
                ml_tile = jnp.where(lane == h, m, jnp.where(lane == HEADS_PER_GROUP + h, acc[:, HEAD_DIM:], ml_tile))
            ml_ref[0, 0, qrows, :] = ml_tile

    for blk in range(min(U + 1, NT)):
        prep(blk)

    def body(tt, carry):
        score_tiles(tt)
        for u in range(U):
            prep((tt + 1) * U + 1 + u)
        return carry

    lax.fori_loop(0, NI - 1, body, 0)
    score_tiles(NI - 1)


def _attention_group(a_g, tables, q_gain, k_gain, g, B, S):
    window, r = ATTN_GROUPS[g]
    assert window // (2 * r) == ATTN_HALF
    L = S // r
    assert L % ATTN_TQ == 0
    hb = max(1, min(HEADS_PER_GROUP, 8192 // L))
    nhb = HEADS_PER_GROUP // hb
    bw = hb * HEAD_DIM
    U = max(1, min(ATTN_CHAINS // hb, L // ATTN_TQ))
    assert (L // ATTN_TQ) % U == 0
    cos, sin = (t[:S].reshape(L, r * HEAD_DIM) for t in tables)

    def col_map(part):
        return lambda b, c, h: (b, c, 0, part * nhb + h)

    tab_spec = pl.BlockSpec((L, HEAD_DIM), lambda b, c, h: (0, c))
    gain_spec = pl.BlockSpec((1, HEAD_DIM), lambda b, c, h: (0, 0))
    return pl.pallas_call(
        functools.partial(_attn_kernel, L=L, hb=hb, U=U),
        grid=(B, r, nhb),
        in_specs=[
            pl.BlockSpec((1, 1, L, bw), col_map(0)),
            pl.BlockSpec((1, 1, L, bw), col_map(1)),
            pl.BlockSpec((1, 1, L, bw), col_map(2)),
            tab_spec, tab_spec, gain_spec, gain_spec,
        ],
        out_specs=[
            pl.BlockSpec((1, 1, L, bw), lambda b, c, h: (b, c, 0, h)),
            pl.BlockSpec((1, 1, L, HEAD_DIM), lambda b, c, h: (b, c, 0, 0)),
        ],
        out_shape=[
            jax.ShapeDtypeStruct((B, r, L, ATTN_OUT), BF16),
            jax.ShapeDtypeStruct((B, r, L, HEAD_DIM), F32),
        ],
        scratch_shapes=[
            pltpu.VMEM((hb, L + ATTN_TQ, HEAD_DIM), BF16),
            pltpu.VMEM((hb, L + ATTN_TQ + ATTN_HALF, HEAD_DIM), BF16),
            pltpu.VMEM((hb, L + ATTN_TQ + ATTN_HALF, 2 * HEAD_DIM), BF16),
            pltpu.VMEM((4, ATTN_TQ, ATTN_TQ + 2 * ATTN_HALF), F32),
        ],
        compiler_params=pltpu.CompilerParams(
            dimension_semantics=("parallel", "parallel", "arbitrary"),
            vmem_limit_bytes=V7X_VMEM_LIMIT),
        name=f"attn_g{g}",
    )(a_g, a_g, a_g, cos, sin, q_gain[g:g + 1], k_gain[g:g + 1])


def _gla_kernel(qf_ref, kf_ref, vf_ref, lrf_ref, qb_ref, kb_ref, vb_ref, lrb_ref, wg_ref, bg_ref,
                of_ref, ob_ref, stf, stb):
    C, SC = GLA_BLOCK, GLA_SUB
    NS = C // SC
    n = pl.program_id(2)

    @pl.when(n == 0)
    def _():
        stf[...] = jnp.zeros_like(stf)
        stb[...] = jnp.zeros_like(stb)

    ri = lax.broadcasted_iota(jnp.int32, (C, C), 0)
    ci = lax.broadcasted_iota(jnp.int32, (C, C), 1)

    def direction(q_ref, k_ref, v_ref, lr_ref, o_ref, st, d, backward, h):
        ks = slice(h * GLA_DK, (h + 1) * GLA_DK)
        vs = slice(h * GLA_DV, (h + 1) * GLA_DV)
        z = _dot(lr_ref[0].astype(BF16), wg_ref[d, :, ks]) + bg_ref[d, :, ks]
        yield
        log_sig = jnp.minimum(z, 0.0) - jnp.log(1.0 + jnp.exp(-jnp.abs(z)))
        g_hi, g_lo = _split_bf16(log_sig * (1.0 / GLA_NORMALIZER))
        tri = jnp.where((ci >= ri) if backward else (ri >= ci), 1.0, 0.0).astype(BF16)
        cum = _dot(tri, g_hi) + _dot(tri, g_lo)
        yield
        mid_row = SC // 2 if backward else SC // 2 - 1
        end_row = 0 if backward else C - 1
        mids = [cum[I * SC + mid_row:I * SC + mid_row + 1, :] for I in range(NS)]
        end = cum[end_row:end_row + 1, :]
        sub = [slice(I * SC, (I + 1) * SC) for I in range(NS)]
        dl = jnp.concatenate([cum[sub[I], :] - mids[I] for I in range(NS)], axis=0)
        qd = q_ref[0, :, ks].astype(F32) * (GLA_DK ** -0.5) * jnp.exp(dl)
        kd = k_ref[0, :, ks].astype(F32) * jnp.exp(-dl)
        qd_b = qd.astype(BF16)
        kd_b = kd.astype(BF16)
        att_rows = []
        for I in range(NS):
            blocks = []
            for J in range(NS):
                if (J > I) if backward else (J < I):
                    blocks.append((kd[sub[J], :] * jnp.exp(mids[I] - mids[J])).astype(BF16))
                else:
                    blocks.append(kd_b[sub[J], :])
            att_rows.append(_dot_nt(qd_b[sub[I], :], jnp.concatenate(blocks, axis=0)))
        yield
        mask = (ci > ri) if backward else (ri >= ci)
        att = jnp.where(mask, jnp.concatenate(att_rows, axis=0), 0.0).astype(BF16)
        qi = jnp.concatenate([qd[sub[I], :] * jnp.exp(mids[I]) for I in range(NS)], axis=0).astype(BF16)
        k2 = jnp.concatenate([kd[sub[I], :] * jnp.exp(end - mids[I]) for I in range(NS)], axis=0).astype(BF16)
        v = v_ref[0, :, vs]
        state = st[h]
        o = _dot(att, v) + _dot_nt(qi, state.astype(BF16))
        upd = _dot_tn(v, k2)
        yield
        o_ref[0, :, vs] = o.astype(BF16)
        st[h] = state * jnp.exp(end) + upd

    chains = []
    for h in range(GLA_HB):
        chains.append(direction(qf_ref, kf_ref, vf_ref, lrf_ref, of_ref, stf, 0, False, h))
        chains.append(direction(qb_ref, kb_ref, vb_ref, lrb_ref, ob_ref, stb, 1, True, h))
    while chains:
        alive = []
        for chain in chains:
            if next(chain, chain) is not chain:
                alive.append(chain)
        chains = alive


def _gla(p3, lr3, wg_pad, bg, B, S):
    TS = GLA_BLOCK
    NB = S // TS
    HB = GLA_HB
    kq, kk, kv = COL_QG // (HB * GLA_DK), COL_KG // (HB * GLA_DK), COL_VG // (HB * GLA_DV)

    def fwd(base):
        return lambda b, h, n: (b, n, base + h)

    def bwd(base):
        return lambda b, h, n: (b, NB - 1 - n, base + h)

    def specs(m, lr_map):
        return [
            pl.BlockSpec((1, TS, HB * GLA_DK), m(kq)),
            pl.BlockSpec((1, TS, HB * GLA_DK), m(kk)),
            pl.BlockSpec((1, TS, HB * GLA_DV), m(kv)),
            pl.BlockSpec((1, TS, LR_COLS), lr_map),
        ]

    return pl.pallas_call(
        _gla_kernel,
        grid=(B, GLA_HEADS // HB, NB),
        in_specs=specs(fwd, lambda b, h, n: (b, n, 0)) + specs(bwd, lambda b, h, n: (b, NB - 1 - n, 0)) + [
            pl.BlockSpec((2, LR_COLS, HB * GLA_DK), lambda b, h, n: (0, 0, h)),
            pl.BlockSpec((2, 1, HB * GLA_DK), lambda b, h, n: (0, 0, h)),
        ],
        out_specs=[
            pl.BlockSpec((1, TS, HB * GLA_DV), lambda b, h, n: (b, n, h)),
            pl.BlockSpec((1, TS, HB * GLA_DV), lambda b, h, n: (b, NB - 1 - n, h)),
        ],
        out_shape=[jax.ShapeDtypeStruct((B, S, GLA_VAL), BF16)] * 2,
        scratch_shapes=[
            pltpu.VMEM((HB, GLA_DV, GLA_DK), F32),
            pltpu.VMEM((HB, GLA_DV, GLA_DK), F32),
        ],
        compiler_params=pltpu.CompilerParams(
            dimension_semantics=("parallel", "parallel", "arbitrary"),
            vmem_limit_bytes=V7X_VMEM_LIMIT),
        name="gla",
    )(p3, p3, p3, lr3, p3, p3, p3, lr3, wg_pad, bg)


def _sigmoid(x):
    return 1.0 / (1.0 + jnp.exp(-x))


def _merge_kernel(o0_ref, o1_ref, o2_ref, l0_ref, l1_ref, l2_ref, of_ref, ob_ref, rg_ref, ga_ref, gb_ref,
                  x_ref, wba_ref, wbg_ref, wout_ref, gnorm_ref, nffn_ref, h_ref, hn_ref,
                  oa_s, og_s, oil_s, lil_s):
    tm = x_ref.shape[0]
    for gi, (o_ref, l_ref) in enumerate(((o1_ref, l1_ref), (o2_ref, l2_ref))):
        r = o_ref.shape[1]
        for c in range(r):
            dst = pl.ds(c, tm // r, stride=r)
            lil_s[gi, dst, :] = l_ref[0, c]
            for hh in range(HEADS_PER_GROUP):
                oil_s[gi, hh, dst, :] = o_ref[0, c, :, hh * HEAD_DIM:(hh + 1) * HEAD_DIM].astype(F32)

    mls = (l0_ref[0, 0], lil_s[0], lil_s[1])
    m = jnp.maximum(jnp.maximum(mls[0], mls[1]), mls[2])
    es = [jnp.exp(ml - m) for ml in mls]
    dens = [pltpu.roll(ml, HEAD_DIM - HEADS_PER_GROUP, 1) for ml in mls]
    inv = 1.0 / (es[0] * dens[0] + es[1] * dens[1] + es[2] * dens[2])
    w0, w1, w2 = es[0] * inv, es[1] * inv, es[2] * inv
    for hh in range(HEADS_PER_GROUP):
        hs = slice(hh * HEAD_DIM, (hh + 1) * HEAD_DIM)
        comb = (w0[:, hh:hh + 1] * o0_ref[0, 0, :, hs].astype(F32)
                + w1[:, hh:hh + 1] * oil_s[0, hh]
                + w2[:, hh:hh + 1] * oil_s[1, hh])
        oa_s[:, hs] = comb.astype(BF16)
    u_a = _dot(oa_s[...], wba_ref[...])

    for h in range(GLA_HEADS):
        vs = slice(h * GLA_DV, (h + 1) * GLA_DV)
        og = of_ref[:, vs].astype(F32) + ob_ref[:, vs].astype(F32)
        ms = jnp.mean(og * og, axis=-1, keepdims=True)
        ogn = og * lax.rsqrt(ms + EPS) * gnorm_ref[...]
        rg = rg_ref[:, vs].astype(F32)
        og_s[:, vs] = (ogn * (rg * _sigmoid(rg))).astype(BF16)
    u_b = _dot(og_s[...], wbg_ref[...])

    merged = _sigmoid(ga_ref[...].astype(F32)) * u_a + _sigmoid(gb_ref[...].astype(F32)) * u_b
    h = x_ref[...] + _dot(merged.astype(BF16), wout_ref[...])
    h_ref[...] = h
    ms = jnp.mean(h * h, axis=-1, keepdims=True)
    hn_ref[...] = (h * lax.rsqrt(ms + EPS) * nffn_ref[...]).astype(BF16)


def _merge(o_groups, ml_groups, o_fwd, o_bwd, p1, p2, x, wba, wbg, wout, gnorm, nffn, B, S, *, tm=256):
    t = x.shape[0]
    per_b = S // tm
    row = lambda b, i: (b * per_b + i, 0)
    const = lambda b, i: (0, 0)

    def resident(shape):
        return pl.BlockSpec(shape, const, pipeline_mode=pl.Buffered(1))

    def split_spec(r, width):
        return pl.BlockSpec((1, r, tm // r, width), lambda b, i: (b, 0, i, 0))

    rs = [r for _, r in ATTN_GROUPS]
    return pl.pallas_call(
        _merge_kernel,
        grid=(B, per_b),
        in_specs=[split_spec(r, ATTN_OUT) for r in rs] + [split_spec(r, HEAD_DIM) for r in rs] + [
            pl.BlockSpec((tm, GLA_VAL), row),
            pl.BlockSpec((tm, GLA_VAL), row),
            pl.BlockSpec((tm, GLA_VAL), lambda b, i: (b * per_b + i, COL_RG // GLA_VAL)),
            pl.BlockSpec((tm, D_MODEL), lambda b, i: (b * per_b + i, COL_GA // D_MODEL)),
            pl.BlockSpec((tm, D_MODEL), lambda b, i: (b * per_b + i, COL_GB // D_MODEL)),
            pl.BlockSpec((tm, D_MODEL), row),
            resident((ATTN_OUT, D_MODEL)),
            resident((GLA_VAL, D_MODEL)),
            resident((D_MODEL, D_MODEL)),
            resident((1, GLA_DV)),
            resident((1, D_MODEL)),
        ],
        out_specs=[pl.BlockSpec((tm, D_MODEL), row), pl.BlockSpec((tm, D_MODEL), row)],
        out_shape=[jax.ShapeDtypeStruct((t, D_MODEL), F32), jax.ShapeDtypeStruct((t, D_MODEL), BF16)],
        scratch_shapes=[
            pltpu.VMEM((tm, ATTN_OUT), BF16),
            pltpu.VMEM((tm, GLA_VAL), BF16),
            pltpu.VMEM((2, HEADS_PER_GROUP, tm, HEAD_DIM), F32),
            pltpu.VMEM((2, tm, HEAD_DIM), F32),
        ],
        compiler_params=pltpu.CompilerParams(
            dimension_semantics=("parallel", "parallel"),
            vmem_limit_bytes=V7X_VMEM_LIMIT),
        name="merge",
    )(*o_groups, *ml_groups, o_fwd, o_bwd, p1, p2, p2, x, wba, wbg, wout, gnorm, nffn)


def _ffn_kernel(hn_ref, w1_ref, w2_ref, h_ref, y_ref):
    @pl.when(pl.program_id(1) == 0)
    def _():
        y_ref[...] = h_ref[...]

    a = _dot(hn_ref[...], w1_ref[...])
    a = jnp.square(jnp.maximum(a, 0.0)).astype(BF16)
    y_ref[...] += _dot(a, w2_ref[...])


def _ffn(hn, h, w1, w2, *, tm=512, tf=1024):
    t = hn.shape[0]
    return pl.pallas_call(
        _ffn_kernel,
        grid=(t // tm, D_FF // tf),
        in_specs=[
            pl.BlockSpec((tm, D_MODEL), lambda i, j: (i, 0)),
            pl.BlockSpec((D_MODEL, tf), lambda i, j: (0, j)),
            pl.BlockSpec((tf, D_MODEL), lambda i, j: (j, 0)),
            pl.BlockSpec((tm, D_MODEL), lambda i, j: (i, 0)),
        ],
        out_specs=pl.BlockSpec((tm, D_MODEL), lambda i, j: (i, 0)),
        out_shape=jax.ShapeDtypeStruct((t, D_MODEL), F32),
        compiler_params=pltpu.CompilerParams(
            dimension_semantics=("parallel", "arbitrary"),
            vmem_limit_bytes=V7X_VMEM_LIMIT),
        name="ffn",
    )(hn, w1, w2, h)


def _rope_tables(s_max):
    inv_freq = ROPE_THETA ** (-jnp.arange(0, ROT_DIM, 2, dtype=F32) / ROT_DIM)
    ang = jnp.arange(s_max, dtype=F32)[:, None] * inv_freq[None, :]
    cos, sin = jnp.cos(ang), jnp.sin(ang)
    rest = HEAD_DIM - ROT_DIM
    cos_t = jnp.concatenate([cos, cos, jnp.ones((s_max, rest), F32)], axis=1)
    sin_t = jnp.concatenate([sin, sin, jnp.zeros((s_max, rest), F32)], axis=1)
    return cos_t, sin_t


def _prepare_layer(w_in, w_gla_gate, b_gla_gate):
    w_t = jnp.swapaxes(w_in, 0, 1)
    wg_pad = jnp.zeros((2, LR_COLS, GLA_KEY), F32)
    wg_pad = wg_pad.at[0, 0:GLA_RANK].set(w_gla_gate[0].astype(F32))
    wg_pad = wg_pad.at[1, GLA_RANK:2 * GLA_RANK].set(w_gla_gate[1].astype(F32)).astype(BF16)
    bg = b_gla_gate.astype(F32).reshape(2, 1, GLA_KEY)
    return w_t, wg_pad, bg


def _layer(x3, tables, norm_mix, prepared, q_norm, k_norm, gla_norm, wba, wbg, wout, norm_ffn, ffn_w):
    B, S, _ = x3.shape
    T = B * S
    w_t, wg_pad, bg = prepared
    tn = 1024
    x = x3.reshape(T, D_MODEL)
    xn, lr = _xnorm(x, norm_mix.reshape(1, D_MODEL), w_t)
    ffn_w = list(ffn_w)
    cast_axis = (1, 0)
    o_groups, ml_groups = [], []
    for g in range(N_GROUPS):
        r = ATTN_GROUPS[g][1]
        pending = [k for k in range(2) if ffn_w[k].dtype != BF16]
        side = (ffn_w[pending[0]], cast_axis[pending[0]]) if r > 1 and pending else None
        a_g = _proj(xn, w_t, lambda j, g=g: j * ATTN_QKV + g * ATTN_OUT, 3, r, B, S, f"proj_attn_r{r}", tn=tn,
                    tm=PROJ_TM if r == 1 else PROJ_TM_SPLIT, side_cast=side)
        if side is not None:
            a_g, ffn_w[pending[0]] = a_g
        o_g, ml_g = _attention_group(a_g.reshape(B, r, S // r, 3 * ATTN_OUT), tables, q_norm, k_norm, g, B, S)
        o_groups.append(o_g)
        ml_groups.append(ml_g)
    p1 = _proj(xn, w_t, lambda j: W_COL_QG + j * tn, P1_COLS // tn, 1, B, S, "proj_gla", tn=tn, tm=PROJ_TM)
    p2 = _proj(xn, w_t, lambda j: W_COL_GATES + j * tn, 2 * D_MODEL // tn, 1, B, S, "proj_gates", tn=tn, tm=PROJ_TM)
    o_fwd, o_bwd = _gla(p1.reshape(B, S, P1_COLS), lr.reshape(B, S, LR_COLS), wg_pad, bg, B, S)
    h, hn = _merge(o_groups, ml_groups, o_fwd.reshape(T, GLA_VAL), o_bwd.reshape(T, GLA_VAL), p1, p2, x,
                   wba, wbg, wout, gla_norm.reshape(1, GLA_DV), norm_ffn.reshape(1, D_MODEL), B, S)
    y = _ffn(hn, h, ffn_w[0].astype(BF16), ffn_w[1].astype(BF16))
    return y.reshape(B, S, D_MODEL), tuple(ffn_w)


def kernel(x_prompt, x_sample, norm_mix, w_in, q_norm, k_norm, w_gla_gate, b_gla_gate, gla_norm,
           w_branch_attn, w_branch_gla, w_out, norm_ffn, w_ff1, w_ff2):
    depth = w_in.shape[0]
    tables = _rope_tables(max(x_prompt.shape[1], x_sample.shape[1]))
    layers = []
    for l in range(depth):
        layers.append((
            norm_mix[l], _prepare_layer(w_in[l], w_gla_gate[l], b_gla_gate[l]),
            q_norm[l].astype(F32), k_norm[l].astype(F32), gla_norm[l].astype(F32),
            w_branch_attn[l].astype(BF16), w_branch_gla[l].astype(BF16), w_out[l].astype(BF16),
            norm_ffn[l].astype(F32)))
    ffn_ws = [(w_ff1[l], w_ff2[l]) for l in range(depth)]
    outs = []
    for x in (x_prompt, x_sample):
        for l, layer in enumerate(layers):
            x, ffn_ws[l] = _layer(x, tables, *layer, ffn_ws[l])
        outs.append(x)
    return tuple(outs)
```

```python
import functools

import jax
import jax.numpy as jnp
from jax import lax
from jax.experimental import pallas as pl
from jax.experimental.pallas import tpu as pltpu

F32 = jnp.float32
BF16 = jnp.bfloat16

D_MODEL = 2048
HEAD_DIM = 128
ATTN_GROUPS = ((128, 1), (512, 4), (2048, 16))
N_GROUPS = 3
HEADS_PER_GROUP = 8
ATTN_QKV = N_GROUPS * HEADS_PER_GROUP * HEAD_DIM
ATTN_OUT = HEADS_PER_GROUP * HEAD_DIM
ROT_DIM = HEAD_DIM // 4
ROPE_THETA = 500000.0
GLA_HEADS = 4
GLA_KEY = 1024
GLA_VAL = 2048
GLA_DK = 256
GLA_DV = 512
GLA_RANK = 16
GLA_NORMALIZER = 16.0
D_FF = 4 * D_MODEL
EPS = 1e-6

W_COL_QG = 3 * ATTN_QKV
W_COL_LR = W_COL_QG + 2 * GLA_KEY + 2 * GLA_VAL
W_COL_GATES = W_COL_LR + 2 * GLA_RANK
COL_QG = 0
COL_KG = 1024
COL_VG = 2048
COL_RG = 4096
P1_COLS = 6144
COL_GA = 0
COL_GB = 2048
LR_COLS = 128

V7X_VMEM_LIMIT = 56 * 1024 * 1024
NEG = -1e30

PROJ_TM = 2048
PROJ_TM_SPLIT = 1024
PERM_ROWS = 256
SIDE_CAST_BLOCKS = 16
ATTN_HALF = 64
ATTN_TQ = 128
ATTN_CHAINS = 16
GLA_BLOCK = 256
GLA_SUB = 64
GLA_HB = 4


def _dot(a, b):
    return jnp.dot(a, b, preferred_element_type=F32)


def _dot_nt(a, b):
    return lax.dot_general(a, b, (((1,), (1,)), ((), ())), preferred_element_type=F32)


def _dot_tn(a, b):
    return lax.dot_general(a, b, (((0,), (0,)), ((), ())), preferred_element_type=F32)


def _split_bf16(x):
    hi = x.astype(BF16)
    lo = (x - hi.astype(F32)).astype(BF16)
    return hi, lo


def _xnorm_kernel(x_ref, gain_ref, wlr_ref, xn_ref, lr_ref):
    x = x_ref[...]
    ms = jnp.mean(x * x, axis=-1, keepdims=True)
    xn = (x * lax.rsqrt(ms + EPS) * gain_ref[...]).astype(BF16)
    xn_ref[...] = xn
    lr_ref[...] = _dot_nt(xn, wlr_ref[...].astype(BF16))


def _xnorm(x, gain, w_t, *, tm=512):
    t = x.shape[0]
    return pl.pallas_call(
        _xnorm_kernel,
        grid=(t // tm,),
        in_specs=[
            pl.BlockSpec((tm, D_MODEL), lambda i: (i, 0)),
            pl.BlockSpec((1, D_MODEL), lambda i: (0, 0)),
            pl.BlockSpec((pl.Element(LR_COLS), pl.Element(D_MODEL)), lambda i: (W_COL_LR, 0)),
        ],
        out_specs=[
            pl.BlockSpec((tm, D_MODEL), lambda i: (i, 0)),
            pl.BlockSpec((tm, LR_COLS), lambda i: (i, 0)),
        ],
        out_shape=[
            jax.ShapeDtypeStruct((t, D_MODEL), BF16),
            jax.ShapeDtypeStruct((t, LR_COLS), F32),
        ],
        compiler_params=pltpu.CompilerParams(
            dimension_semantics=("parallel",),
            vmem_limit_bytes=V7X_VMEM_LIMIT),
        name="xnorm",
    )(x, gain, w_t)


def _proj_kernel(*refs, r, side_cast):
    if side_cast:
        xn_ref, w_ref, cast_in_ref, o_ref, cast_out_ref, wb_s = refs
        cast_out_ref[...] = cast_in_ref[...].astype(BF16)
    else:
        xn_ref, w_ref, o_ref, wb_s = refs

    @pl.when(pl.program_id(1) == 0)
    def _():
        wb_s[...] = w_ref[...].astype(BF16)

    acc = _dot_nt(xn_ref[...], wb_s[...]).astype(BF16)
    if r == 1:
        o_ref[...] = acc
        return
    n = PERM_ROWS // r
    dst = lax.broadcasted_iota(jnp.int32, (PERM_ROWS, PERM_ROWS), 0)
    src = lax.broadcasted_iota(jnp.int32, (PERM_ROWS, PERM_ROWS), 1)
    perm = jnp.where(src == (dst % n) * r + dst // n, 1.0, 0.0).astype(BF16)
    for g in range(acc.shape[0] // PERM_ROWS):
        grouped = _dot(perm, acc[g * PERM_ROWS:(g + 1) * PERM_ROWS, :]).astype(BF16)
        for c in range(r):
            o_ref[0, c, g * n:(g + 1) * n, :] = grouped[c * n:(c + 1) * n, :]


def _proj(xn, w_t, row_start, ncol_blocks, r, B, S, name, *, tm=1024, tn=1024, side_cast=None):
    t = xn.shape[0]
    per_b = S // tm
    ntiles = t // tm
    if r == 1:
        out_specs = [pl.BlockSpec((tm, tn), lambda j, i: (i, j))]
        out_shapes = [jax.ShapeDtypeStruct((t, ncol_blocks * tn), BF16)]
    else:
        out_specs = [pl.BlockSpec((1, r, tm // r, tn), lambda j, i: (i // per_b, 0, i % per_b, j))]
        out_shapes = [jax.ShapeDtypeStruct((B, r, S // r, ncol_blocks * tn), BF16)]
    in_specs = [
        pl.BlockSpec((tm, D_MODEL), lambda j, i: (i, 0)),
        pl.BlockSpec((pl.Element(tn), pl.Element(D_MODEL)), lambda j, i: (pl.multiple_of(row_start(j), 8), 0)),
    ]
    operands = [xn, w_t]
    if side_cast is not None:
        w, axis = side_cast
        assert ncol_blocks * ntiles >= SIDE_CAST_BLOCKS
        blk = list(w.shape)
        blk[axis] //= SIDE_CAST_BLOCKS

        def cast_map(j, i):
            idx = [0, 0]
            idx[axis] = jnp.minimum(j * ntiles + i, SIDE_CAST_BLOCKS - 1)
            return tuple(idx)

        in_specs.append(pl.BlockSpec(tuple(blk), cast_map))
        out_specs.append(pl.BlockSpec(tuple(blk), cast_map))
        out_shapes.append(jax.ShapeDtypeStruct(w.shape, BF16))
        operands.append(w)
    outs = pl.pallas_call(
        functools.partial(_proj_kernel, r=r, side_cast=side_cast is not None),
        grid=(ncol_blocks, ntiles),
        in_specs=in_specs,
        out_specs=out_specs,
        out_shape=out_shapes,
        scratch_shapes=[pltpu.VMEM((tn, D_MODEL), BF16)],
        compiler_params=pltpu.CompilerParams(
            dimension_semantics=("arbitrary", "arbitrary"),
            vmem_limit_bytes=V7X_VMEM_LIMIT),
        name=name,
    )(*operands)
    return outs[0] if side_cast is None else tuple(outs)


def _attn_kernel(q_ref, k_ref, v_ref, cos_ref, sin_ref, qg_ref, kg_ref,
                 o_ref, ml_ref, qs, ks, vs, bias_s, *, L, hb, U):
    TQ, HALF = ATTN_TQ, ATTN_HALF
    TK = TQ + 2 * HALF
    NT = L // TQ
    NI = NT // U
    hblk = pl.program_id(2)

    @pl.when(hblk == 0)
    def _():
        ml_ref[...] = jnp.zeros_like(ml_ref)

    ri = lax.broadcasted_iota(jnp.int32, (TQ, TK), 0)
    ci = lax.broadcasted_iota(jnp.int32, (TQ, TK), 1)
    d = ci - ri
    band = jnp.where(d < 0, NEG, jnp.where(d > 2 * HALF, NEG, 0.0)).astype(F32)
    first = jnp.where(ci < HALF, NEG, band)
    bias_s[0] = band
    bias_s[1] = first
    bias_s[2] = jnp.where(ci >= TQ + HALF, NEG, band)
    bias_s[3] = jnp.where(ci >= TQ + HALF, NEG, first)

    a = lax.broadcasted_iota(jnp.int32, (HEAD_DIM, HEAD_DIM), 0)
    b = lax.broadcasted_iota(jnp.int32, (HEAD_DIM, HEAD_DIM), 1)
    half = ROT_DIM // 2
    ones_m = jnp.ones((HEAD_DIM, HEAD_DIM), BF16)
    rot_m = jnp.where((b < half) & (a == b + half), -1.0,
                      jnp.where((b >= half) & (b < ROT_DIM) & (a == b - half), 1.0, 0.0)).astype(BF16)

    zpad = jnp.zeros((HALF, HEAD_DIM), BF16)
    ztail = jnp.zeros((TQ, HEAD_DIM), BF16)
    for hh in range(hb):
        ks[hh, 0:HALF, :] = zpad
        ks[hh, L + HALF:L + HALF + TQ, :] = ztail
        vs[hh, 0:HALF, 0:HEAD_DIM] = zpad
        vs[hh, L + HALF:L + HALF + TQ, 0:HEAD_DIM] = ztail
        vs[hh, :, HEAD_DIM:2 * HEAD_DIM] = jnp.ones((vs.shape[1], HEAD_DIM), BF16)

    qgain = qg_ref[...] * (HEAD_DIM ** -0.5)
    kgain = kg_ref[...]

    def norm_rope(x, gain, cos, sin):
        ssq = _dot((x * x).astype(BF16), ones_m)
        xn = x * lax.rsqrt(ssq * (1.0 / HEAD_DIM) + EPS) * gain
        return xn * cos + _dot(xn.astype(BF16), rot_m) * sin

    def prep(blk):
        if isinstance(blk, int):
            src, dst, koff = min(blk, NT - 1) * TQ, blk * TQ, blk * TQ + HALF
        else:
            src = pl.multiple_of(jnp.minimum(blk, NT - 1) * TQ, TQ)
            dst = pl.multiple_of(blk * TQ, TQ)
            koff = pl.multiple_of(dst + HALF, HALF)
        rows = pl.ds(src, TQ)
        qdst = pl.ds(dst, TQ)
        kdst = pl.ds(koff, TQ)
        cos, sin = cos_ref[rows, :], sin_ref[rows, :]
        for hh in range(hb):
            hs = slice(hh * HEAD_DIM, (hh + 1) * HEAD_DIM)
            qs[hh, qdst, :] = norm_rope(q_ref[0, 0, rows, hs].astype(F32), qgain, cos, sin).astype(BF16)
            ks[hh, kdst, :] = norm_rope(k_ref[0, 0, rows, hs].astype(F32), kgain, cos, sin).astype(BF16)
            vs[hh, kdst, 0:HEAD_DIM] = v_ref[0, 0, rows, hs]

    lane = lax.broadcasted_iota(jnp.int32, (TQ, HEAD_DIM), 1)

    def score_tiles(tt):
        for u in range(U):
            t = tt * U + u
            q0 = t * TQ if isinstance(t, int) else pl.multiple_of(t * TQ, TQ)
            qrows = pl.ds(q0, TQ)
            krows = pl.ds(q0, TK)
            bias = bias_s[jnp.where(t == 0, 1, 0) + jnp.where(t == NT - 1, 2, 0)]
            ml_tile = ml_ref[0, 0, qrows, :]
            for hh in range(hb):
                hs = slice(hh * HEAD_DIM, (hh + 1) * HEAD_DIM)
                h = hblk * hb + hh
                s = _dot_nt(qs[hh, qrows, :], ks[hh, krows, :]) + bias
                m = jnp.max(s, axis=-1, keepdims=True)
                p = jnp.exp(s - m).astype(BF16)
                acc = _dot(p, vs[hh, krows, :])
                o_ref[0, 0, qrows, hs] = acc[:, :HEAD_DIM].astype(BF16)
                ml_tile = jnp.where(lane == h, m, jnp.where(lane == HEADS_PER_GROUP + h, acc[:, HEAD_DIM:], ml_tile))
            ml_ref[0, 0, qrows, :] = ml_tile

    for blk in range(min(U + 1, NT)):
        prep(blk)

    def body(tt, carry):
        score_tiles(tt)
        for u in range(U):
            prep((tt + 1) * U + 1 + u)
        return carry

    lax.fori_loop(0, NI - 1, body, 0)
    score_tiles(NI - 1)


def _attention_group(a_g, tables, q_gain, k_gain, g, B, S):
    window, r = ATTN_GROUPS[g]
    assert window // (2 * r) == ATTN_HALF
    L = S // r
    assert L % ATTN_TQ == 0
    hb = max(1, min(HEADS_PER_GROUP, 8192 // L))
    nhb = HEADS_PER_GROUP // hb
    bw = hb * HEAD_DIM
    U = max(1, min(ATTN_CHAINS // hb, L // ATTN_TQ))
    assert (L // ATTN_TQ) % U == 0
    cos, sin = (t[:S].reshape(L, r * HEAD_DIM) for t in tables)

    def col_map(part):
        return lambda b, c, h: (b, c, 0, part * nhb + h)

    tab_spec = pl.BlockSpec((L, HEAD_DIM), lambda b, c, h: (0, c))
    gain_spec = pl.BlockSpec((1, HEAD_DIM), lambda b, c, h: (0, 0))
    return pl.pallas_call(
        functools.partial(_attn_kernel, L=L, hb=hb, U=U),
        grid=(B, r, nhb),
        in_specs=[
            pl.BlockSpec((1, 1, L, bw), col_map(0)),
            pl.BlockSpec((1, 1, L, bw), col_map(1)),
            pl.BlockSpec((1, 1, L, bw), col_map(2)),
            tab_spec, tab_spec, gain_spec, gain_spec,
        ],
        out_specs=[
            pl.BlockSpec((1, 1, L, bw), lambda b, c, h: (b, c, 0, h)),
            pl.BlockSpec((1, 1, L, HEAD_DIM), lambda b, c, h: (b, c, 0, 0)),
        ],
        out_shape=[
            jax.ShapeDtypeStruct((B, r, L, ATTN_OUT), BF16),
            jax.ShapeDtypeStruct((B, r, L, HEAD_DIM), F32),
        ],
        scratch_shapes=[
            pltpu.VMEM((hb, L + ATTN_TQ, HEAD_DIM), BF16),
            pltpu.VMEM((hb, L + ATTN_TQ + ATTN_HALF, HEAD_DIM), BF16),
            pltpu.VMEM((hb, L + ATTN_TQ + ATTN_HALF, 2 * HEAD_DIM), BF16),
            pltpu.VMEM((4, ATTN_TQ, ATTN_TQ + 2 * ATTN_HALF), F32),
        ],
        compiler_params=pltpu.CompilerParams(
            dimension_semantics=("parallel", "parallel", "arbitrary"),
            vmem_limit_bytes=V7X_VMEM_LIMIT),
        name=f"attn_g{g}",
    )(a_g, a_g, a_g, cos, sin, q_gain[g:g + 1], k_gain[g:g + 1])


def _gla_kernel(qf_ref, kf_ref, vf_ref, lrf_ref, qb_ref, kb_ref, vb_ref, lrb_ref, wg_ref, bg_ref,
                of_ref, ob_ref, stf, stb):
    C, SC = GLA_BLOCK, GLA_SUB
    NS = C // SC
    n = pl.program_id(2)

    @pl.when(n == 0)
    def _():
        stf[...] = jnp.zeros_like(stf)
        stb[...] = jnp.zeros_like(stb)

    ri = lax.broadcasted_iota(jnp.int32, (C, C), 0)
    ci = lax.broadcasted_iota(jnp.int32, (C, C), 1)

    def direction(q_ref, k_ref, v_ref, lr_ref, o_ref, st, d, backward, h):
        ks = slice(h * GLA_DK, (h + 1) * GLA_DK)
        vs = slice(h * GLA_DV, (h + 1) * GLA_DV)
        z = _dot(lr_ref[0].astype(BF16), wg_ref[d, :, ks]) + bg_ref[d, :, ks]
        yield
        log_sig = jnp.minimum(z, 0.0) - jnp.log(1.0 + jnp.exp(-jnp.abs(z)))
        g_hi, g_lo = _split_bf16(log_sig * (1.0 / GLA_NORMALIZER))
        tri = jnp.where((ci >= ri) if backward else (ri >= ci), 1.0, 0.0).astype(BF16)
        cum = _dot(tri, g_hi) + _dot(tri, g_lo)
        yield
        mid_row = SC // 2 if backward else SC // 2 - 1
        end_row = 0 if backward else C - 1
        mids = [cum[I * SC + mid_row:I * SC + mid_row + 1, :] for I in range(NS)]
        end = cum[end_row:end_row + 1, :]
        sub = [slice(I * SC, (I + 1) * SC) for I in range(NS)]
        dl = jnp.concatenate([cum[sub[I], :] - mids[I] for I in range(NS)], axis=0)
        qd = q_ref[0, :, ks].astype(F32) * (GLA_DK ** -0.5) * jnp.exp(dl)
        kd = k_ref[0, :, ks].astype(F32) * jnp.exp(-dl)
        qd_b = qd.astype(BF16)
        kd_b = kd.astype(BF16)
        att_rows = []
        for I in range(NS):
            blocks = []
            for J in range(NS):
                if (J > I) if backward else (J < I):
                    blocks.append((kd[sub[J], :] * jnp.exp(mids[I] - mids[J])).astype(BF16))
                else:
                    blocks.append(kd_b[sub[J], :])
            att_rows.append(_dot_nt(qd_b[sub[I], :], jnp.concatenate(blocks, axis=0)))
        yield
        mask = (ci > ri) if backward else (ri >= ci)
        att = jnp.where(mask, jnp.concatenate(att_rows, axis=0), 0.0).astype(BF16)
        qi = jnp.concatenate([qd[sub[I], :] * jnp.exp(mids[I]) for I in range(NS)], axis=0).astype(BF16)
        k2 = jnp.concatenate([kd[sub[I], :] * jnp.exp(end - mids[I]) for I in range(NS)], axis=0).astype(BF16)
        v = v_ref[0, :, vs]
        state = st[h]
        o = _dot(att, v) + _dot_nt(qi, state.astype(BF16))
        upd = _dot_tn(v, k2)
        yield
        o_ref[0, :, vs] = o.astype(BF16)
        st[h] = state * jnp.exp(end) + upd

    chains = []
    for h in range(GLA_HB):
        chains.append(direction(qf_ref, kf_ref, vf_ref, lrf_ref, of_ref, stf, 0, False, h))
        chains.append(direction(qb_ref, kb_ref, vb_ref, lrb_ref, ob_ref, stb, 1, True, h))
    while chains:
        alive = []
        for chain in chains:
            if next(chain, chain) is not chain:
                alive.append(chain)
        chains = alive


def _gla(p3, lr3, wg_pad, bg, B, S):
    TS = GLA_BLOCK
    NB = S // TS
    HB = GLA_HB
    kq, kk, kv = COL_QG // (HB * GLA_DK), COL_KG // (HB * GLA_DK), COL_VG // (HB * GLA_DV)

    def fwd(base):
        return lambda b, h, n: (b, n, base + h)

    def bwd(base):
        return lambda b, h, n: (b, NB - 1 - n, base + h)

    def specs(m, lr_map):
        return [
            pl.BlockSpec((1, TS, HB * GLA_DK), m(kq)),
            pl.BlockSpec((1, TS, HB * GLA_DK), m(kk)),
            pl.BlockSpec((1, TS, HB * GLA_DV), m(kv)),
            pl.BlockSpec((1, TS, LR_COLS), lr_map),
        ]

    return pl.pallas_call(
        _gla_kernel,
        grid=(B, GLA_HEADS // HB, NB),
        in_specs=specs(fwd, lambda b, h, n: (b, n, 0)) + specs(bwd, lambda b, h, n: (b, NB - 1 - n, 0)) + [
            pl.BlockSpec((2, LR_COLS, HB * GLA_DK), lambda b, h, n: (0, 0, h)),
            pl.BlockSpec((2, 1, HB * GLA_DK), lambda b, h, n: (0, 0, h)),
        ],
        out_specs=[
            pl.BlockSpec((1, TS, HB * GLA_DV), lambda b, h, n: (b, n, h)),
            pl.BlockSpec((1, TS, HB * GLA_DV), lambda b, h, n: (b, NB - 1 - n, h)),
        ],
        out_shape=[jax.ShapeDtypeStruct((B, S, GLA_VAL), BF16)] * 2,
        scratch_shapes=[
            pltpu.VMEM((HB, GLA_DV, GLA_DK), F32),
            pltpu.VMEM((HB, GLA_DV, GLA_DK), F32),
        ],
        compiler_params=pltpu.CompilerParams(
            dimension_semantics=("parallel", "parallel", "arbitrary"),
            vmem_limit_bytes=V7X_VMEM_LIMIT),
        name="gla",
    )(p3, p3, p3, lr3, p3, p3, p3, lr3, wg_pad, bg)


def _sigmoid(x):
    return 1.0 / (1.0 + jnp.exp(-x))


def _merge_kernel(o0_ref, o1_ref, o2_ref, l0_ref, l1_ref, l2_ref, of_ref, ob_ref, rg_ref, ga_ref, gb_ref,
                  x_ref, wba_ref, wbg_ref, wout_ref, gnorm_ref, nffn_ref, h_ref, hn_ref,
                  oa_s, og_s, oil_s, lil_s):
    tm = x_ref.shape[0]
    for gi, (o_ref, l_ref) in enumerate(((o1_ref, l1_ref), (o2_ref, l2_ref))):
        r = o_ref.shape[1]
        for c in range(r):
            dst = pl.ds(c, tm // r, stride=r)
            lil_s[gi, dst, :] = l_ref[0, c]
            for hh in range(HEADS_PER_GROUP):
                oil_s[gi, hh, dst, :] = o_ref[0, c, :, hh * HEAD_DIM:(hh + 1) * HEAD_DIM].astype(F32)

    mls = (l0_ref[0, 0], lil_s[0], lil_s[1])
    m = jnp.maximum(jnp.maximum(mls[0], mls[1]), mls[2])
    es = [jnp.exp(ml - m) for ml in mls]
    dens = [pltpu.roll(ml, HEAD_DIM - HEADS_PER_GROUP, 1) for ml in mls]
    inv = 1.0 / (es[0] * dens[0] + es[1] * dens[1] + es[2] * dens[2])
    w0, w1, w2 = es[0] * inv, es[1] * inv, es[2] * inv
    for hh in range(HEADS_PER_GROUP):
        hs = slice(hh * HEAD_DIM, (hh + 1) * HEAD_DIM)
        comb = (w0[:, hh:hh + 1] * o0_ref[0, 0, :, hs].astype(F32)
                + w1[:, hh:hh + 1] * oil_s[0, hh]
                + w2[:, hh:hh + 1] * oil_s[1, hh])
        oa_s[:, hs] = comb.astype(BF16)
    u_a = _dot(oa_s[...], wba_ref[...])

    for h in range(GLA_HEADS):
        vs = slice(h * GLA_DV, (h + 1) * GLA_DV)
        og = of_ref[:, vs].astype(F32) + ob_ref[:, vs].astype(F32)
        ms = jnp.mean(og * og, axis=-1, keepdims=True)
        ogn = og * lax.rsqrt(ms + EPS) * gnorm_ref[...]
        rg = rg_ref[:, vs].astype(F32)
        og_s[:, vs] = (ogn * (rg * _sigmoid(rg))).astype(BF16)
    u_b = _dot(og_s[...], wbg_ref[...])

    merged = _sigmoid(ga_ref[...].astype(F32)) * u_a + _sigmoid(gb_ref[...].astype(F32)) * u_b
    h = x_ref[...] + _dot(merged.astype(BF16), wout_ref[...])
    h_ref[...] = h
    ms = jnp.mean(h * h, axis=-1, keepdims=True)
    hn_ref[...] = (h * lax.rsqrt(ms + EPS) * nffn_ref[...]).astype(BF16)


def _merge(o_groups, ml_groups, o_fwd, o_bwd, p1, p2, x, wba, wbg, wout, gnorm, nffn, B, S, *, tm=256):
    t = x.shape[0]
    per_b = S // tm
    row = lambda b, i: (b * per_b + i, 0)
    const = lambda b, i: (0, 0)

    def resident(shape):
        return pl.BlockSpec(shape, const, pipeline_mode=pl.Buffered(1))

    def split_spec(r, width):
        return pl.BlockSpec((1, r, tm // r, width), lambda b, i: (b, 0, i, 0))

    rs = [r for _, r in ATTN_GROUPS]
    return pl.pallas_call(
        _merge_kernel,
        grid=(B, per_b),
        in_specs=[split_spec(r, ATTN_OUT) for r in rs] + [split_spec(r, HEAD_DIM) for r in rs] + [
            pl.BlockSpec((tm, GLA_VAL), row),
            pl.BlockSpec((tm, GLA_VAL), row),
            pl.BlockSpec((tm, GLA_VAL), lambda b, i: (b * per_b + i, COL_RG // GLA_VAL)),
            pl.BlockSpec((tm, D_MODEL), lambda b, i: (b * per_b + i, COL_GA // D_MODEL)),
            pl.BlockSpec((tm, D_MODEL), lambda b, i: (b * per_b + i, COL_GB // D_MODEL)),
            pl.BlockSpec((tm, D_MODEL), row),
            resident((ATTN_OUT, D_MODEL)),
            resident((GLA_VAL, D_MODEL)),
            resident((D_MODEL, D_MODEL)),
            resident((1, GLA_DV)),
            resident((1, D_MODEL)),
        ],
        out_specs=[pl.BlockSpec((tm, D_MODEL), row), pl.BlockSpec((tm, D_MODEL), row)],
        out_shape=[jax.ShapeDtypeStruct((t, D_MODEL), F32), jax.ShapeDtypeStruct((t, D_MODEL), BF16)],
        scratch_shapes=[
            pltpu.VMEM((tm, ATTN_OUT), BF16),
            pltpu.VMEM((tm, GLA_VAL), BF16),
            pltpu.VMEM((2, HEADS_PER_GROUP, tm, HEAD_DIM), F32),
            pltpu.VMEM((2, tm, HEAD_DIM), F32),
        ],
        compiler_params=pltpu.CompilerParams(
            dimension_semantics=("parallel", "parallel"),
            vmem_limit_bytes=V7X_VMEM_LIMIT),
        name="merge",
    )(*o_groups, *ml_groups, o_fwd, o_bwd, p1, p2, p2, x, wba, wbg, wout, gnorm, nffn)


def _ffn_kernel(hn_ref, w1_ref, w2_ref, h_ref, y_ref):
    @pl.when(pl.program_id(1) == 0)
    def _():
        y_ref[...] = h_ref[...]

    a = _dot(hn_ref[...], w1_ref[...])
    a = jnp.square(jnp.maximum(a, 0.0)).astype(BF16)
    y_ref[...] += _dot(a, w2_ref[...])


def _ffn(hn, h, w1, w2, *, tm=512, tf=1024):
    t = hn.shape[0]
    return pl.pallas_call(
        _ffn_kernel,
        grid=(t // tm, D_FF // tf),
        in_specs=[
            pl.BlockSpec((tm, D_MODEL), lambda i, j: (i, 0)),
            pl.BlockSpec((D_MODEL, tf), lambda i, j: (0, j)),
            pl.BlockSpec((tf, D_MODEL), lambda i, j: (j, 0)),
            pl.BlockSpec((tm, D_MODEL), lambda i, j: (i, 0)),
        ],
        out_specs=pl.BlockSpec((tm, D_MODEL), lambda i, j: (i, 0)),
        out_shape=jax.ShapeDtypeStruct((t, D_MODEL), F32),
        compiler_params=pltpu.CompilerParams(
            dimension_semantics=("parallel", "arbitrary"),
            vmem_limit_bytes=V7X_VMEM_LIMIT),
        name="ffn",
    )(hn, w1, w2, h)


def _rope_tables(s_max):
    inv_freq = ROPE_THETA ** (-jnp.arange(0, ROT_DIM, 2, dtype=F32) / ROT_DIM)
    ang = jnp.arange(s_max, dtype=F32)[:, None] * inv_freq[None, :]
    cos, sin = jnp.cos(ang), jnp.sin(ang)
    rest = HEAD_DIM - ROT_DIM
    cos_t = jnp.concatenate([cos, cos, jnp.ones((s_max, rest), F32)], axis=1)
    sin_t = jnp.concatenate([sin, sin, jnp.zeros((s_max, rest), F32)], axis=1)
    return cos_t, sin_t


def _prepare_layer(w_in, w_gla_gate, b_gla_gate):
    w_t = jnp.swapaxes(w_in, 0, 1)
    wg_pad = jnp.zeros((2, LR_COLS, GLA_KEY), F32)
    wg_pad = wg_pad.at[0, 0:GLA_RANK].set(w_gla_gate[0].astype(F32))
    wg_pad = wg_pad.at[1, GLA_RANK:2 * GLA_RANK].set(w_gla_gate[1].astype(F32)).astype(BF16)
    bg = b_gla_gate.astype(F32).reshape(2, 1, GLA_KEY)
    return w_t, wg_pad, bg


def _layer(x3, tables, norm_mix, prepared, q_norm, k_norm, gla_norm, wba, wbg, wout, norm_ffn, ffn_w):
    B, S, _ = x3.shape
    T = B * S
    w_t, wg_pad, bg = prepared
    tn = 1024
    x = x3.reshape(T, D_MODEL)
    xn, lr = _xnorm(x, norm_mix.reshape(1, D_MODEL), w_t)
    ffn_w = list(ffn_w)
    cast_axis = (1, 0)
    o_groups, ml_groups = [], []
    for g in range(N_GROUPS):
        r = ATTN_GROUPS[g][1]
        pending = [k for k in range(2) if ffn_w[k].dtype != BF16]
        side = (ffn_w[pending[0]], cast_axis[pending[0]]) if r > 1 and pending else None
        a_g = _proj(xn, w_t, lambda j, g=g: j * ATTN_QKV + g * ATTN_OUT, 3, r, B, S, f"proj_attn_r{r}", tn=tn,
                    tm=PROJ_TM if r == 1 else PROJ_TM_SPLIT, side_cast=side)
        if side is not None:
            a_g, ffn_w[pending[0]] = a_g
        o_g, ml_g = _attention_group(a_g.reshape(B, r, S // r, 3 * ATTN_OUT), tables, q_norm, k_norm, g, B, S)
        o_groups.append(o_g)
        ml_groups.append(ml_g)
    p1 = _proj(xn, w_t, lambda j: W_COL_QG + j * tn, P1_COLS // tn, 1, B, S, "proj_gla", tn=tn, tm=PROJ_TM)
    p2 = _proj(xn, w_t, lambda j: W_COL_GATES + j * tn, 2 * D_MODEL // tn, 1, B, S, "proj_gates", tn=tn, tm=PROJ_TM)
    o_fwd, o_bwd = _gla(p1.reshape(B, S, P1_COLS), lr.reshape(B, S, LR_COLS), wg_pad, bg, B, S)
    h, hn = _merge(o_groups, ml_groups, o_fwd.reshape(T, GLA_VAL), o_bwd.reshape(T, GLA_VAL), p1, p2, x,
                   wba, wbg, wout, gla_norm.reshape(1, GLA_DV), norm_ffn.reshape(1, D_MODEL), B, S)
    y = _ffn(hn, h, ffn_w[0].astype(BF16), ffn_w[1].astype(BF16))
    return y.reshape(B, S, D_MODEL), tuple(ffn_w)


def kernel(x_prompt, x_sample, norm_mix, w_in, q_norm, k_norm, w_gla_gate, b_gla_gate, gla_norm,
           w_branch_attn, w_branch_gla, w_out, norm_ffn, w_ff1, w_ff2):
    depth = w_in.shape[0]
    tables = _rope_tables(max(x_prompt.shape[1], x_sample.shape[1]))
    layers = []
    for l in range(depth):
        layers.append((
            norm_mix[l], _prepare_layer(w_in[l], w_gla_gate[l], b_gla_gate[l]),
            q_norm[l].astype(F32), k_norm[l].astype(F32), gla_norm[l].astype(F32),
            w_branch_attn[l].astype(BF16), w_branch_gla[l].astype(BF16), w_out[l].astype(BF16),
            norm_ffn[l].astype(F32)))
    ffn_ws = [(w_ff1[l], w_ff2[l]) for l in range(depth)]
    outs = []
    for x in (x_prompt, x_sample):
        for l, layer in enumerate(layers):
            x, ffn_ws[l] = _layer(x, tables, *layer, ffn_ws[l])
        outs.append(x)
    return tuple(outs)
```

```python
import functools

import jax
import jax.numpy as jnp
from jax import lax
from jax.experimental import pallas as pl
from jax.experimental.pallas import tpu as pltpu

F32 = jnp.float32
BF16 = jnp.bfloat16

D_MODEL = 2048
HEAD_DIM = 128
ATTN_GROUPS = ((128, 1), (512, 4), (2048, 16))
N_GROUPS = 3
HEADS_PER_GROUP = 8
ATTN_QKV = N_GROUPS * HEADS_PER_GROUP * HEAD_DIM
ATTN_OUT = HEADS_PER_GROUP * HEAD_DIM
ROT_DIM = HEAD_DIM // 4
ROPE_THETA = 500000.0
GLA_HEADS = 4
GLA_KEY = 1024
GLA_VAL = 2048
GLA_DK = 256
GLA_DV = 512
GLA_RANK = 16
GLA_NORMALIZER = 16.0
D_FF = 4 * D_MODEL
EPS = 1e-6

W_COL_QG = 3 * ATTN_QKV
W_COL_LR = W_COL_QG + 2 * GLA_KEY + 2 * GLA_VAL
W_COL_GATES = W_COL_LR + 2 * GLA_RANK
COL_QG = 0
COL_KG = 1024
COL_VG = 2048
COL_RG = 4096
P1_COLS = 6144
COL_GA = 0
COL_GB = 2048
LR_COLS = 128

V7X_VMEM_LIMIT = 56 * 1024 * 1024
NEG = -1e30

PROJ_TM = 2048
PROJ_TM_SPLIT = 1024
PERM_ROWS = 256
SIDE_CAST_BLOCKS = 16
ATTN_HALF = 64
ATTN_TQ = 128
ATTN_CHAINS = 32
GLA_BLOCK = 256
GLA_SUB = 64
GLA_HB = 4


def _dot(a, b):
    return jnp.dot(a, b, preferred_element_type=F32)


def _dot_nt(a, b):
    return lax.dot_general(a, b, (((1,), (1,)), ((), ())), preferred_element_type=F32)


def _dot_tn(a, b):
    return lax.dot_general(a, b, (((0,), (0,)), ((), ())), preferred_element_type=F32)


def _split_bf16(x):
    hi = x.astype(BF16)
    lo = (x - hi.astype(F32)).astype(BF16)
    return hi, lo


def _xnorm_kernel(x_ref, gain_ref, wlr_ref, xn_ref, lr_ref):
    x = x_ref[...]
    ms = jnp.mean(x * x, axis=-1, keepdims=True)
    xn = (x * lax.rsqrt(ms + EPS) * gain_ref[...]).astype(BF16)
    xn_ref[...] = xn
    lr_ref[...] = _dot_nt(xn, wlr_ref[...].astype(BF16))


def _xnorm(x, gain, w_t, *, tm=512):
    t = x.shape[0]
    return pl.pallas_call(
        _xnorm_kernel,
        grid=(t // tm,),
        in_specs=[
            pl.BlockSpec((tm, D_MODEL), lambda i: (i, 0)),
            pl.BlockSpec((1, D_MODEL), lambda i: (0, 0)),
            pl.BlockSpec((pl.Element(LR_COLS), pl.Element(D_MODEL)), lambda i: (W_COL_LR, 0)),
        ],
        out_specs=[
            pl.BlockSpec((tm, D_MODEL), lambda i: (i, 0)),
            pl.BlockSpec((tm, LR_COLS), lambda i: (i, 0)),
        ],
        out_shape=[
            jax.ShapeDtypeStruct((t, D_MODEL), BF16),
            jax.ShapeDtypeStruct((t, LR_COLS), F32),
        ],
        compiler_params=pltpu.CompilerParams(
            dimension_semantics=("parallel",),
            vmem_limit_bytes=V7X_VMEM_LIMIT),
        name="xnorm",
    )(x, gain, w_t)


def _proj_kernel(*refs, r, side_cast):
    if side_cast:
        xn_ref, w_ref, cast_in_ref, o_ref, cast_out_ref, wb_s = refs
        cast_out_ref[...] = cast_in_ref[...].astype(BF16)
    else:
        xn_ref, w_ref, o_ref, wb_s = refs

    @pl.when(pl.program_id(1) == 0)
    def _():
        wb_s[...] = w_ref[...].astype(BF16)

    acc = _dot_nt(xn_ref[...], wb_s[...]).astype(BF16)
    if r == 1:
        o_ref[...] = acc
        return
    n = PERM_ROWS // r
    dst = lax.broadcasted_iota(jnp.int32, (PERM_ROWS, PERM_ROWS), 0)
    src = lax.broadcasted_iota(jnp.int32, (PERM_ROWS, PERM_ROWS), 1)
    perm = jnp.where(src == (dst % n) * r + dst // n, 1.0, 0.0).astype(BF16)
    for g in range(acc.shape[0] // PERM_ROWS):
        grouped = _dot(perm, acc[g * PERM_ROWS:(g + 1) * PERM_ROWS, :]).astype(BF16)
        for c in range(r):
            o_ref[0, c, g * n:(g + 1) * n, :] = grouped[c * n:(c + 1) * n, :]


def _proj(xn, w_t, row_start, ncol_blocks, r, B, S, name, *, tm=1024, tn=1024, side_cast=None):
    t = xn.shape[0]
    per_b = S // tm
    ntiles = t // tm
    if r == 1:
        out_specs = [pl.BlockSpec((tm, tn), lambda j, i: (i, j))]
        out_shapes = [jax.ShapeDtypeStruct((t, ncol_blocks * tn), BF16)]
    else:
        out_specs = [pl.BlockSpec((1, r, tm // r, tn), lambda j, i: (i // per_b, 0, i % per_b, j))]
        out_shapes = [jax.ShapeDtypeStruct((B, r, S // r, ncol_blocks * tn), BF16)]
    in_specs = [
        pl.BlockSpec((tm, D_MODEL), lambda j, i: (i, 0)),
        pl.BlockSpec((pl.Element(tn), pl.Element(D_MODEL)), lambda j, i: (pl.multiple_of(row_start(j), 8), 0)),
    ]
    operands = [xn, w_t]
    if side_cast is not None:
        w, axis = side_cast
        assert ncol_blocks * ntiles >= SIDE_CAST_BLOCKS
        blk = list(w.shape)
        blk[axis] //= SIDE_CAST_BLOCKS

        def cast_map(j, i):
            idx = [0, 0]
            idx[axis] = jnp.minimum(j * ntiles + i, SIDE_CAST_BLOCKS - 1)
            return tuple(idx)

        in_specs.append(pl.BlockSpec(tuple(blk), cast_map))
        out_specs.append(pl.BlockSpec(tuple(blk), cast_map))
        out_shapes.append(jax.ShapeDtypeStruct(w.shape, BF16))
        operands.append(w)
    outs = pl.pallas_call(
        functools.partial(_proj_kernel, r=r, side_cast=side_cast is not None),
        grid=(ncol_blocks, ntiles),
        in_specs=in_specs,
        out_specs=out_specs,
        out_shape=out_shapes,
        scratch_shapes=[pltpu.VMEM((tn, D_MODEL), BF16)],
        compiler_params=pltpu.CompilerParams(
            dimension_semantics=("arbitrary", "arbitrary"),
            vmem_limit_bytes=V7X_VMEM_LIMIT),
        name=name,
    )(*operands)
    return outs[0] if side_cast is None else tuple(outs)


def _attn_kernel(q_ref, k_ref, v_ref, cos_ref, sin_ref, qg_ref, kg_ref,
                 o_ref, ml_ref, qs, ks, vs, bias_s, *, L, hb, U):
    TQ, HALF = ATTN_TQ, ATTN_HALF
    TK = TQ + 2 * HALF
    NT = L // TQ
    NI = NT // U
    hblk = pl.program_id(2)

    @pl.when(hblk == 0)
    def _():
        ml_ref[...] = jnp.zeros_like(ml_ref)

    ri = lax.broadcasted_iota(jnp.int32, (TQ, TK), 0)
    ci = lax.broadcasted_iota(jnp.int32, (TQ, TK), 1)
    d = ci - ri
    band = jnp.where(d < 0, NEG, jnp.where(d > 2 * HALF, NEG, 0.0)).astype(F32)
    first = jnp.where(ci < HALF, NEG, band)
    bias_s[0] = band
    bias_s[1] = first
    bias_s[2] = jnp.where(ci >= TQ + HALF, NEG, band)
    bias_s[3] = jnp.where(ci >= TQ + HALF, NEG, first)

    a = lax.broadcasted_iota(jnp.int32, (HEAD_DIM, HEAD_DIM), 0)
    b = lax.broadcasted_iota(jnp.int32, (HEAD_DIM, HEAD_DIM), 1)
    half = ROT_DIM // 2
    ones_m = jnp.ones((HEAD_DIM, HEAD_DIM), BF16)
    rot_m = jnp.where((b < half) & (a == b + half), -1.0,
                      jnp.where((b >= half) & (b < ROT_DIM) & (a == b - half), 1.0, 0.0)).astype(BF16)

    zpad = jnp.zeros((HALF, HEAD_DIM), BF16)
    ztail = jnp.zeros((TQ, HEAD_DIM), BF16)
    for hh in range(hb):
        ks[hh, 0:HALF, :] = zpad
        ks[hh, L + HALF:L + HALF + TQ, :] = ztail
        vs[hh, 0:HALF, 0:HEAD_DIM] = zpad
        vs[hh, L + HALF:L + HALF + TQ, 0:HEAD_DIM] = ztail
        vs[hh, :, HEAD_DIM:2 * HEAD_DIM] = jnp.ones((vs.shape[1], HEAD_DIM), BF16)

    qgain = qg_ref[...] * (HEAD_DIM ** -0.5)
    kgain = kg_ref[...]

    def norm_rope(x, gain, cos, sin):
        ssq = _dot((x * x).astype(BF16), ones_m)
        xn = x * lax.rsqrt(ssq * (1.0 / HEAD_DIM) + EPS) * gain
        return xn * cos + _dot(xn.astype(BF16), rot_m) * sin

    def prep(blk):
        if isinstance(blk, int):
            src, dst, koff = min(blk, NT - 1) * TQ, blk * TQ, blk * TQ + HALF
        else:
            src = pl.multiple_of(jnp.minimum(blk, NT - 1) * TQ, TQ)
            dst = pl.multiple_of(blk * TQ, TQ)
            koff = pl.multiple_of(dst + HALF, HALF)
        rows = pl.ds(src, TQ)
        qdst = pl.ds(dst, TQ)
        kdst = pl.ds(koff, TQ)
        cos, sin = cos_ref[rows, :], sin_ref[rows, :]
        for hh in range(hb):
            hs = slice(hh * HEAD_DIM, (hh + 1) * HEAD_DIM)
            qs[hh, qdst, :] = norm_rope(q_ref[0, 0, rows, hs].astype(F32), qgain, cos, sin).astype(BF16)
            ks[hh, kdst, :] = norm_rope(k_ref[0, 0, rows, hs].astype(F32), kgain, cos, sin).astype(BF16)
            vs[hh, kdst, 0:HEAD_DIM] = v_ref[0, 0, rows, hs]

    lane = lax.broadcasted_iota(jnp.int32, (TQ, HEAD_DIM), 1)

    def score_tiles(tt):
        for u in range(U):
            t = tt * U + u
            q0 = t * TQ if isinstance(t, int) else pl.multiple_of(t * TQ, TQ)
            qrows = pl.ds(q0, TQ)
            krows = pl.ds(q0, TK)
            bias = bias_s[jnp.where(t == 0, 1, 0) + jnp.where(t == NT - 1, 2, 0)]
            ml_tile = ml_ref[0, 0, qrows, :]
            for hh in range(hb):
                hs = slice(hh * HEAD_DIM, (hh + 1) * HEAD_DIM)
                h = hblk * hb + hh
                s = _dot_nt(qs[hh, qrows, :], ks[hh, krows, :]) + bias
                m = jnp.max(s, axis=-1, keepdims=True)
                p = jnp.exp(s - m).astype(BF16)
                acc = _dot(p, vs[hh, krows, :])
                o_ref[0, 0, qrows, hs] = acc[:, :HEAD_DIM].astype(BF16)
                ml_tile = jnp.where(lane == h, m, jnp.where(lane == HEADS_PER_GROUP + h, acc[:, HEAD_DIM:], ml_tile))
            ml_ref[0, 0, qrows, :] = ml_tile

    for blk in range(min(U + 1, NT)):
        prep(blk)

    def body(tt, carry):
        score_tiles(tt)
        for u in range(U):
            prep((tt + 1) * U + 1 + u)
        return carry

    lax.fori_loop(0, NI - 1, body, 0)
    score_tiles(NI - 1)


def _attention_group(a_g, tables, q_gain, k_gain, g, B, S):
    window, r = ATTN_GROUPS[g]
    assert window // (2 * r) == ATTN_HALF
    L = S // r
    assert L % ATTN_TQ == 0
    hb = max(1, min(HEADS_PER_GROUP, 8192 // L))
    nhb = HEADS_PER_GROUP // hb
    bw = hb * HEAD_DIM
    nt = L // ATTN_TQ
    U = max(1, min(ATTN_CHAINS // hb, nt // 2 if nt >= 4 else nt))
    assert (L // ATTN_TQ) % U == 0
    cos, sin = (t[:S].reshape(L, r * HEAD_DIM) for t in tables)

    def col_map(part):
        return lambda b, c, h: (b, c, 0, part * nhb + h)

    tab_spec = pl.BlockSpec((L, HEAD_DIM), lambda b, c, h: (0, c))
    gain_spec = pl.BlockSpec((1, HEAD_DIM), lambda b, c, h: (0, 0))
    return pl.pallas_call(
        functools.partial(_attn_kernel, L=L, hb=hb, U=U),
        grid=(B, r, nhb),
        in_specs=[
            pl.BlockSpec((1, 1, L, bw), col_map(0)),
            pl.BlockSpec((1, 1, L, bw), col_map(1)),
            pl.BlockSpec((1, 1, L, bw), col_map(2)),
            tab_spec, tab_spec, gain_spec, gain_spec,
        ],
        out_specs=[
            pl.BlockSpec((1, 1, L, bw), lambda b, c, h: (b, c, 0, h)),
            pl.BlockSpec((1, 1, L, HEAD_DIM), lambda b, c, h: (b, c, 0, 0)),
        ],
        out_shape=[
            jax.ShapeDtypeStruct((B, r, L, ATTN_OUT), BF16),
            jax.ShapeDtypeStruct((B, r, L, HEAD_DIM), F32),
        ],
        scratch_shapes=[
            pltpu.VMEM((hb, L + ATTN_TQ, HEAD_DIM), BF16),
            pltpu.VMEM((hb, L + ATTN_TQ + ATTN_HALF, HEAD_DIM), BF16),
            pltpu.VMEM((hb, L + ATTN_TQ + ATTN_HALF, 2 * HEAD_DIM), BF16),
            pltpu.VMEM((4, ATTN_TQ, ATTN_TQ + 2 * ATTN_HALF), F32),
        ],
        compiler_params=pltpu.CompilerParams(
            dimension_semantics=("parallel", "parallel", "arbitrary"),
            vmem_limit_bytes=V7X_VMEM_LIMIT),
        name=f"attn_g{g}",
    )(a_g, a_g, a_g, cos, sin, q_gain[g:g + 1], k_gain[g:g + 1])


def _gla_kernel(qf_ref, kf_ref, vf_ref, lrf_ref, qb_ref, kb_ref, vb_ref, lrb_ref, wg_ref, bg_ref,
                of_ref, ob_ref, stf, stb):
    C, SC = GLA_BLOCK, GLA_SUB
    NS = C // SC
    n = pl.program_id(2)

    @pl.when(n == 0)
    def _():
        stf[...] = jnp.zeros_like(stf)
        stb[...] = jnp.zeros_like(stb)

    ri = lax.broadcasted_iota(jnp.int32, (C, C), 0)
    ci = lax.broadcasted_iota(jnp.int32, (C, C), 1)

    def direction(q_ref, k_ref, v_ref, lr_ref, o_ref, st, d, backward, h):
        ks = slice(h * GLA_DK, (h + 1) * GLA_DK)
        vs = slice(h * GLA_DV, (h + 1) * GLA_DV)
        z = _dot(lr_ref[0].astype(BF16), wg_ref[d, :, ks]) + bg_ref[d, :, ks]
        yield
        log_sig = jnp.minimum(z, 0.0) - jnp.log(1.0 + jnp.exp(-jnp.abs(z)))
        g_hi, g_lo = _split_bf16(log_sig * (1.0 / GLA_NORMALIZER))
        tri = jnp.where((ci >= ri) if backward else (ri >= ci), 1.0, 0.0).astype(BF16)
        cum = _dot(tri, g_hi) + _dot(tri, g_lo)
        yield
        mid_row = SC // 2 if backward else SC // 2 - 1
        end_row = 0 if backward else C - 1
        mids = [cum[I * SC + mid_row:I * SC + mid_row + 1, :] for I in range(NS)]
        end = cum[end_row:end_row + 1, :]
        sub = [slice(I * SC, (I + 1) * SC) for I in range(NS)]
        dl = jnp.concatenate([cum[sub[I], :] - mids[I] for I in range(NS)], axis=0)
        qd = q_ref[0, :, ks].astype(F32) * (GLA_DK ** -0.5) * jnp.exp(dl)
        kd = k_ref[0, :, ks].astype(F32) * jnp.exp(-dl)
        qd_b = qd.astype(BF16)
        kd_b = kd.astype(BF16)
        att_rows = []
        for I in range(NS):
            blocks = []
            for J in range(NS):
                if (J > I) if backward else (J < I):
                    blocks.append((kd[sub[J], :] * jnp.exp(mids[I] - mids[J])).astype(BF16))
                else:
                    blocks.append(kd_b[sub[J], :])
            att_rows.append(_dot_nt(qd_b[sub[I], :], jnp.concatenate(blocks, axis=0)))
        yield
        mask = (ci > ri) if backward else (ri >= ci)
        att = jnp.where(mask, jnp.concatenate(att_rows, axis=0), 0.0).astype(BF16)
        qi = jnp.concatenate([qd[sub[I], :] * jnp.exp(mids[I]) for I in range(NS)], axis=0).astype(BF16)
        k2 = jnp.concatenate([kd[sub[I], :] * jnp.exp(end - mids[I]) for I in range(NS)], axis=0).astype(BF16)
        v = v_ref[0, :, vs]
        state = st[h]
        o = _dot(att, v) + _dot_nt(qi, state.astype(BF16))
        upd = _dot_tn(v, k2)
        yield
        o_ref[0, :, vs] = o.astype(BF16)
        st[h] = state * jnp.exp(end) + upd

    chains = []
    for h in range(GLA_HB):
        chains.append(direction(qf_ref, kf_ref, vf_ref, lrf_ref, of_ref, stf, 0, False, h))
        chains.append(direction(qb_ref, kb_ref, vb_ref, lrb_ref, ob_ref, stb, 1, True, h))
    while chains:
        alive = []
        for chain in chains:
            if next(chain, chain) is not chain:
                alive.append(chain)
        chains = alive


def _gla(p3, lr3, wg_pad, bg, B, S):
    TS = GLA_BLOCK
    NB = S // TS
    HB = GLA_HB
    kq, kk, kv = COL_QG // (HB * GLA_DK), COL_KG // (HB * GLA_DK), COL_VG // (HB * GLA_DV)

    def fwd(base):
        return lambda b, h, n: (b, n, base + h)

    def bwd(base):
        return lambda b, h, n: (b, NB - 1 - n, base + h)

    def specs(m, lr_map):
        return [
            pl.BlockSpec((1, TS, HB * GLA_DK), m(kq)),
            pl.BlockSpec((1, TS, HB * GLA_DK), m(kk)),
            pl.BlockSpec((1, TS, HB * GLA_DV), m(kv)),
            pl.BlockSpec((1, TS, LR_COLS), lr_map),
        ]

    return pl.pallas_call(
        _gla_kernel,
        grid=(B, GLA_HEADS // HB, NB),
        in_specs=specs(fwd, lambda b, h, n: (b, n, 0)) + specs(bwd, lambda b, h, n: (b, NB - 1 - n, 0)) + [
            pl.BlockSpec((2, LR_COLS, HB * GLA_DK), lambda b, h, n: (0, 0, h)),
            pl.BlockSpec((2, 1, HB * GLA_DK), lambda b, h, n: (0, 0, h)),
        ],
        out_specs=[
            pl.BlockSpec((1, TS, HB * GLA_DV), lambda b, h, n: (b, n, h)),
            pl.BlockSpec((1, TS, HB * GLA_DV), lambda b, h, n: (b, NB - 1 - n, h)),
        ],
        out_shape=[jax.ShapeDtypeStruct((B, S, GLA_VAL), BF16)] * 2,
        scratch_shapes=[
            pltpu.VMEM((HB, GLA_DV, GLA_DK), F32),
            pltpu.VMEM((HB, GLA_DV, GLA_DK), F32),
        ],
        compiler_params=pltpu.CompilerParams(
            dimension_semantics=("parallel", "parallel", "arbitrary"),
            vmem_limit_bytes=V7X_VMEM_LIMIT),
        name="gla",
    )(p3, p3, p3, lr3, p3, p3, p3, lr3, wg_pad, bg)


def _sigmoid(x):
    return 1.0 / (1.0 + jnp.exp(-x))


def _merge_kernel(o0_ref, o1_ref, o2_ref, l0_ref, l1_ref, l2_ref, of_ref, ob_ref, rg_ref, ga_ref, gb_ref,
                  x_ref, wba_ref, wbg_ref, wout_ref, gnorm_ref, nffn_ref, h_ref, hn_ref,
                  oa_s, og_s, oil_s, lil_s):
    tm = x_ref.shape[0]
    for gi, (o_ref, l_ref) in enumerate(((o1_ref, l1_ref), (o2_ref, l2_ref))):
        r = o_ref.shape[1]
        for c in range(r):
            dst = pl.ds(c, tm // r, stride=r)
            lil_s[gi, dst, :] = l_ref[0, c]
            for hh in range(HEADS_PER_GROUP):
                oil_s[gi, hh, dst, :] = o_ref[0, c, :, hh * HEAD_DIM:(hh + 1) * HEAD_DIM].astype(F32)

    mls = (l0_ref[0, 0], lil_s[0], lil_s[1])
    m = jnp.maximum(jnp.maximum(mls[0], mls[1]), mls[2])
    es = [jnp.exp(ml - m) for ml in mls]
    dens = [pltpu.roll(ml, HEAD_DIM - HEADS_PER_GROUP, 1) for ml in mls]
    inv = 1.0 / (es[0] * dens[0] + es[1] * dens[1] + es[2] * dens[2])
    w0, w1, w2 = es[0] * inv, es[1] * inv, es[2] * inv
    for hh in range(HEADS_PER_GROUP):
        hs = slice(hh * HEAD_DIM, (hh + 1) * HEAD_DIM)
        comb = (w0[:, hh:hh + 1] * o0_ref[0, 0, :, hs].astype(F32)
                + w1[:, hh:hh + 1] * oil_s[0, hh]
                + w2[:, hh:hh + 1] * oil_s[1, hh])
        oa_s[:, hs] = comb.astype(BF16)
    u_a = _dot(oa_s[...], wba_ref[...])

    for h in range(GLA_HEADS):
        vs = slice(h * GLA_DV, (h + 1) * GLA_DV)
        og = of_ref[:, vs].astype(F32) + ob_ref[:, vs].astype(F32)
        ms = jnp.mean(og * og, axis=-1, keepdims=True)
        ogn = og * lax.rsqrt(ms + EPS) * gnorm_ref[...]
        rg = rg_ref[:, vs].astype(F32)
        og_s[:, vs] = (ogn * (rg * _sigmoid(rg))).astype(BF16)
    u_b = _dot(og_s[...], wbg_ref[...])

    merged = _sigmoid(ga_ref[...].astype(F32)) * u_a + _sigmoid(gb_ref[...].astype(F32)) * u_b
    h = x_ref[...] + _dot(merged.astype(BF16), wout_ref[...])
    h_ref[...] = h
    ms = jnp.mean(h * h, axis=-1, keepdims=True)
    hn_ref[...] = (h * lax.rsqrt(ms + EPS) * nffn_ref[...]).astype(BF16)


def _merge(o_groups, ml_groups, o_fwd, o_bwd, p1, p2, x, wba, wbg, wout, gnorm, nffn, B, S, *, tm=256):
    t = x.shape[0]
    per_b = S // tm
    row = lambda b, i: (b * per_b + i, 0)
    const = lambda b, i: (0, 0)

    def resident(shape):
        return pl.BlockSpec(shape, const, pipeline_mode=pl.Buffered(1))

    def split_spec(r, width):
        return pl.BlockSpec((1, r, tm // r, width), lambda b, i: (b, 0, i, 0))

    rs = [r for _, r in ATTN_GROUPS]
    return pl.pallas_call(
        _merge_kernel,
        grid=(B, per_b),
        in_specs=[split_spec(r, ATTN_OUT) for r in rs] + [split_spec(r, HEAD_DIM) for r in rs] + [
            pl.BlockSpec((tm, GLA_VAL), row),
            pl.BlockSpec((tm, GLA_VAL), row),
            pl.BlockSpec((tm, GLA_VAL), lambda b, i: (b * per_b + i, COL_RG // GLA_VAL)),
            pl.BlockSpec((tm, D_MODEL), lambda b, i: (b * per_b + i, COL_GA // D_MODEL)),
            pl.BlockSpec((tm, D_MODEL), lambda b, i: (b * per_b + i, COL_GB // D_MODEL)),
            pl.BlockSpec((tm, D_MODEL), row),
            resident((ATTN_OUT, D_MODEL)),
            resident((GLA_VAL, D_MODEL)),
            resident((D_MODEL, D_MODEL)),
            resident((1, GLA_DV)),
            resident((1, D_MODEL)),
        ],
        out_specs=[pl.BlockSpec((tm, D_MODEL), row), pl.BlockSpec((tm, D_MODEL), row)],
        out_shape=[jax.ShapeDtypeStruct((t, D_MODEL), F32), jax.ShapeDtypeStruct((t, D_MODEL), BF16)],
        scratch_shapes=[
            pltpu.VMEM((tm, ATTN_OUT), BF16),
            pltpu.VMEM((tm, GLA_VAL), BF16),
            pltpu.VMEM((2, HEADS_PER_GROUP, tm, HEAD_DIM), F32),
            pltpu.VMEM((2, tm, HEAD_DIM), F32),
        ],
        compiler_params=pltpu.CompilerParams(
            dimension_semantics=("parallel", "parallel"),
            vmem_limit_bytes=V7X_VMEM_LIMIT),
        name="merge",
    )(*o_groups, *ml_groups, o_fwd, o_bwd, p1, p2, p2, x, wba, wbg, wout, gnorm, nffn)


def _ffn_kernel(hn_ref, w1_ref, w2_ref, h_ref, y_ref):
    @pl.when(pl.program_id(1) == 0)
    def _():
        y_ref[...] = h_ref[...]

    a = _dot(hn_ref[...], w1_ref[...])
    a = jnp.square(jnp.maximum(a, 0.0)).astype(BF16)
    y_ref[...] += _dot(a, w2_ref[...])


def _ffn(hn, h, w1, w2, *, tm=512, tf=1024):
    t = hn.shape[0]
    return pl.pallas_call(
        _ffn_kernel,
        grid=(t // tm, D_FF // tf),
        in_specs=[
            pl.BlockSpec((tm, D_MODEL), lambda i, j: (i, 0)),
            pl.BlockSpec((D_MODEL, tf), lambda i, j: (0, j)),
            pl.BlockSpec((tf, D_MODEL), lambda i, j: (j, 0)),
            pl.BlockSpec((tm, D_MODEL), lambda i, j: (i, 0)),
        ],
        out_specs=pl.BlockSpec((tm, D_MODEL), lambda i, j: (i, 0)),
        out_shape=jax.ShapeDtypeStruct((t, D_MODEL), F32),
        compiler_params=pltpu.CompilerParams(
            dimension_semantics=("parallel", "arbitrary"),
            vmem_limit_bytes=V7X_VMEM_LIMIT),
        name="ffn",
    )(hn, w1, w2, h)


def _rope_tables(s_max):
    inv_freq = ROPE_THETA ** (-jnp.arange(0, ROT_DIM, 2, dtype=F32) / ROT_DIM)
    ang = jnp.arange(s_max, dtype=F32)[:, None] * inv_freq[None, :]
    cos, sin = jnp.cos(ang), jnp.sin(ang)
    rest = HEAD_DIM - ROT_DIM
    cos_t = jnp.concatenate([cos, cos, jnp.ones((s_max, rest), F32)], axis=1)
    sin_t = jnp.concatenate([sin, sin, jnp.zeros((s_max, rest), F32)], axis=1)
    return cos_t, sin_t


def _prepare_layer(w_in, w_gla_gate, b_gla_gate):
    w_t = jnp.swapaxes(w_in, 0, 1)
    wg_pad = jnp.zeros((2, LR_COLS, GLA_KEY), F32)
    wg_pad = wg_pad.at[0, 0:GLA_RANK].set(w_gla_gate[0].astype(F32))
    wg_pad = wg_pad.at[1, GLA_RANK:2 * GLA_RANK].set(w_gla_gate[1].astype(F32)).astype(BF16)
    bg = b_gla_gate.astype(F32).reshape(2, 1, GLA_KEY)
    return w_t, wg_pad, bg


def _layer(x3, tables, norm_mix, prepared, q_norm, k_norm, gla_norm, wba, wbg, wout, norm_ffn, ffn_w):
    B, S, _ = x3.shape
    T = B * S
    w_t, wg_pad, bg = prepared
    tn = 1024
    x = x3.reshape(T, D_MODEL)
    xn, lr = _xnorm(x, norm_mix.reshape(1, D_MODEL), w_t)
    ffn_w = list(ffn_w)
    cast_axis = (1, 0)
    o_groups, ml_groups = [], []
    for g in range(N_GROUPS):
        r = ATTN_GROUPS[g][1]
        pending = [k for k in range(2) if ffn_w[k].dtype != BF16]
        side = (ffn_w[pending[0]], cast_axis[pending[0]]) if r > 1 and pending else None
        a_g = _proj(xn, w_t, lambda j, g=g: j * ATTN_QKV + g * ATTN_OUT, 3, r, B, S, f"proj_attn_r{r}", tn=tn,
                    tm=PROJ_TM if r == 1 else PROJ_TM_SPLIT, side_cast=side)
        if side is not None:
            a_g, ffn_w[pending[0]] = a_g
        o_g, ml_g = _attention_group(a_g.reshape(B, r, S // r, 3 * ATTN_OUT), tables, q_norm, k_norm, g, B, S)
        o_groups.append(o_g)
        ml_groups.append(ml_g)
    p1 = _proj(xn, w_t, lambda j: W_COL_QG + j * tn, P1_COLS // tn, 1, B, S, "proj_gla", tn=tn, tm=PROJ_TM)
    p2 = _proj(xn, w_t, lambda j: W_COL_GATES + j * tn, 2 * D_MODEL // tn, 1, B, S, "proj_gates", tn=tn, tm=PROJ_TM)
    o_fwd, o_bwd = _gla(p1.reshape(B, S, P1_COLS), lr.reshape(B, S, LR_COLS), wg_pad, bg, B, S)
    h, hn = _merge(o_groups, ml_groups, o_fwd.reshape(T, GLA_VAL), o_bwd.reshape(T, GLA_VAL), p1, p2, x,
                   wba, wbg, wout, gla_norm.reshape(1, GLA_DV), norm_ffn.reshape(1, D_MODEL), B, S)
    y = _ffn(hn, h, ffn_w[0].astype(BF16), ffn_w[1].astype(BF16))
    return y.reshape(B, S, D_MODEL), tuple(ffn_w)


def kernel(x_prompt, x_sample, norm_mix, w_in, q_norm, k_norm, w_gla_gate, b_gla_gate, gla_norm,
           w_branch_attn, w_branch_gla, w_out, norm_ffn, w_ff1, w_ff2):
    depth = w_in.shape[0]
    tables = _rope_tables(max(x_prompt.shape[1], x_sample.shape[1]))
    layers = []
    for l in range(depth):
        layers.append((
            norm_mix[l], _prepare_layer(w_in[l], w_gla_gate[l], b_gla_gate[l]),
            q_norm[l].astype(F32), k_norm[l].astype(F32), gla_norm[l].astype(F32),
            w_branch_attn[l].astype(BF16), w_branch_gla[l].astype(BF16), w_out[l].astype(BF16),
            norm_ffn[l].astype(F32)))
    ffn_ws = [(w_ff1[l], w_ff2[l]) for l in range(depth)]
    outs = []
    for x in (x_prompt, x_sample):
        for l, layer in enumerate(layers):
            x, ffn_ws[l] = _layer(x, tables, *layer, ffn_ws[l])
        outs.append(x)
    return tuple(outs)
```

```python
import functools

import jax
import jax.numpy as jnp
from jax import lax
from jax.experimental import pallas as pl
from jax.experimental.pallas import tpu as pltpu

F32 = jnp.float32
BF16 = jnp.bfloat16

D_MODEL = 2048
HEAD_DIM = 128
ATTN_GROUPS = ((128, 1), (512, 4), (2048, 16))
N_GROUPS = 3
HEADS_PER_GROUP = 8
ATTN_QKV = N_GROUPS * HEADS_PER_GROUP * HEAD_DIM
ATTN_OUT = HEADS_PER_GROUP * HEAD_DIM
ROT_DIM = HEAD_DIM // 4
ROPE_THETA = 500000.0
GLA_HEADS = 4
GLA_KEY = 1024
GLA_VAL = 2048
GLA_DK = 256
GLA_DV = 512
GLA_RANK = 16
GLA_NORMALIZER = 16.0
D_FF = 4 * D_MODEL
EPS = 1e-6

W_COL_QG = 3 * ATTN_QKV
W_COL_LR = W_COL_QG + 2 * GLA_KEY + 2 * GLA_VAL
W_COL_GATES = W_COL_LR + 2 * GLA_RANK
COL_QG = 0
COL_KG = 1024
COL_VG = 2048
COL_RG = 4096
P1_COLS = 6144
COL_GA = 0
COL_GB = 2048
LR_COLS = 128

V7X_VMEM_LIMIT = 56 * 1024 * 1024
NEG = -1e30

PROJ_TM = 2048
PROJ_TM_SPLIT = 1024
PERM_ROWS = 256
SIDE_CAST_BLOCKS = 16
ATTN_HALF = 64
ATTN_TQ = 128
ATTN_CHAINS = 32
GLA_BLOCK = 256
GLA_SUB = 64
GLA_HB = 4
MERGE_PIECES = 4


def _dot(a, b):
    return jnp.dot(a, b, preferred_element_type=F32)


def _dot_nt(a, b):
    return lax.dot_general(a, b, (((1,), (1,)), ((), ())), preferred_element_type=F32)


def _dot_tn(a, b):
    return lax.dot_general(a, b, (((0,), (0,)), ((), ())), preferred_element_type=F32)


def _split_bf16(x):
    hi = x.astype(BF16)
    lo = (x - hi.astype(F32)).astype(BF16)
    return hi, lo


def _xnorm_kernel(x_ref, gain_ref, wlr_ref, xn_ref, lr_ref):
    x = x_ref[...]
    ms = jnp.mean(x * x, axis=-1, keepdims=True)
    xn = (x * lax.rsqrt(ms + EPS) * gain_ref[...]).astype(BF16)
    xn_ref[...] = xn
    lr_ref[...] = _dot_nt(xn, wlr_ref[...].astype(BF16))


def _xnorm(x, gain, w_t, *, tm=512):
    t = x.shape[0]
    return pl.pallas_call(
        _xnorm_kernel,
        grid=(t // tm,),
        in_specs=[
            pl.BlockSpec((tm, D_MODEL), lambda i: (i, 0)),
            pl.BlockSpec((1, D_MODEL), lambda i: (0, 0)),
            pl.BlockSpec((pl.Element(LR_COLS), pl.Element(D_MODEL)), lambda i: (W_COL_LR, 0)),
        ],
        out_specs=[
            pl.BlockSpec((tm, D_MODEL), lambda i: (i, 0)),
            pl.BlockSpec((tm, LR_COLS), lambda i: (i, 0)),
        ],
        out_shape=[
            jax.ShapeDtypeStruct((t, D_MODEL), BF16),
            jax.ShapeDtypeStruct((t, LR_COLS), F32),
        ],
        compiler_params=pltpu.CompilerParams(
            dimension_semantics=("parallel",),
            vmem_limit_bytes=V7X_VMEM_LIMIT),
        name="xnorm",
    )(x, gain, w_t)


def _proj_kernel(*refs, r, side_cast):
    if side_cast:
        xn_ref, w_ref, cast_in_ref, o_ref, cast_out_ref, wb_s = refs
        cast_out_ref[...] = cast_in_ref[...].astype(BF16)
    else:
        xn_ref, w_ref, o_ref, wb_s = refs

    @pl.when(pl.program_id(1) == 0)
    def _():
        wb_s[...] = w_ref[...].astype(BF16)

    acc = _dot_nt(xn_ref[...], wb_s[...]).astype(BF16)
    if r == 1:
        o_ref[...] = acc
        return
    n = PERM_ROWS // r
    dst = lax.broadcasted_iota(jnp.int32, (PERM_ROWS, PERM_ROWS), 0)
    src = lax.broadcasted_iota(jnp.int32, (PERM_ROWS, PERM_ROWS), 1)
    perm = jnp.where(src == (dst % n) * r + dst // n, 1.0, 0.0).astype(BF16)
    for g in range(acc.shape[0] // PERM_ROWS):
        grouped = _dot(perm, acc[g * PERM_ROWS:(g + 1) * PERM_ROWS, :]).astype(BF16)
        for c in range(r):
            o_ref[0, c, g * n:(g + 1) * n, :] = grouped[c * n:(c + 1) * n, :]


def _proj(xn, w_t, row_start, ncol_blocks, r, B, S, name, *, tm=1024, tn=1024, side_cast=None):
    t = xn.shape[0]
    per_b = S // tm
    ntiles = t // tm
    if r == 1:
        out_specs = [pl.BlockSpec((tm, tn), lambda j, i: (i, j))]
        out_shapes = [jax.ShapeDtypeStruct((t, ncol_blocks * tn), BF16)]
    else:
        out_specs = [pl.BlockSpec((1, r, tm // r, tn), lambda j, i: (i // per_b, 0, i % per_b, j))]
        out_shapes = [jax.ShapeDtypeStruct((B, r, S // r, ncol_blocks * tn), BF16)]
    in_specs = [
        pl.BlockSpec((tm, D_MODEL), lambda j, i: (i, 0)),
        pl.BlockSpec((pl.Element(tn), pl.Element(D_MODEL)), lambda j, i: (pl.multiple_of(row_start(j), 8), 0)),
    ]
    operands = [xn, w_t]
    if side_cast is not None:
        w, axis = side_cast
        assert ncol_blocks * ntiles >= SIDE_CAST_BLOCKS
        blk = list(w.shape)
        blk[axis] //= SIDE_CAST_BLOCKS

        def cast_map(j, i):
            idx = [0, 0]
            idx[axis] = jnp.minimum(j * ntiles + i, SIDE_CAST_BLOCKS - 1)
            return tuple(idx)

        in_specs.append(pl.BlockSpec(tuple(blk), cast_map))
        out_specs.append(pl.BlockSpec(tuple(blk), cast_map))
        out_shapes.append(jax.ShapeDtypeStruct(w.shape, BF16))
        operands.append(w)
    outs = pl.pallas_call(
        functools.partial(_proj_kernel, r=r, side_cast=side_cast is not None),
        grid=(ncol_blocks, ntiles),
        in_specs=in_specs,
        out_specs=out_specs,
        out_shape=out_shapes,
        scratch_shapes=[pltpu.VMEM((tn, D_MODEL), BF16)],
        compiler_params=pltpu.CompilerParams(
            dimension_semantics=("arbitrary", "arbitrary"),
            vmem_limit_bytes=V7X_VMEM_LIMIT),
        name=name,
    )(*operands)
    return outs[0] if side_cast is None else tuple(outs)


def _attn_kernel(q_ref, k_ref, v_ref, cos_ref, sin_ref, qg_ref, kg_ref,
                 o_ref, ml_ref, qs, ks, vs, bias_s, *, L, hb, U):
    TQ, HALF = ATTN_TQ, ATTN_HALF
    TK = TQ + 2 * HALF
    NT = L // TQ
    NI = NT // U
    hblk = pl.program_id(2)

    @pl.when(hblk == 0)
    def _():
        ml_ref[...] = jnp.zeros_like(ml_ref)

    ri = lax.broadcasted_iota(jnp.int32, (TQ, TK), 0)
    ci = lax.broadcasted_iota(jnp.int32, (TQ, TK), 1)
    d = ci - ri
    band = jnp.where(d < 0, NEG, jnp.where(d > 2 * HALF, NEG, 0.0)).astype(F32)
    first = jnp.where(ci < HALF, NEG, band)
    bias_s[0] = band
    bias_s[1] = first
    bias_s[2] = jnp.where(ci >= TQ + HALF, NEG, band)
    bias_s[3] = jnp.where(ci >= TQ + HALF, NEG, first)

    a = lax.broadcasted_iota(jnp.int32, (HEAD_DIM, HEAD_DIM), 0)
    b = lax.broadcasted_iota(jnp.int32, (HEAD_DIM, HEAD_DIM), 1)
    half = ROT_DIM // 2
    ones_m = jnp.ones((HEAD_DIM, HEAD_DIM), BF16)
    rot_m = jnp.where((b < half) & (a == b + half), -1.0,
                      jnp.where((b >= half) & (b < ROT_DIM) & (a == b - half), 1.0, 0.0)).astype(BF16)

    zpad = jnp.zeros((HALF, HEAD_DIM), BF16)
    ztail = jnp.zeros((TQ, HEAD_DIM), BF16)
    for hh in range(hb):
        ks[hh, 0:HALF, :] = zpad
        ks[hh, L + HALF:L + HALF + TQ, :] = ztail
        vs[hh, 0:HALF, 0:HEAD_DIM] = zpad
        vs[hh, L + HALF:L + HALF + TQ, 0:HEAD_DIM] = ztail
        vs[hh, :, HEAD_DIM:2 * HEAD_DIM] = jnp.ones((vs.shape[1], HEAD_DIM), BF16)

    qgain = qg_ref[...] * (HEAD_DIM ** -0.5)
    kgain = kg_ref[...]

    def norm_rope(x, gain, cos, sin):
        ssq = _dot((x * x).astype(BF16), ones_m)
        xn = x * lax.rsqrt(ssq * (1.0 / HEAD_DIM) + EPS) * gain
        return xn * cos + _dot(xn.astype(BF16), rot_m) * sin

    def prep(blk):
        if isinstance(blk, int):
            src, dst, koff = min(blk, NT - 1) * TQ, blk * TQ, blk * TQ + HALF
        else:
            src = pl.multiple_of(jnp.minimum(blk, NT - 1) * TQ, TQ)
            dst = pl.multiple_of(blk * TQ, TQ)
            koff = pl.multiple_of(dst + HALF, HALF)
        rows = pl.ds(src, TQ)
        qdst = pl.ds(dst, TQ)
        kdst = pl.ds(koff, TQ)
        cos, sin = cos_ref[rows, :], sin_ref[rows, :]
        for hh in range(hb):
            hs = slice(hh * HEAD_DIM, (hh + 1) * HEAD_DIM)
            qs[hh, qdst, :] = norm_rope(q_ref[0, 0, rows, hs].astype(F32), qgain, cos, sin).astype(BF16)
            ks[hh, kdst, :] = norm_rope(k_ref[0, 0, rows, hs].astype(F32), kgain, cos, sin).astype(BF16)
            vs[hh, kdst, 0:HEAD_DIM] = v_ref[0, 0, rows, hs]

    lane = lax.broadcasted_iota(jnp.int32, (TQ, HEAD_DIM), 1)

    def score_tiles(tt):
        for u in range(U):
            t = tt * U + u
            q0 = t * TQ if isinstance(t, int) else pl.multiple_of(t * TQ, TQ)
            qrows = pl.ds(q0, TQ)
            krows = pl.ds(q0, TK)
            bias = bias_s[jnp.where(t == 0, 1, 0) + jnp.where(t == NT - 1, 2, 0)]
            ml_tile = ml_ref[0, 0, qrows, :]
            for hh in range(hb):
                hs = slice(hh * HEAD_DIM, (hh + 1) * HEAD_DIM)
                h = hblk * hb + hh
                s = _dot_nt(qs[hh, qrows, :], ks[hh, krows, :]) + bias
                m = jnp.max(s, axis=-1, keepdims=True)
                p = jnp.exp(s - m).astype(BF16)
                acc = _dot(p, vs[hh, krows, :])
                o_ref[0, 0, qrows, hs] = acc[:, :HEAD_DIM].astype(BF16)
                ml_tile = jnp.where(lane == h, m, jnp.where(lane == HEADS_PER_GROUP + h, acc[:, HEAD_DIM:], ml_tile))
            ml_ref[0, 0, qrows, :] = ml_tile

    for blk in range(min(U + 1, NT)):
        prep(blk)

    def body(tt, carry):
        score_tiles(tt)
        for u in range(U):
            prep((tt + 1) * U + 1 + u)
        return carry

    lax.fori_loop(0, NI - 1, body, 0)
    score_tiles(NI - 1)


def _attention_group(a_g, tables, q_gain, k_gain, g, B, S):
    window, r = ATTN_GROUPS[g]
    assert window // (2 * r) == ATTN_HALF
    L = S // r
    assert L % ATTN_TQ == 0
    hb = max(1, min(HEADS_PER_GROUP, 8192 // L))
    nhb = HEADS_PER_GROUP // hb
    bw = hb * HEAD_DIM
    nt = L // ATTN_TQ
    U = max(1, min(ATTN_CHAINS // hb, nt // 2 if nt >= 4 else nt))
    assert (L // ATTN_TQ) % U == 0
    cos, sin = (t[:S].reshape(L, r * HEAD_DIM) for t in tables)

    def col_map(part):
        return lambda b, c, h: (b, c, 0, part * nhb + h)

    tab_spec = pl.BlockSpec((L, HEAD_DIM), lambda b, c, h: (0, c))
    gain_spec = pl.BlockSpec((1, HEAD_DIM), lambda b, c, h: (0, 0))
    return pl.pallas_call(
        functools.partial(_attn_kernel, L=L, hb=hb, U=U),
        grid=(B, r, nhb),
        in_specs=[
            pl.BlockSpec((1, 1, L, bw), col_map(0)),
            pl.BlockSpec((1, 1, L, bw), col_map(1)),
            pl.BlockSpec((1, 1, L, bw), col_map(2)),
            tab_spec, tab_spec, gain_spec, gain_spec,
        ],
        out_specs=[
            pl.BlockSpec((1, 1, L, bw), lambda b, c, h: (b, c, 0, h)),
            pl.BlockSpec((1, 1, L, HEAD_DIM), lambda b, c, h: (b, c, 0, 0)),
        ],
        out_shape=[
            jax.ShapeDtypeStruct((B, r, L, ATTN_OUT), BF16),
            jax.ShapeDtypeStruct((B, r, L, HEAD_DIM), F32),
        ],
        scratch_shapes=[
            pltpu.VMEM((hb, L + ATTN_TQ, HEAD_DIM), BF16),
            pltpu.VMEM((hb, L + ATTN_TQ + ATTN_HALF, HEAD_DIM), BF16),
            pltpu.VMEM((hb, L + ATTN_TQ + ATTN_HALF, 2 * HEAD_DIM), BF16),
            pltpu.VMEM((4, ATTN_TQ, ATTN_TQ + 2 * ATTN_HALF), F32),
        ],
        compiler_params=pltpu.CompilerParams(
            dimension_semantics=("parallel", "parallel", "arbitrary"),
            vmem_limit_bytes=V7X_VMEM_LIMIT),
        name=f"attn_g{g}",
    )(a_g, a_g, a_g, cos, sin, q_gain[g:g + 1], k_gain[g:g + 1])


def _gla_kernel(qf_ref, kf_ref, vf_ref, lrf_ref, qb_ref, kb_ref, vb_ref, lrb_ref, wg_ref, bg_ref,
                of_ref, ob_ref, stf, stb):
    C, SC = GLA_BLOCK, GLA_SUB
    NS = C // SC
    n = pl.program_id(2)

    @pl.when(n == 0)
    def _():
        stf[...] = jnp.zeros_like(stf)
        stb[...] = jnp.zeros_like(stb)

    ri = lax.broadcasted_iota(jnp.int32, (C, C), 0)
    ci = lax.broadcasted_iota(jnp.int32, (C, C), 1)

    def direction(q_ref, k_ref, v_ref, lr_ref, o_ref, st, d, backward, h):
        ks = slice(h * GLA_DK, (h + 1) * GLA_DK)
        vs = slice(h * GLA_DV, (h + 1) * GLA_DV)
        z = _dot(lr_ref[0].astype(BF16), wg_ref[d, :, ks]) + bg_ref[d, :, ks]
        yield
        log_sig = jnp.minimum(z, 0.0) - jnp.log(1.0 + jnp.exp(-jnp.abs(z)))
        g_hi, g_lo = _split_bf16(log_sig * (1.0 / GLA_NORMALIZER))
        tri = jnp.where((ci >= ri) if backward else (ri >= ci), 1.0, 0.0).astype(BF16)
        cum = _dot(tri, g_hi) + _dot(tri, g_lo)
        yield
        mid_row = SC // 2 if backward else SC // 2 - 1
        end_row = 0 if backward else C - 1
        mids = [cum[I * SC + mid_row:I * SC + mid_row + 1, :] for I in range(NS)]
        end = cum[end_row:end_row + 1, :]
        sub = [slice(I * SC, (I + 1) * SC) for I in range(NS)]
        dl = jnp.concatenate([cum[sub[I], :] - mids[I] for I in range(NS)], axis=0)
        qd = q_ref[0, :, ks].astype(F32) * (GLA_DK ** -0.5) * jnp.exp(dl)
        kd = k_ref[0, :, ks].astype(F32) * jnp.exp(-dl)
        qd_b = qd.astype(BF16)
        kd_b = kd.astype(BF16)
        att_rows = []
        for I in range(NS):
            blocks = []
            for J in range(NS):
                if (J > I) if backward else (J < I):
                    blocks.append((kd[sub[J], :] * jnp.exp(mids[I] - mids[J])).astype(BF16))
                else:
                    blocks.append(kd_b[sub[J], :])
            att_rows.append(_dot_nt(qd_b[sub[I], :], jnp.concatenate(blocks, axis=0)))
        yield
        mask = (ci > ri) if backward else (ri >= ci)
        att = jnp.where(mask, jnp.concatenate(att_rows, axis=0), 0.0).astype(BF16)
        qi = jnp.concatenate([qd[sub[I], :] * jnp.exp(mids[I]) for I in range(NS)], axis=0).astype(BF16)
        k2 = jnp.concatenate([kd[sub[I], :] * jnp.exp(end - mids[I]) for I in range(NS)], axis=0).astype(BF16)
        v = v_ref[0, :, vs]
        state = st[h]
        o = _dot(att, v) + _dot_nt(qi, state.astype(BF16))
        upd = _dot_tn(v, k2)
        yield
        o_ref[0, :, vs] = o.astype(BF16)
        st[h] = state * jnp.exp(end) + upd

    chains = []
    for h in range(GLA_HB):
        chains.append(direction(qf_ref, kf_ref, vf_ref, lrf_ref, of_ref, stf, 0, False, h))
        chains.append(direction(qb_ref, kb_ref, vb_ref, lrb_ref, ob_ref, stb, 1, True, h))
    while chains:
        alive = []
        for chain in chains:
            if next(chain, chain) is not chain:
                alive.append(chain)
        chains = alive


def _gla(p3, lr3, wg_pad, bg, B, S):
    TS = GLA_BLOCK
    NB = S // TS
    HB = GLA_HB
    kq, kk, kv = COL_QG // (HB * GLA_DK), COL_KG // (HB * GLA_DK), COL_VG // (HB * GLA_DV)

    def fwd(base):
        return lambda b, h, n: (b, n, base + h)

    def bwd(base):
        return lambda b, h, n: (b, NB - 1 - n, base + h)

    def specs(m, lr_map):
        return [
            pl.BlockSpec((1, TS, HB * GLA_DK), m(kq)),
            pl.BlockSpec((1, TS, HB * GLA_DK), m(kk)),
            pl.BlockSpec((1, TS, HB * GLA_DV), m(kv)),
            pl.BlockSpec((1, TS, LR_COLS), lr_map),
        ]

    return pl.pallas_call(
        _gla_kernel,
        grid=(B, GLA_HEADS // HB, NB),
        in_specs=specs(fwd, lambda b, h, n: (b, n, 0)) + specs(bwd, lambda b, h, n: (b, NB - 1 - n, 0)) + [
            pl.BlockSpec((2, LR_COLS, HB * GLA_DK), lambda b, h, n: (0, 0, h)),
            pl.BlockSpec((2, 1, HB * GLA_DK), lambda b, h, n: (0, 0, h)),
        ],
        out_specs=[
            pl.BlockSpec((1, TS, HB * GLA_DV), lambda b, h, n: (b, n, h)),
            pl.BlockSpec((1, TS, HB * GLA_DV), lambda b, h, n: (b, NB - 1 - n, h)),
        ],
        out_shape=[jax.ShapeDtypeStruct((B, S, GLA_VAL), BF16)] * 2,
        scratch_shapes=[
            pltpu.VMEM((HB, GLA_DV, GLA_DK), F32),
            pltpu.VMEM((HB, GLA_DV, GLA_DK), F32),
        ],
        compiler_params=pltpu.CompilerParams(
            dimension_semantics=("parallel", "parallel", "arbitrary"),
            vmem_limit_bytes=V7X_VMEM_LIMIT),
        name="gla",
    )(p3, p3, p3, lr3, p3, p3, p3, lr3, wg_pad, bg)


def _sigmoid(x):
    return 1.0 / (1.0 + jnp.exp(-x))


def _merge_kernel(o0_ref, o1_ref, o2_ref, l0_ref, l1_ref, l2_ref, of_ref, ob_ref, rg_ref, ga_ref, gb_ref,
                  x_ref, wba_ref, wbg_ref, wout_ref, gnorm_ref, nffn_ref, h_ref, hn_ref,
                  oa_s, og_s, oil_s, lil_s):
    tm = x_ref.shape[0]
    for gi, (o_ref, l_ref) in enumerate(((o1_ref, l1_ref), (o2_ref, l2_ref))):
        r = o_ref.shape[1]
        for c in range(r):
            dst = pl.ds(c, tm // r, stride=r)
            lil_s[gi, dst, :] = l_ref[0, c]
            for hh in range(HEADS_PER_GROUP):
                oil_s[gi, hh, dst, :] = o_ref[0, c, :, hh * HEAD_DIM:(hh + 1) * HEAD_DIM].astype(F32)

    mls = (l0_ref[0, 0], lil_s[0], lil_s[1])
    m = jnp.maximum(jnp.maximum(mls[0], mls[1]), mls[2])
    es = [jnp.exp(ml - m) for ml in mls]
    dens = [pltpu.roll(ml, HEAD_DIM - HEADS_PER_GROUP, 1) for ml in mls]
    inv = 1.0 / (es[0] * dens[0] + es[1] * dens[1] + es[2] * dens[2])
    w0, w1, w2 = es[0] * inv, es[1] * inv, es[2] * inv
    for hh in range(HEADS_PER_GROUP):
        hs = slice(hh * HEAD_DIM, (hh + 1) * HEAD_DIM)
        comb = (w0[:, hh:hh + 1] * o0_ref[0, 0, :, hs].astype(F32)
                + w1[:, hh:hh + 1] * oil_s[0, hh]
                + w2[:, hh:hh + 1] * oil_s[1, hh])
        oa_s[:, hs] = comb.astype(BF16)

    for h in range(GLA_HEADS):
        vs = slice(h * GLA_DV, (h + 1) * GLA_DV)
        og = of_ref[:, vs].astype(F32) + ob_ref[:, vs].astype(F32)
        ms = jnp.mean(og * og, axis=-1, keepdims=True)
        ogn = og * lax.rsqrt(ms + EPS) * gnorm_ref[...]
        rg = rg_ref[:, vs].astype(F32)
        og_s[:, vs] = (ogn * (rg * _sigmoid(rg))).astype(BF16)

    oa, og = oa_s[...], og_s[...]
    cw = D_MODEL // MERGE_PIECES
    u = [(_dot(oa, wba_ref[:, 0:cw]), _dot(og, wbg_ref[:, 0:cw]))]
    out = None
    for c in range(MERGE_PIECES):
        cs = slice(c * cw, (c + 1) * cw)
        if c + 1 < MERGE_PIECES:
            ns = slice((c + 1) * cw, (c + 2) * cw)
            u.append((_dot(oa, wba_ref[:, ns]), _dot(og, wbg_ref[:, ns])))
        merged = (_sigmoid(ga_ref[:, cs].astype(F32)) * u[c][0]
                  + _sigmoid(gb_ref[:, cs].astype(F32)) * u[c][1]).astype(BF16)
        part = _dot(merged, wout_ref[cs, :])
        out = part if out is None else out + part
    h = x_ref[...] + out
    h_ref[...] = h
    ms = jnp.mean(h * h, axis=-1, keepdims=True)
    hn_ref[...] = (h * lax.rsqrt(ms + EPS) * nffn_ref[...]).astype(BF16)


def _merge(o_groups, ml_groups, o_fwd, o_bwd, p1, p2, x, wba, wbg, wout, gnorm, nffn, B, S, *, tm=256):
    t = x.shape[0]
    per_b = S // tm
    row = lambda b, i: (b * per_b + i, 0)
    const = lambda b, i: (0, 0)

    def resident(shape):
        return pl.BlockSpec(shape, const, pipeline_mode=pl.Buffered(1))

    def split_spec(r, width):
        return pl.BlockSpec((1, r, tm // r, width), lambda b, i: (b, 0, i, 0))

    rs = [r for _, r in ATTN_GROUPS]
    return pl.pallas_call(
        _merge_kernel,
        grid=(B, per_b),
        in_specs=[split_spec(r, ATTN_OUT) for r in rs] + [split_spec(r, HEAD_DIM) for r in rs] + [
            pl.BlockSpec((tm, GLA_VAL), row),
            pl.BlockSpec((tm, GLA_VAL), row),
            pl.BlockSpec((tm, GLA_VAL), lambda b, i: (b * per_b + i, COL_RG // GLA_VAL)),
            pl.BlockSpec((tm, D_MODEL), lambda b, i: (b * per_b + i, COL_GA // D_MODEL)),
            pl.BlockSpec((tm, D_MODEL), lambda b, i: (b * per_b + i, COL_GB // D_MODEL)),
            pl.BlockSpec((tm, D_MODEL), row),
            resident((ATTN_OUT, D_MODEL)),
            resident((GLA_VAL, D_MODEL)),
            resident((D_MODEL, D_MODEL)),
            resident((1, GLA_DV)),
            resident((1, D_MODEL)),
        ],
        out_specs=[pl.BlockSpec((tm, D_MODEL), row), pl.BlockSpec((tm, D_MODEL), row)],
        out_shape=[jax.ShapeDtypeStruct((t, D_MODEL), F32), jax.ShapeDtypeStruct((t, D_MODEL), BF16)],
        scratch_shapes=[
            pltpu.VMEM((tm, ATTN_OUT), BF16),
            pltpu.VMEM((tm, GLA_VAL), BF16),
            pltpu.VMEM((2, HEADS_PER_GROUP, tm, HEAD_DIM), F32),
            pltpu.VMEM((2, tm, HEAD_DIM), F32),
        ],
        compiler_params=pltpu.CompilerParams(
            dimension_semantics=("parallel", "parallel"),
            vmem_limit_bytes=V7X_VMEM_LIMIT),
        name="merge",
    )(*o_groups, *ml_groups, o_fwd, o_bwd, p1, p2, p2, x, wba, wbg, wout, gnorm, nffn)


def _ffn_kernel(hn_ref, w1_ref, w2_ref, h_ref, y_ref):
    @pl.when(pl.program_id(1) == 0)
    def _():
        y_ref[...] = h_ref[...]

    a = _dot(hn_ref[...], w1_ref[...])
    a = jnp.square(jnp.maximum(a, 0.0)).astype(BF16)
    y_ref[...] += _dot(a, w2_ref[...])


def _ffn(hn, h, w1, w2, *, tm=512, tf=1024):
    t = hn.shape[0]
    return pl.pallas_call(
        _ffn_kernel,
        grid=(t // tm, D_FF // tf),
        in_specs=[
            pl.BlockSpec((tm, D_MODEL), lambda i, j: (i, 0)),
            pl.BlockSpec((D_MODEL, tf), lambda i, j: (0, j)),
            pl.BlockSpec((tf, D_MODEL), lambda i, j: (j, 0)),
            pl.BlockSpec((tm, D_MODEL), lambda i, j: (i, 0)),
        ],
        out_specs=pl.BlockSpec((tm, D_MODEL), lambda i, j: (i, 0)),
        out_shape=jax.ShapeDtypeStruct((t, D_MODEL), F32),
        compiler_params=pltpu.CompilerParams(
            dimension_semantics=("parallel", "arbitrary"),
            vmem_limit_bytes=V7X_VMEM_LIMIT),
        name="ffn",
    )(hn, w1, w2, h)


def _rope_tables(s_max):
    inv_freq = ROPE_THETA ** (-jnp.arange(0, ROT_DIM, 2, dtype=F32) / ROT_DIM)
    ang = jnp.arange(s_max, dtype=F32)[:, None] * inv_freq[None, :]
    cos, sin = jnp.cos(ang), jnp.sin(ang)
    rest = HEAD_DIM - ROT_DIM
    cos_t = jnp.concatenate([cos, cos, jnp.ones((s_max, rest), F32)], axis=1)
    sin_t = jnp.concatenate([sin, sin, jnp.zeros((s_max, rest), F32)], axis=1)
    return cos_t, sin_t


def _prepare_layer(w_in, w_gla_gate, b_gla_gate):
    w_t = jnp.swapaxes(w_in, 0, 1)
    wg_pad = jnp.zeros((2, LR_COLS, GLA_KEY), F32)
    wg_pad = wg_pad.at[0, 0:GLA_RANK].set(w_gla_gate[0].astype(F32))
    wg_pad = wg_pad.at[1, GLA_RANK:2 * GLA_RANK].set(w_gla_gate[1].astype(F32)).astype(BF16)
    bg = b_gla_gate.astype(F32).reshape(2, 1, GLA_KEY)
    return w_t, wg_pad, bg


def _layer(x3, tables, norm_mix, prepared, q_norm, k_norm, gla_norm, wba, wbg, wout, norm_ffn, ffn_w):
    B, S, _ = x3.shape
    T = B * S
    w_t, wg_pad, bg = prepared
    tn = 1024
    x = x3.reshape(T, D_MODEL)
    xn, lr = _xnorm(x, norm_mix.reshape(1, D_MODEL), w_t)
    ffn_w = list(ffn_w)
    cast_axis = (1, 0)
    o_groups, ml_groups = [], []
    for g in range(N_GROUPS):
        r = ATTN_GROUPS[g][1]
        pending = [k for k in range(2) if ffn_w[k].dtype != BF16]
        side = (ffn_w[pending[0]], cast_axis[pending[0]]) if r > 1 and pending else None
        a_g = _proj(xn, w_t, lambda j, g=g: j * ATTN_QKV + g * ATTN_OUT, 3, r, B, S, f"proj_attn_r{r}", tn=tn,
                    tm=PROJ_TM if r == 1 else PROJ_TM_SPLIT, side_cast=side)
        if side is not None:
            a_g, ffn_w[pending[0]] = a_g
        o_g, ml_g = _attention_group(a_g.reshape(B, r, S // r, 3 * ATTN_OUT), tables, q_norm, k_norm, g, B, S)
        o_groups.append(o_g)
        ml_groups.append(ml_g)
    p1 = _proj(xn, w_t, lambda j: W_COL_QG + j * tn, P1_COLS // tn, 1, B, S, "proj_gla", tn=tn, tm=PROJ_TM)
    p2 = _proj(xn, w_t, lambda j: W_COL_GATES + j * tn, 2 * D_MODEL // tn, 1, B, S, "proj_gates", tn=tn, tm=PROJ_TM)
    o_fwd, o_bwd = _gla(p1.reshape(B, S, P1_COLS), lr.reshape(B, S, LR_COLS), wg_pad, bg, B, S)
    h, hn = _merge(o_groups, ml_groups, o_fwd.reshape(T, GLA_VAL), o_bwd.reshape(T, GLA_VAL), p1, p2, x,
                   wba, wbg, wout, gla_norm.reshape(1, GLA_DV), norm_ffn.reshape(1, D_MODEL), B, S)
    y = _ffn(hn, h, ffn_w[0].astype(BF16), ffn_w[1].astype(BF16))
    return y.reshape(B, S, D_MODEL), tuple(ffn_w)


def kernel(x_prompt, x_sample, norm_mix, w_in, q_norm, k_norm, w_gla_gate, b_gla_gate, gla_norm,
           w_branch_attn, w_branch_gla, w_out, norm_ffn, w_ff1, w_ff2):
    depth = w_in.shape[0]
    tables = _rope_tables(max(x_prompt.shape[1], x_sample.shape[1]))
    layers = []
    for l in range(depth):
        layers.append((
            norm_mix[l], _prepare_layer(w_in[l], w_gla_gate[l], b_gla_gate[l]),
            q_norm[l].astype(F32), k_norm[l].astype(F32), gla_norm[l].astype(F32),
            w_branch_attn[l].astype(BF16), w_branch_gla[l].astype(BF16), w_out[l].astype(BF16),
            norm_ffn[l].astype(F32)))
    ffn_ws = [(w_ff1[l], w_ff2[l]) for l in range(depth)]
    outs = []
    for x in (x_prompt, x_sample):
        for l, layer in enumerate(layers):
            x, ffn_ws[l] = _layer(x, tables, *layer, ffn_ws[l])
        outs.append(x)
    return tuple(outs)
```
